```python
import jax, jax.numpy as jnp
from jax import lax
import numpy as np


D_MODEL = 2048
BATCH = 2
SEQ = 8192
DEPTH = 1

D_MIX = D_MODEL
REC_HEADS = 8
REC_DK = 128
REC_DV = 128
REC_WIDTH = REC_HEADS * REC_DV
REC_CHUNK = 64
ATT_HEADS = 8
ATT_DH = 128
ATT_KV_HEADS = 2
ATT_GROUP = ATT_HEADS // ATT_KV_HEADS
ATT_WIDTH = ATT_HEADS * ATT_DH
IDX_HEADS = 8
IDX_DIM = 64
TOPK_MAX = 256
Q_BLOCK = 128
N_GROUPS = 4
EXPERTS_PER_GROUP = 8
N_EXPERTS = N_GROUPS * EXPERTS_PER_GROUP
TOPK_IN_GROUP = 2
D_EXPERT = 512
EPS = 1e-6

IN_WIDTHS = (REC_HEADS * REC_DK,
             REC_HEADS * REC_DK,
             REC_WIDTH,
             REC_WIDTH,
             ATT_WIDTH,
             ATT_KV_HEADS * ATT_DH,
             ATT_KV_HEADS * ATT_DH,
             IDX_HEADS * IDX_DIM,
             IDX_DIM,
             IDX_HEADS)
IN_COLS = sum(IN_WIDTHS)

kernel_name = 'hymba_hgrn2_dsa_hmoe_adaln'


def _rmsnorm(x, g):
    xf = x.astype(jnp.float32)
    xf = xf * lax.rsqrt(jnp.mean(xf * xf, axis=-1, keepdims=True) + EPS)
    return xf.astype(x.dtype) * g


def _head_rmsnorm(o, g):
    B, S, H, Dh = o.shape
    of = o.astype(jnp.float32)
    of = of * lax.rsqrt(jnp.mean(of * of, axis=-1, keepdims=True) + EPS)
    return of.reshape(B, S, H * Dh) * g.astype(jnp.float32)


def _split_in(proj):
    pieces = []
    off = 0
    for w in IN_WIDTHS:
        pieces.append(proj[..., off:off + w])
        off += w
    return pieces


def _hgrn2_mixer(q, f_logit, inp, gate, lb, g_out):
    B, S, _ = q.shape
    nc = S // REC_CHUNK
    f32 = jnp.float32
    f = lb + (1.0 - lb) * jax.nn.sigmoid(f_logit.astype(f32))
    log_f = jnp.log(f)
    k = 1.0 - f
    qf = jax.nn.silu(q.astype(f32)) * REC_DK ** -0.5

    def chunked(t, d):
        return t.reshape(B, nc, REC_CHUNK, REC_HEADS, d).transpose(1, 0, 3, 2, 4)

    qc = chunked(qf, REC_DK)
    kc = chunked(k, REC_DK)
    vc = chunked(inp.astype(f32), REC_DV)
    bc = jnp.cumsum(chunked(log_f, REC_DK), axis=3)
    causal = jnp.tril(jnp.ones((REC_CHUNK, REC_CHUNK), dtype=bool))[:, :, None]

    def step(state, xs):
        q_, k_, v_, b_ = xs
        inter = jnp.einsum('bhtd,bhdv->bhtv', q_ * jnp.exp(b_), state)
        rel = jnp.where(causal, b_[:, :, :, None, :] - b_[:, :, None, :, :], -jnp.inf)
        scores = jnp.einsum('bhtd,bhtsd,bhsd->bhts', q_, jnp.exp(rel), k_)
        intra = jnp.einsum('bhts,bhsv->bhtv', scores, v_)
        b_end = b_[:, :, -1:, :]
        state = (jnp.exp(b_end[:, :, 0, :])[..., None] * state
                 + jnp.einsum('bhsd,bhsv->bhdv', k_ * jnp.exp(b_end - b_), v_))
        return state, inter + intra

    s0 = jnp.zeros((B, REC_HEADS, REC_DK, REC_DV), f32)
    _, o = lax.scan(step, s0, (qc, kc, vc, bc))
    o = o.transpose(1, 0, 3, 2, 4).reshape(B, S, REC_HEADS, REC_DV)
    o = _head_rmsnorm(o, g_out) * jax.nn.silu(gate.astype(f32))
    return o.astype(q.dtype)


def _dsa_mixer(q, k, v, q_idx, k_idx, w_idx, g_out):
    B, S, _ = q.shape
    f32 = jnp.float32
    k_sel = min(TOPK_MAX, S // 4)
    nb = S // Q_BLOCK
    qb_all = q.reshape(B, nb, Q_BLOCK, ATT_KV_HEADS, ATT_GROUP, ATT_DH)
    kh = k.reshape(B, S, ATT_KV_HEADS, ATT_DH)
    vh = v.reshape(B, S, ATT_KV_HEADS, ATT_DH)
    qi_all = q_idx.reshape(B, nb, Q_BLOCK, IDX_HEADS, IDX_DIM)
    wi_all = w_idx.reshape(B, nb, Q_BLOCK, IDX_HEADS)
    k_idx_f = k_idx.astype(f32)
    key_pos = jnp.arange(S)
    gather = jax.vmap(lambda kv, ix: kv[ix])

    def block(args):
        qb, qib, wb, blk = args
        t = blk * Q_BLOCK + jnp.arange(Q_BLOCK)
        causal = key_pos[None, :] <= t[:, None]
        rel = jax.nn.relu(jnp.einsum('bqhd,bsd->bqhs', qib.astype(f32), k_idx_f) * IDX_DIM ** -0.5)
        score = jnp.einsum('bqhs,bqh->bqs', rel, wb.astype(f32) * IDX_HEADS ** -0.5)
        score = jnp.where(causal[None], score, -jnp.inf)
        _, idx = lax.top_k(score, k_sel)
        valid = idx <= t[None, :, None]
        kg = gather(kh, idx)
        vg = gather(vh, idx)
        logits = jnp.einsum('bqhgd,bqnhd->bqhgn', qb, kg).astype(f32) * ATT_DH ** -0.5
        logits = jnp.where(valid[:, :, None, None, :], logits, -jnp.inf)
        p = jax.nn.softmax(logits, axis=-1).astype(v.dtype)
        return jnp.einsum('bqhgn,bqnhd->bqhgd', p, vg)

    xs = (jnp.moveaxis(qb_all, 1, 0), jnp.moveaxis(qi_all, 1, 0),
          jnp.moveaxis(wi_all, 1, 0), jnp.arange(nb))
    out = lax.map(block, xs)
    out = jnp.moveaxis(out, 0, 1).reshape(B, S, ATT_HEADS, ATT_DH)
    return _head_rmsnorm(out, g_out).astype(q.dtype)


def _hier_moe(h, w_rg, b_rg, w_re, b_re, w_gate, w_up, w_down):
    B, S, D = h.shape
    f32 = jnp.float32
    xt = h.reshape(B * S, D)
    n = xt.shape[0]
    g_prob = jax.nn.softmax((xt @ w_rg + b_rg).astype(f32), axis=-1)
    p_g, g_idx = lax.top_k(g_prob, 1)
    e_logits = (xt @ w_re + b_re).astype(f32).reshape(n, N_GROUPS, EXPERTS_PER_GROUP)
    e_in_group = jnp.take_along_axis(e_logits, g_idx[:, :, None], axis=1)[:, 0]
    top_v, top_i = lax.top_k(e_in_group, TOPK_IN_GROUP)
    w_sel = jax.nn.softmax(top_v, axis=-1) * p_g
    expert_id = g_idx * EXPERTS_PER_GROUP + top_i
    combine = jnp.sum(jax.nn.one_hot(expert_id, N_EXPERTS, dtype=f32) * w_sel[..., None], axis=1)
    combine = combine.astype(xt.dtype)
    y = jnp.zeros_like(xt)
    for e in range(N_EXPERTS):
        he = jax.nn.silu(xt @ w_gate[e]) * (xt @ w_up[e])
        y = y + combine[:, e:e + 1] * (he @ w_down[e])
    return y.reshape(B, S, D)


def setup_inputs(seed: int = 0) -> dict:
    key = jax.random.key(seed)
    ks = jax.random.split(key, 20)
    f32 = jnp.float32
    nrm = lambda k, shape, s: jax.random.normal(k, shape, f32) * s
    return {
        'x': nrm(ks[0], (BATCH, SEQ, D_MODEL), 1.0),
        'c': nrm(ks[1], (BATCH, D_MODEL), 1.0),
        'w_ada': nrm(ks[2], (DEPTH, D_MODEL, 6 * D_MODEL), 0.5 * D_MODEL ** -0.5),
        'b_ada': nrm(ks[3], (DEPTH, 6 * D_MODEL), 0.02),
        'g_norm_mix': 1.0 + nrm(ks[4], (DEPTH, D_MODEL), 0.02),
        'w_in': nrm(ks[5], (DEPTH, D_MODEL, IN_COLS), D_MODEL ** -0.5),
        'lb_logits': nrm(ks[6], (DEPTH + 1, REC_HEADS * REC_DK), 0.5),
        'g_rec_out': 1.0 + nrm(ks[7], (DEPTH, REC_WIDTH), 0.02),
        'g_att_out': 1.0 + nrm(ks[8], (DEPTH, ATT_WIDTH), 0.02),
        'w_out': nrm(ks[9], (DEPTH, D_MIX, D_MODEL), D_MIX ** -0.5),
        'g_norm_ffn': 1.0 + nrm(ks[10], (DEPTH, D_MODEL), 0.02),
        'w_router_group': nrm(ks[11], (DEPTH, D_MODEL, N_GROUPS), D_MODEL ** -0.5),
        'b_router_group': nrm(ks[12], (DEPTH, N_GROUPS), 0.01),
        'w_router_expert': nrm(ks[13], (DEPTH, D_MODEL, N_EXPERTS), D_MODEL ** -0.5),
        'b_router_expert': nrm(ks[14], (DEPTH, N_EXPERTS), 0.01),
        'w_expert_gate': nrm(ks[15], (DEPTH, N_EXPERTS, D_MODEL, D_EXPERT), D_MODEL ** -0.5),
        'w_expert_up': nrm(ks[16], (DEPTH, N_EXPERTS, D_MODEL, D_EXPERT), D_MODEL ** -0.5),
        'w_expert_down': nrm(ks[17], (DEPTH, N_EXPERTS, D_EXPERT, D_MODEL), D_EXPERT ** -0.5),
        'g_final': 1.0 + nrm(ks[18], (D_MODEL,), 0.02),
    }


def reference(x, c, w_ada, b_ada, g_norm_mix, w_in, lb_logits, g_rec_out, g_att_out, w_out,
              g_norm_ffn, w_router_group, b_router_group, w_router_expert, b_router_expert,
              w_expert_gate, w_expert_up, w_expert_down, g_final):
    lower_bounds = jnp.cumsum(jax.nn.softmax(lb_logits.astype(jnp.float32), axis=0), axis=0)
    c_act = jax.nn.silu(c)
    for layer in range(DEPTH):
        mod = (c_act @ w_ada[layer] + b_ada[layer])[:, None, :]
        sh1, sc1, gt1, sh2, sc2, gt2 = jnp.split(mod, 6, axis=-1)

        h = _rmsnorm(x, g_norm_mix[layer]) * (1.0 + sc1) + sh1
        (r_q, r_f, r_i, r_g, a_q, a_k, a_v, i_q, i_k, i_w) = _split_in(h @ w_in[layer])
        rec = _hgrn2_mixer(r_q, r_f, r_i, r_g, lower_bounds[layer], g_rec_out[layer])
        att = _dsa_mixer(a_q, a_k, a_v, i_q, i_k, i_w, g_att_out[layer])
        mixed = jnp.concatenate([rec, att], axis=-1) @ w_out[layer]
        x = x + gt1 * mixed

        h = _rmsnorm(x, g_norm_ffn[layer]) * (1.0 + sc2) + sh2
        y = _hier_moe(h, w_router_group[layer], b_router_group[layer], w_router_expert[layer],
                      b_router_expert[layer], w_expert_gate[layer], w_expert_up[layer],
                      w_expert_down[layer])
        x = x + gt2 * y
    return _rmsnorm(x, g_final)
```

```python
import functools

import jax
import jax.numpy as jnp
import numpy as np
from jax import lax
from jax.experimental import pallas as pl
from jax.experimental.pallas import tpu as pltpu

F32 = jnp.float32
BF16 = jnp.bfloat16
I32 = jnp.int32

EPS = 1e-6
LANES = 128

REC_HEADS = 8
REC_D = 128
REC_CHUNK = 64
REC_SUB = 16
ATT_HEADS = 8
ATT_DH = 128
ATT_KV_HEADS = 2
ATT_GROUP = ATT_HEADS // ATT_KV_HEADS
IDX_HEADS = 8
IDX_DIM = 64
TOPK_MAX = 256
N_GROUPS = 4
EXPERTS_PER_GROUP = 8
N_EXPERTS = N_GROUPS * EXPERTS_PER_GROUP
D_EXPERT = 512

OFF_RQ = 0
OFF_RF = 1024
OFF_RI = 2048
OFF_RG = 3072
OFF_AQ = 4096
OFF_AK = 5120
OFF_AV = 5376
OFF_IQ = 5632
OFF_IK = 6144
OFF_IW = 6208
IN_COLS = 6216
IN_PAD = 6272

VMEM_LIMIT = 48 * 1024 * 1024

INT_MIN = -(2 ** 31)
KEY_NEGINF = int(np.array(-np.inf, np.float32).view(np.int32)) ^ 0x7FFFFFFF
NEG_BIG = -1e30
EXP_CLAMP = 80.0


def _silu(v):
    return v * jax.nn.sigmoid(v)


def _nt_dot(a, b):
    return lax.dot_general(a, b, (((1,), (1,)), ((), ())), preferred_element_type=F32)


def _tn_dot(a, b):
    return lax.dot_general(a, b, (((0,), (0,)), ((), ())), preferred_element_type=F32)


def _adaln_kernel(c_ref, w_ref, b_ref, o_ref):
    ca = _silu(c_ref[...])
    o_ref[...] = jnp.dot(ca, w_ref[...], preferred_element_type=F32,
                         precision=lax.Precision.HIGHEST) + b_ref[...]


def _adaln(c, w, b):
    bsz, d = c.shape
    n = w.shape[1]
    tn = 512
    return pl.pallas_call(
        _adaln_kernel,
        out_shape=jax.ShapeDtypeStruct((bsz, n), F32),
        grid=(n // tn,),
        in_specs=[pl.BlockSpec((bsz, d), lambda j: (0, 0)),
                  pl.BlockSpec((d, tn), lambda j: (0, j)),
                  pl.BlockSpec((1, tn), lambda j: (0, j))],
        out_specs=pl.BlockSpec((bsz, tn), lambda j: (0, j)),
        compiler_params=pltpu.CompilerParams(dimension_semantics=("arbitrary",),
                                             vmem_limit_bytes=VMEM_LIMIT),
        name="adaln",
    )(c, w, b.reshape(1, n))


def _norm_mod(x, g, sc, sh):
    xn = x * lax.rsqrt(jnp.mean(x * x, axis=-1, keepdims=True) + EPS)
    return xn * g * (1.0 + sc) + sh


def _inproj_kernel(x_ref, sc_ref, sh_ref, g_ref, w_ref, o_ref, h_ref):
    @pl.when(pl.program_id(1) == 0)
    def _():
        h_ref[...] = _norm_mod(x_ref[...], g_ref[...], sc_ref[0], sh_ref[0]).astype(BF16)

    o_ref[...] = jnp.dot(h_ref[...], w_ref[...], preferred_element_type=F32)


def _inproj(x2, sc, sh, g, w_bf, seq):
    n, d = x2.shape
    ncol = w_bf.shape[1]
    tm = min(512, seq)
    tn = 896
    per_b = seq // tm
    return pl.pallas_call(
        _inproj_kernel,
        out_shape=jax.ShapeDtypeStruct((n, ncol), F32),
        grid=(n // tm, ncol // tn),
        in_specs=[pl.BlockSpec((tm, d), lambda i, j: (i, 0)),
                  pl.BlockSpec((1, 1, d), lambda i, j: (i // per_b, 0, 0)),
                  pl.BlockSpec((1, 1, d), lambda i, j: (i // per_b, 0, 0)),
                  pl.BlockSpec((1, d), lambda i, j: (0, 0)),
                  pl.BlockSpec((d, tn), lambda i, j: (0, j))],
        out_specs=pl.BlockSpec((tm, tn), lambda i, j: (i, j)),
        scratch_shapes=[pltpu.VMEM((tm, d), BF16)],
        compiler_params=pltpu.CompilerParams(dimension_semantics=("arbitrary", "arbitrary"),
                                             vmem_limit_bytes=VMEM_LIMIT),
        name="inproj",
    )(x2, sc, sh, g, w_bf)


def _kvprep_kernel(kv_ref, ik_ref, k_ref, v_ref, kx_ref):
    kv = kv_ref[...]
    k_ref[...] = kv[:, :256].astype(BF16)
    v_ref[...] = kv[:, 256:].astype(BF16)
    kx_ref[...] = ik_ref[...][:, :IDX_DIM].astype(BF16)


def _kvprep(proj):
    n = proj.shape[0]
    tm = 512
    return pl.pallas_call(
        _kvprep_kernel,
        out_shape=(jax.ShapeDtypeStruct((n, 256), BF16),
                   jax.ShapeDtypeStruct((n, 256), BF16),
                   jax.ShapeDtypeStruct((n, IDX_DIM), BF16)),
        grid=(n // tm,),
        in_specs=[pl.BlockSpec((tm, 512), lambda i: (i, OFF_AK // 512)),
                  pl.BlockSpec((tm, LANES), lambda i: (i, OFF_IK // LANES))],
        out_specs=(pl.BlockSpec((tm, 256), lambda i: (i, 0)),
                   pl.BlockSpec((tm, 256), lambda i: (i, 0)),
                   pl.BlockSpec((tm, IDX_DIM), lambda i: (i, 0))),
        compiler_params=pltpu.CompilerParams(dimension_semantics=("arbitrary",),
                                             vmem_limit_bytes=VMEM_LIMIT),
        name="kvprep",
    )(proj, proj)


def _hgrn_kernel(q_ref, f_ref, i_ref, g_ref, lbl_ref, gout_ref, o_ref, st_ref, *, chunks, layer):
    @pl.when(pl.program_id(2) == 0)
    def _():
        st_ref[...] = jnp.zeros_like(st_ref)

    lbl = lbl_ref[...]
    e = jnp.exp(lbl - jnp.max(lbl, axis=0, keepdims=True))
    sm = e / jnp.sum(e, axis=0, keepdims=True)
    lb = jnp.sum(sm[: layer + 1], axis=0, keepdims=True)
    gout = gout_ref[...]

    c = REC_CHUNK
    rr = lax.broadcasted_iota(I32, (c, c), 0)
    cc = lax.broadcasted_iota(I32, (c, c), 1)
    tri = (rr >= cc).astype(F32)

    for ci in range(chunks):
        sl = slice(ci * c, (ci + 1) * c)
        f = lb + (1.0 - lb) * jax.nn.sigmoid(f_ref[sl, :])
        logf = jnp.log(f)
        k = 1.0 - f
        qf = _silu(q_ref[sl, :]) * (REC_D ** -0.5)
        v = i_ref[sl, :]
        vb = v.astype(BF16)
        b = jnp.dot(tri, logf, preferred_element_type=F32, precision=lax.Precision.HIGHEST)
        b_end = b[c - 1:c, :]
        st = st_ref[...]
        o_inter = _nt_dot((qf * jnp.exp(b)).astype(BF16), st.astype(BF16))

        parts = []
        for j in range(c // REC_SUB):
            lo, hi = j * REC_SUB, (j + 1) * REC_SUB
            ref = jnp.zeros((1, REC_D), F32) if j == 0 else b[lo - 1:lo, :]
            qt = qf[lo:hi] * jnp.exp(b[lo:hi] - ref)
            kk = k[:hi] * jnp.exp(jnp.minimum(ref - b[:hi], EXP_CLAMP))
            s = _nt_dot(qt.astype(BF16), kk.astype(BF16))
            r2 = lax.broadcasted_iota(I32, (REC_SUB, hi), 0) + lo
            c2 = lax.broadcasted_iota(I32, (REC_SUB, hi), 1)
            s = jnp.where(c2 <= r2, s, 0.0)
            parts.append(jnp.dot(s.astype(BF16), vb[:hi], preferred_element_type=F32))
        o = o_inter + jnp.concatenate(parts, axis=0)

        kd = k * jnp.exp(b_end - b)
        st_ref[...] = st * jnp.exp(b_end) + _tn_dot(vb, kd.astype(BF16))

        on = o * lax.rsqrt(jnp.mean(o * o, axis=-1, keepdims=True) + EPS)
        o_ref[sl, :] = (on * gout * _silu(g_ref[sl, :])).astype(o_ref.dtype)


def _hgrn(proj, lb_logits, g_out, bsz, seq, layer):
    n = proj.shape[0]
    tc = min(512, seq)
    per_b = seq // tc
    nl = lb_logits.shape[0]

    def col(off):
        return lambda b, h, c: (b * per_b + c, off // REC_D + h)

    return pl.pallas_call(
        functools.partial(_hgrn_kernel, chunks=tc // REC_CHUNK, layer=layer),
        out_shape=jax.ShapeDtypeStruct((n, REC_HEADS * REC_D), BF16),
        grid=(bsz, REC_HEADS, per_b),
        in_specs=[pl.BlockSpec((tc, REC_D), col(OFF_RQ)),
                  pl.BlockSpec((tc, REC_D), col(OFF_RF)),
                  pl.BlockSpec((tc, REC_D), col(OFF_RI)),
                  pl.BlockSpec((tc, REC_D), col(OFF_RG)),
                  pl.BlockSpec((nl, REC_D), lambda b, h, c: (0, h)),
                  pl.BlockSpec((1, REC_D), lambda b, h, c: (0, h))],
        out_specs=pl.BlockSpec((tc, REC_D), lambda b, h, c: (b * per_b + c, h)),
        scratch_shapes=[pltpu.VMEM((REC_D, REC_D), F32)],
        compiler_params=pltpu.CompilerParams(
            dimension_semantics=("arbitrary", "arbitrary", "arbitrary"),
            vmem_limit_bytes=VMEM_LIMIT),
        name="hgrn2",
    )(proj, proj, proj, proj, lb_logits, g_out.reshape(1, -1))


TQ = 128
TK = 512


def _dsa_kernel(qi_ref, iw_ref, aq_ref, kx_ref, k_ref, v_ref, gout_ref, o_ref,
                key_ref, m_ref, l_ref, acc_ref, *, ksel, seq):
    t0 = pl.program_id(1) * TQ
    nkt = (t0 + TQ + TK - 1) // TK

    row = t0 + lax.broadcasted_iota(I32, (TQ, TK), 0)
    lane = lax.broadcasted_iota(I32, (TQ, TK), 1)

    qi = qi_ref[...]
    qh = jnp.concatenate([qi[:, h * IDX_DIM:(h + 1) * IDX_DIM] for h in range(IDX_HEADS)], axis=0)
    qh = (qh * (IDX_DIM ** -0.5)).astype(BF16)
    iw = iw_ref[...]
    wcol = [iw[:, IDX_DIM + h:IDX_DIM + h + 1] * (IDX_HEADS ** -0.5) for h in range(IDX_HEADS)]

    def score_body(kt, carry):
        k0 = pl.multiple_of(kt * TK, TK)
        rel = _nt_dot(qh, kx_ref[pl.ds(k0, TK), :])
        sc = jnp.zeros((TQ, TK), F32)
        for h in range(IDX_HEADS):
            sc = sc + wcol[h] * jnp.maximum(rel[h * TQ:(h + 1) * TQ], 0.0)
        sc = jnp.where(k0 + lane <= row, sc, -jnp.inf)
        bits = lax.bitcast_convert_type(sc, I32)
        key_ref[:, pl.ds(k0, TK)] = bits ^ ((bits >> 31) & 0x7FFFFFFF)
        return carry

    lax.fori_loop(0, nkt, score_body, 0)

    lane1 = lax.broadcasted_iota(I32, (TQ, LANES), 1)

    def count(pred):
        def body(kt, acc):
            k0 = pl.multiple_of(kt * TK, TK)
            kk = key_ref[:, pl.ds(k0, TK)]
            for u in range(TK // LANES):
                hit = pred(kk[:, u * LANES:(u + 1) * LANES], k0 + u * LANES + lane1)
                acc = acc + jnp.where(hit, 1.0, 0.0)
            return acc
        acc = lax.fori_loop(0, nkt, body, jnp.zeros((TQ, LANES), F32))
        return jnp.sum(acc, axis=1, keepdims=True)

    def count_ge(cand):
        cb = jnp.broadcast_to(cand, (TQ, LANES))
        return count(lambda kk, pos: kk >= cb)

    kf = float(ksel)
    thr = jnp.where(count_ge(jnp.zeros((TQ, 1), I32)) >= kf, 0, INT_MIN).astype(I32)

    def bit_body(i, thr):
        cand = thr | lax.shift_left(jnp.int32(1), 30 - i)
        return jnp.where(count_ge(cand) >= kf, cand, thr)

    thr = lax.fori_loop(0, 31, bit_body, thr)

    tb = jnp.broadcast_to(thr, (TQ, LANES))
    c_gt = count(lambda kk, pos: kk > tb)
    c_eq = count(lambda kk, pos: kk == tb)
    need = kf - c_gt
    real = thr > KEY_NEGINF
    excess = jnp.logical_and(c_eq > need, real)
    any_excess = jnp.max(jnp.where(excess, 1.0, 0.0)) > 0.0

    def find_last():
        def jbody(i, y):
            cand = y | lax.shift_left(jnp.int32(1), (seq.bit_length() - 1) - i)
            cb = jnp.broadcast_to(cand, (TQ, LANES))
            below = count(lambda kk, pos: jnp.logical_and(kk == tb, pos < cb))
            return jnp.where(below <= need - 1.0, cand, y)
        return lax.fori_loop(0, seq.bit_length(), jbody, jnp.zeros((TQ, 1), I32))

    last = lax.cond(any_excess, find_last, lambda: jnp.full((TQ, 1), seq, I32))
    last = jnp.where(real, last, -1)
    thr_m = jnp.maximum(thr, KEY_NEGINF)

    aq = aq_ref[...]
    qg = []
    for g in range(ATT_KV_HEADS):
        blk = [aq[:, (g * ATT_GROUP + j) * ATT_DH:(g * ATT_GROUP + j + 1) * ATT_DH] for j in range(ATT_GROUP)]
        qg.append((jnp.concatenate(blk, axis=0) * (ATT_DH ** -0.5)).astype(BF16))

    m_ref[...] = jnp.full(m_ref.shape, NEG_BIG, F32)
    l_ref[...] = jnp.zeros(l_ref.shape, F32)
    acc_ref[...] = jnp.zeros(acc_ref.shape, F32)

    def att_body(kt, carry):
        k0 = pl.multiple_of(kt * TK, TK)
        kk = key_ref[:, pl.ds(k0, TK)]
        pos = k0 + lane
        sel = jnp.logical_or(kk > thr_m, jnp.logical_and(kk == thr_m, pos <= last))
        bias = jnp.where(sel, 0.0, NEG_BIG)
        bias = jnp.concatenate([bias] * ATT_GROUP, axis=0)
        for g in range(ATT_KV_HEADS):
            kt_ = k_ref[pl.ds(k0, TK), g * ATT_DH:(g + 1) * ATT_DH]
            vt_ = v_ref[pl.ds(k0, TK), g * ATT_DH:(g + 1) * ATT_DH]
            s = _nt_dot(qg[g], kt_) + bias
            m_old = m_ref[g]
            m_new = jnp.maximum(m_old, jnp.max(s, axis=1, keepdims=True))
            alpha = jnp.exp(m_old - m_new)
            p = jnp.exp(s - m_new)
            l_ref[g] = alpha * l_ref[g] + jnp.sum(p, axis=1, keepdims=True)
            acc_ref[g] = alpha * acc_ref[g] + jnp.dot(p.astype(BF16), vt_, preferred_element_type=F32)
            m_ref[g] = m_new
        return carry

    lax.fori_loop(0, nkt, att_body, 0)

    gout = gout_ref[...]
    for g in range(ATT_KV_HEADS):
        o = acc_ref[g] / l_ref[g]
        on = o * lax.rsqrt(jnp.mean(o * o, axis=-1, keepdims=True) + EPS)
        for j in range(ATT_GROUP):
            hsl = slice((g * ATT_GROUP + j) * ATT_DH, (g * ATT_GROUP + j + 1) * ATT_DH)
            o_ref[:, hsl] = (on[j * TQ:(j + 1) * TQ] * gout[:, hsl]).astype(o_ref.dtype)


def _dsa(proj, kb, vb, kxb, g_out, bsz, seq):
    n = proj.shape[0]
    nqb = seq // TQ
    ksel = min(TOPK_MAX, seq // 4)
    kb3 = kb.reshape(bsz, seq, 256)
    vb3 = vb.reshape(bsz, seq, 256)
    kx3 = kxb.reshape(bsz, seq, IDX_DIM)
    gq = ATT_GROUP * TQ
    return pl.pallas_call(
        functools.partial(_dsa_kernel, ksel=ksel, seq=seq),
        out_shape=jax.ShapeDtypeStruct((n, ATT_HEADS * ATT_DH), BF16),
        grid=(bsz, nqb),
        in_specs=[pl.BlockSpec((TQ, 512), lambda b, q: (b * nqb + q, OFF_IQ // 512)),
                  pl.BlockSpec((TQ, LANES), lambda b, q: (b * nqb + q, OFF_IK // LANES)),
                  pl.BlockSpec((TQ, 1024), lambda b, q: (b * nqb + q, OFF_AQ // 1024)),
                  pl.BlockSpec((None, seq, IDX_DIM), lambda b, q: (b, 0, 0)),
                  pl.BlockSpec((None, seq, 256), lambda b, q: (b, 0, 0)),
                  pl.BlockSpec((None, seq, 256), lambda b, q: (b, 0, 0)),
                  pl.BlockSpec((1, ATT_HEADS * ATT_DH), lambda b, q: (0, 0))],
        out_specs=pl.BlockSpec((TQ, ATT_HEADS * ATT_DH), lambda b, q: (b * nqb + q, 0)),
        scratch_shapes=[pltpu.VMEM((TQ, seq), I32),
                        pltpu.VMEM((ATT_KV_HEADS, gq, 1), F32),
                        pltpu.VMEM((ATT_KV_HEADS, gq, 1), F32),
                        pltpu.VMEM((ATT_KV_HEADS, gq, ATT_DH), F32)],
        compiler_params=pltpu.CompilerParams(dimension_semantics=("arbitrary", "arbitrary"),
                                             vmem_limit_bytes=VMEM_LIMIT),
        name="dsa",
    )(proj, proj, proj, kx3, kb3, vb3, g_out.reshape(1, -1))


def _outproj_kernel(rec_ref, att_ref, x_ref, gt_ref, sc_ref, sh_ref, g_ref, wo_ref, wr_ref, br_ref,
                    x1_ref, h2_ref, ids_ref, wts_ref):
    wo = wo_ref[...]
    half = rec_ref.shape[1]
    mixed = (jnp.dot(rec_ref[...], wo[:half], preferred_element_type=F32)
             + jnp.dot(att_ref[...], wo[half:], preferred_element_type=F32))
    x1 = x_ref[...] + gt_ref[0] * mixed
    x1_ref[...] = x1
    h2 = _norm_mod(x1, g_ref[...], sc_ref[0], sh_ref[0])
    h2_ref[...] = h2.astype(BF16)

    logits = jnp.dot(h2, wr_ref[...], preferred_element_type=F32,
                     precision=lax.Precision.HIGHEST) + br_ref[...]
    tm = logits.shape[0]
    lane = lax.broadcasted_iota(I32, (tm, LANES), 1)
    big = jnp.int32(LANES)

    def argmax_first(vals, mask):
        mv = jnp.where(mask, vals, -jnp.inf)
        top = jnp.max(mv, axis=1, keepdims=True)
        idx = jnp.min(jnp.where(jnp.logical_and(mask, mv == top), lane, big), axis=1, keepdims=True)
        return top, idx

    gmask = lane < N_GROUPS
    gtop, gidx = argmax_first(logits, gmask)
    p_g = 1.0 / jnp.sum(jnp.where(gmask, jnp.exp(logits - gtop), 0.0), axis=1, keepdims=True)
    e_lo = N_GROUPS + gidx * EXPERTS_PER_GROUP
    emask = jnp.logical_and(lane >= e_lo, lane < e_lo + EXPERTS_PER_GROUP)
    v1, i1 = argmax_first(logits, emask)
    v2, i2 = argmax_first(logits, jnp.logical_and(emask, lane != i1))
    r = jnp.exp(v2 - v1)
    w1 = p_g / (1.0 + r)
    w2 = p_g * r / (1.0 + r)
    ids_ref[...] = jnp.where(lane == 0, i1 - N_GROUPS, jnp.where(lane == 1, i2 - N_GROUPS, 0))
    wts_ref[...] = jnp.where(lane == 0, w1, jnp.where(lane == 1, w2, 0.0))


def _outproj(rec, att, x2, gt, sc, sh, g, wo_bf, wr, br, seq):
    n, d = x2.shape
    tm = min(512, seq)
    per_b = seq // tm
    half = rec.shape[1]
    bspec = pl.BlockSpec((1, 1, d), lambda i: (i // per_b, 0, 0))
    return pl.pallas_call(
        _outproj_kernel,
        out_shape=(jax.ShapeDtypeStruct((n, d), F32),
                   jax.ShapeDtypeStruct((n, d), BF16),
                   jax.ShapeDtypeStruct((n, LANES), I32),
                   jax.ShapeDtypeStruct((n, LANES), F32)),
        grid=(n // tm,),
        in_specs=[pl.BlockSpec((tm, half), lambda i: (i, 0)),
                  pl.BlockSpec((tm, half), lambda i: (i, 0)),
                  pl.BlockSpec((tm, d), lambda i: (i, 0)),
                  bspec, bspec, bspec,
                  pl.BlockSpec((1, d), lambda i: (0, 0)),
                  pl.BlockSpec((2 * half, d), lambda i: (0, 0)),
                  pl.BlockSpec((d, LANES), lambda i: (0, 0)),
                  pl.BlockSpec((1, LANES), lambda i: (0, 0))],
        out_specs=(pl.BlockSpec((tm, d), lambda i: (i, 0)),
                   pl.BlockSpec((tm, d), lambda i: (i, 0)),
                   pl.BlockSpec((tm, LANES), lambda i: (i, 0)),
                   pl.BlockSpec((tm, LANES), lambda i: (i, 0))),
        compiler_params=pltpu.CompilerParams(dimension_semantics=("arbitrary",),
                                             vmem_limit_bytes=VMEM_LIMIT),
        name="outproj",
    )(rec, att, x2, gt, sc, sh, g, wo_bf, wr, br)


MOE_TM = 256


def _moe_kernel(te_ref, nt_ref, hs_ref, rw_ref, wg_ref, wu_ref, wd_ref, o_ref):
    t = pl.program_id(0)

    @pl.when(t < nt_ref[0])
    def _():
        xs = hs_ref[...]
        gte = jnp.dot(xs, wg_ref[...].astype(BF16), preferred_element_type=F32)
        up = jnp.dot(xs, wu_ref[...].astype(BF16), preferred_element_type=F32)
        act = (_silu(gte) * up).astype(BF16)
        y = jnp.dot(act, wd_ref[...].astype(BF16), preferred_element_type=F32)
        o_ref[...] = y * rw_ref[...]

    @pl.when(t >= nt_ref[0])
    def _():
        o_ref[...] = jnp.zeros_like(o_ref)


def _moe(tile_expert, n_tiles, hs, rw, wg, wu, wd):
    p, d = hs.shape
    de = wg.shape[2]
    tm = MOE_TM
    grid_spec = pltpu.PrefetchScalarGridSpec(
        num_scalar_prefetch=2,
        grid=(p // tm,),
        in_specs=[pl.BlockSpec((tm, d), lambda t, te, nt: (t, 0)),
                  pl.BlockSpec((tm, 1), lambda t, te, nt: (t, 0)),
                  pl.BlockSpec((None, d, de), lambda t, te, nt: (te[t], 0, 0)),
                  pl.BlockSpec((None, d, de), lambda t, te, nt: (te[t], 0, 0)),
                  pl.BlockSpec((None, de, d), lambda t, te, nt: (te[t], 0, 0))],
        out_specs=pl.BlockSpec((tm, d), lambda t, te, nt: (t, 0)),
    )
    return pl.pallas_call(
        _moe_kernel,
        out_shape=jax.ShapeDtypeStruct((p, d), F32),
        grid_spec=grid_spec,
        compiler_params=pltpu.CompilerParams(dimension_semantics=("arbitrary",),
                                             vmem_limit_bytes=56 * 1024 * 1024),
        name="moe",
    )(tile_expert, n_tiles, hs, rw, wg, wu, wd)


def _final_kernel(x1_ref, ya_ref, yb_ref, gt_ref, g_ref, o_ref):
    xo = x1_ref[...] + gt_ref[0] * (ya_ref[...] + yb_ref[...])
    o_ref[...] = xo * lax.rsqrt(jnp.mean(xo * xo, axis=-1, keepdims=True) + EPS) * g_ref[...]


def _final(x1, ya, yb, gt, g, seq):
    n, d = x1.shape
    tm = min(512, seq)
    per_b = seq // tm
    row = pl.BlockSpec((tm, d), lambda i: (i, 0))
    return pl.pallas_call(
        _final_kernel,
        out_shape=jax.ShapeDtypeStruct((n, d), F32),
        grid=(n // tm,),
        in_specs=[row, row, row,
                  pl.BlockSpec((1, 1, d), lambda i: (i // per_b, 0, 0)),
                  pl.BlockSpec((1, d), lambda i: (0, 0))],
        out_specs=row,
        compiler_params=pltpu.CompilerParams(dimension_semantics=("arbitrary",),
                                             vmem_limit_bytes=VMEM_LIMIT),
        name="final",
    )(x1, ya, yb, gt, g)


def _route_tables(ids, wts, n_tok):
    tm = MOE_TM
    na = 2 * n_tok
    p_rows = na + N_EXPERTS * tm
    eid = ids.reshape(na)
    w = wts.reshape(na)
    order = jnp.argsort(eid, stable=True).astype(I32)
    counts = jnp.zeros((N_EXPERTS,), I32).at[eid].add(1)
    padded = ((counts + tm - 1) // tm) * tm
    pend = jnp.cumsum(padded)
    poff = pend - padded
    off = jnp.cumsum(counts) - counts
    es = eid[order]
    dest = poff[es] + (jnp.arange(na, dtype=I32) - off[es])
    row_token = jnp.zeros((p_rows,), I32).at[dest].set(order // 2)
    row_w = jnp.zeros((p_rows,), F32).at[dest].set(w[order])
    row_of_assign = jnp.zeros((na,), I32).at[order].set(dest)
    tile_start = jnp.arange(p_rows // tm, dtype=I32) * tm
    tile_expert = jnp.minimum(jnp.searchsorted(pend, tile_start, side="right"), N_EXPERTS - 1).astype(I32)
    n_tiles = (pend[-1] // tm).astype(I32).reshape(1)
    return row_token, row_w, row_of_assign, tile_expert, n_tiles


def kernel(x, c, w_ada, b_ada, g_norm_mix, w_in, lb_logits, g_rec_out, g_att_out, w_out, g_norm_ffn,
           w_router_group, b_router_group, w_router_expert, b_router_expert,
           w_expert_gate, w_expert_up, w_expert_down, g_final):
    bsz, seq, d = x.shape
    n = bsz * seq
    assert w_ada.shape[0] == 1, "single trunk layer"
    layer = 0
    x2 = x.reshape(n, d)

    mod = _adaln(c, w_ada[layer], b_ada[layer])
    sh1, sc1, gt1, sh2, sc2, gt2 = [m.reshape(bsz, 1, d) for m in jnp.split(mod, 6, axis=-1)]

    w_in_bf = jnp.pad(w_in[layer], ((0, 0), (0, IN_PAD - IN_COLS))).astype(BF16)
    proj = _inproj(x2, sc1, sh1, g_norm_mix[layer].reshape(1, d), w_in_bf, seq)
    kb, vb, kxb = _kvprep(proj)
    rec = _hgrn(proj, lb_logits, g_rec_out[layer], bsz, seq, layer)
    att = _dsa(proj, kb, vb, kxb, g_att_out[layer], bsz, seq)

    wr = jnp.concatenate([w_router_group[layer], w_router_expert[layer]], axis=1)
    wr = jnp.pad(wr, ((0, 0), (0, LANES - wr.shape[1])))
    br = jnp.concatenate([b_router_group[layer], b_router_expert[layer]])
    br = jnp.pad(br, (0, LANES - br.shape[0])).reshape(1, LANES)
    x1, h2, ids, wts = _outproj(rec, att, x2, gt1, sc2, sh2, g_norm_ffn[layer].reshape(1, d),
                                w_out[layer].astype(BF16), wr, br, seq)

    row_token, row_w, row_of_assign, tile_expert, n_tiles = _route_tables(ids[:, :2], wts[:, :2], n)
    hs = jnp.take(h2, row_token, axis=0)
    ys = _moe(tile_expert, n_tiles, hs, row_w.reshape(-1, 1),
              w_expert_gate[layer], w_expert_up[layer], w_expert_down[layer])
    ra = row_of_assign.reshape(n, 2)
    ya = jnp.take(ys, ra[:, 0], axis=0)
    yb = jnp.take(ys, ra[:, 1], axis=0)
    out = _final(x1, ya, yb, gt2, g_final.reshape(1, d), seq)
    return out.reshape(bsz, seq, d)
```

```python
import functools

import jax
import jax.numpy as jnp
import numpy as np
from jax import lax
from jax.experimental import pallas as pl
from jax.experimental.pallas import tpu as pltpu

F32 = jnp.float32
BF16 = jnp.bfloat16
I32 = jnp.int32

EPS = 1e-6
LANES = 128

REC_HEADS = 8
REC_D = 128
REC_CHUNK = 64
REC_SUB = 16
ATT_HEADS = 8
ATT_DH = 128
ATT_KV_HEADS = 2
ATT_GROUP = ATT_HEADS // ATT_KV_HEADS
IDX_HEADS = 8
IDX_DIM = 64
TOPK_MAX = 256
N_GROUPS = 4
EXPERTS_PER_GROUP = 8
N_EXPERTS = N_GROUPS * EXPERTS_PER_GROUP
D_EXPERT = 512

OFF_RQ = 0
OFF_RF = 1024
OFF_RI = 2048
OFF_RG = 3072
OFF_AQ = 4096
OFF_AK = 5120
OFF_AV = 5376
OFF_IQ = 5632
OFF_IK = 6144
OFF_IW = 6208
IN_COLS = 6216
IN_PAD = 6272

VMEM_LIMIT = 48 * 1024 * 1024

INT_MIN = -(2 ** 31)
KEY_NEGINF = int(np.array(-np.inf, np.float32).view(np.int32)) ^ 0x7FFFFFFF
NEG_BIG = -1e30
EXP_CLAMP = 80.0


def _silu(v):
    return v * jax.nn.sigmoid(v)


def _nt_dot(a, b):
    return lax.dot_general(a, b, (((1,), (1,)), ((), ())), preferred_element_type=F32)


def _tn_dot(a, b):
    return lax.dot_general(a, b, (((0,), (0,)), ((), ())), preferred_element_type=F32)


def _adaln_kernel(c_ref, w_ref, b_ref, o_ref):
    ca = _silu(c_ref[...])
    o_ref[...] = jnp.dot(ca, w_ref[...], preferred_element_type=F32,
                         precision=lax.Precision.HIGHEST) + b_ref[...]


def _adaln(c, w, b):
    bsz, d = c.shape
    n = w.shape[1]
    tn = 512
    return pl.pallas_call(
        _adaln_kernel,
        out_shape=jax.ShapeDtypeStruct((bsz, n), F32),
        grid=(n // tn,),
        in_specs=[pl.BlockSpec((bsz, d), lambda j: (0, 0)),
                  pl.BlockSpec((d, tn), lambda j: (0, j)),
                  pl.BlockSpec((1, tn), lambda j: (0, j))],
        out_specs=pl.BlockSpec((bsz, tn), lambda j: (0, j)),
        compiler_params=pltpu.CompilerParams(dimension_semantics=("arbitrary",),
                                             vmem_limit_bytes=VMEM_LIMIT),
        name="adaln",
    )(c, w, b.reshape(1, n))


def _norm_mod(x, g, sc, sh):
    xn = x * lax.rsqrt(jnp.mean(x * x, axis=-1, keepdims=True) + EPS)
    return xn * g * (1.0 + sc) + sh


def _inproj_kernel(x_ref, sc_ref, sh_ref, g_ref, w_ref, o_ref, h_ref):
    @pl.when(pl.program_id(1) == 0)
    def _():
        h_ref[...] = _norm_mod(x_ref[...], g_ref[...], sc_ref[0], sh_ref[0]).astype(BF16)

    o_ref[...] = jnp.dot(h_ref[...], w_ref[...], preferred_element_type=F32)


def _inproj(x2, sc, sh, g, w_bf, seq):
    n, d = x2.shape
    ncol = w_bf.shape[1]
    tm = min(512, seq)
    tn = 896
    per_b = seq // tm
    return pl.pallas_call(
        _inproj_kernel,
        out_shape=jax.ShapeDtypeStruct((n, ncol), F32),
        grid=(n // tm, ncol // tn),
        in_specs=[pl.BlockSpec((tm, d), lambda i, j: (i, 0)),
                  pl.BlockSpec((1, 1, d), lambda i, j: (i // per_b, 0, 0)),
                  pl.BlockSpec((1, 1, d), lambda i, j: (i // per_b, 0, 0)),
                  pl.BlockSpec((1, d), lambda i, j: (0, 0)),
                  pl.BlockSpec((d, tn), lambda i, j: (0, j))],
        out_specs=pl.BlockSpec((tm, tn), lambda i, j: (i, j)),
        scratch_shapes=[pltpu.VMEM((tm, d), BF16)],
        compiler_params=pltpu.CompilerParams(dimension_semantics=("arbitrary", "arbitrary"),
                                             vmem_limit_bytes=VMEM_LIMIT),
        name="inproj",
    )(x2, sc, sh, g, w_bf)


VT_ROWS = ATT_DH + 16


def _kvprep_kernel(kv_ref, ik_ref, k_ref, vt_ref, kx_ref):
    kv = kv_ref[...]
    tm = kv.shape[0]
    k_ref[...] = kv[:, :256].astype(BF16)
    tail = jnp.where(lax.broadcasted_iota(I32, (16, tm), 0) == 0, 1.0, 0.0)
    for g in range(ATT_KV_HEADS):
        vt = kv[:, 256 + g * ATT_DH:256 + (g + 1) * ATT_DH].T
        vt_ref[g] = jnp.concatenate([vt, tail], axis=0).astype(BF16)
    kx_ref[...] = ik_ref[...][:, :IDX_DIM].astype(BF16)


def _kvprep(proj):
    n = proj.shape[0]
    tm = 512
    return pl.pallas_call(
        _kvprep_kernel,
        out_shape=(jax.ShapeDtypeStruct((n, 256), BF16),
                   jax.ShapeDtypeStruct((ATT_KV_HEADS, VT_ROWS, n), BF16),
                   jax.ShapeDtypeStruct((n, IDX_DIM), BF16)),
        grid=(n // tm,),
        in_specs=[pl.BlockSpec((tm, 512), lambda i: (i, OFF_AK // 512)),
                  pl.BlockSpec((tm, LANES), lambda i: (i, OFF_IK // LANES))],
        out_specs=(pl.BlockSpec((tm, 256), lambda i: (i, 0)),
                   pl.BlockSpec((ATT_KV_HEADS, VT_ROWS, tm), lambda i: (0, 0, i)),
                   pl.BlockSpec((tm, IDX_DIM), lambda i: (i, 0))),
        compiler_params=pltpu.CompilerParams(dimension_semantics=("arbitrary",),
                                             vmem_limit_bytes=VMEM_LIMIT),
        name="kvprep",
    )(proj, proj)


def _hgrn_kernel(q_ref, f_ref, i_ref, g_ref, lbl_ref, gout_ref, o_ref, st_ref, *, chunks, layer):
    @pl.when(pl.program_id(2) == 0)
    def _():
        st_ref[...] = jnp.zeros_like(st_ref)

    lbl = lbl_ref[...]
    e = jnp.exp(lbl - jnp.max(lbl, axis=0, keepdims=True))
    sm = e / jnp.sum(e, axis=0, keepdims=True)
    lb = jnp.sum(sm[: layer + 1], axis=0, keepdims=True)
    gout = gout_ref[...]

    c = REC_CHUNK
    rr = lax.broadcasted_iota(I32, (c, c), 0)
    cc = lax.broadcasted_iota(I32, (c, c), 1)
    tri = (rr >= cc).astype(F32)

    for ci in range(chunks):
        sl = slice(ci * c, (ci + 1) * c)
        f = lb + (1.0 - lb) * jax.nn.sigmoid(f_ref[sl, :])
        logf = jnp.log(f)
        k = 1.0 - f
        qf = _silu(q_ref[sl, :]) * (REC_D ** -0.5)
        v = i_ref[sl, :]
        vb = v.astype(BF16)
        b = jnp.dot(tri, logf, preferred_element_type=F32, precision=lax.Precision.HIGHEST)
        b_end = b[c - 1:c, :]
        st = st_ref[...]
        o_inter = _nt_dot((qf * jnp.exp(b)).astype(BF16), st.astype(BF16))

        parts = []
        for j in range(c // REC_SUB):
            lo, hi = j * REC_SUB, (j + 1) * REC_SUB
            ref = jnp.zeros((1, REC_D), F32) if j == 0 else b[lo - 1:lo, :]
            qt = qf[lo:hi] * jnp.exp(b[lo:hi] - ref)
            kk = k[:hi] * jnp.exp(jnp.minimum(ref - b[:hi], EXP_CLAMP))
            s = _nt_dot(qt.astype(BF16), kk.astype(BF16))
            r2 = lax.broadcasted_iota(I32, (REC_SUB, hi), 0) + lo
            c2 = lax.broadcasted_iota(I32, (REC_SUB, hi), 1)
            s = jnp.where(c2 <= r2, s, 0.0)
            parts.append(jnp.dot(s.astype(BF16), vb[:hi], preferred_element_type=F32))
        o = o_inter + jnp.concatenate(parts, axis=0)

        kd = k * jnp.exp(b_end - b)
        st_ref[...] = st * jnp.exp(b_end) + _tn_dot(vb, kd.astype(BF16))

        on = o * lax.rsqrt(jnp.mean(o * o, axis=-1, keepdims=True) + EPS)
        o_ref[sl, :] = (on * gout * _silu(g_ref[sl, :])).astype(o_ref.dtype)


def _hgrn(proj, lb_logits, g_out, bsz, seq, layer):
    n = proj.shape[0]
    tc = min(512, seq)
    per_b = seq // tc
    nl = lb_logits.shape[0]

    def col(off):
        return lambda b, h, c: (b * per_b + c, off // REC_D + h)

    return pl.pallas_call(
        functools.partial(_hgrn_kernel, chunks=tc // REC_CHUNK, layer=layer),
        out_shape=jax.ShapeDtypeStruct((n, REC_HEADS * REC_D), BF16),
        grid=(bsz, REC_HEADS, per_b),
        in_specs=[pl.BlockSpec((tc, REC_D), col(OFF_RQ)),
                  pl.BlockSpec((tc, REC_D), col(OFF_RF)),
                  pl.BlockSpec((tc, REC_D), col(OFF_RI)),
                  pl.BlockSpec((tc, REC_D), col(OFF_RG)),
                  pl.BlockSpec((nl, REC_D), lambda b, h, c: (0, h)),
                  pl.BlockSpec((1, REC_D), lambda b, h, c: (0, h))],
        out_specs=pl.BlockSpec((tc, REC_D), lambda b, h, c: (b * per_b + c, h)),
        scratch_shapes=[pltpu.VMEM((REC_D, REC_D), F32)],
        compiler_params=pltpu.CompilerParams(
            dimension_semantics=("arbitrary", "arbitrary", "arbitrary"),
            vmem_limit_bytes=VMEM_LIMIT),
        name="hgrn2",
    )(proj, proj, proj, proj, lb_logits, g_out.reshape(1, -1))


TQ = 128
TK = 1024
LOG2E = 1.4426950408889634
CNT_ROWS = 64
ATT_QW = 512


def _dsa_kernel(qi_ref, iw_ref, aq_ref, kx_ref, k_ref, vt_ref, gout_ref, o_ref,
                key_ref, m_ref, acc_ref, *, ksel, seq):
    t0 = pl.program_id(1) * TQ
    nkt = (t0 + TQ + TK - 1) // TK
    qpos = t0 + lax.broadcasted_iota(I32, (TK, TQ), 1)
    krow = lax.broadcasted_iota(I32, (TK, TQ), 0)

    qit = (qi_ref[...] * (IDX_DIM ** -0.5)).T
    qht = jnp.concatenate([qit[h * IDX_DIM:(h + 1) * IDX_DIM] for h in range(IDX_HEADS)],
                          axis=1).astype(BF16)
    iwt = iw_ref[...].T
    wrow = [iwt[IDX_DIM + h:IDX_DIM + h + 1] * (IDX_HEADS ** -0.5) for h in range(IDX_HEADS)]

    def score_body(kt, carry):
        k0 = pl.multiple_of(kt * TK, TK)
        rel = jnp.dot(kx_ref[pl.ds(k0, TK), :], qht, preferred_element_type=F32)
        sc = jnp.zeros((TK, TQ), F32)
        for h in range(IDX_HEADS):
            sc = sc + wrow[h] * jnp.maximum(rel[:, h * TQ:(h + 1) * TQ], 0.0)
        sc = jnp.where(k0 + krow <= qpos, sc, -jnp.inf)
        bits = lax.bitcast_convert_type(sc, I32)
        key_ref[pl.ds(k0, TK), :] = bits ^ ((bits >> 31) & 0x7FFFFFFF)
        return carry

    lax.fori_loop(0, nkt, score_body, 0)

    def count(pred):
        def body(kt, acc):
            k0 = pl.multiple_of(kt * TK, TK)
            hit = pred(key_ref[pl.ds(k0, TK), :], k0 + krow)
            return acc + jnp.sum(jnp.where(hit, 1.0, 0.0).reshape(TK // CNT_ROWS, CNT_ROWS, TQ), axis=0)
        acc = lax.fori_loop(0, nkt, body, jnp.zeros((CNT_ROWS, TQ), F32))
        return jnp.sum(acc, axis=0, keepdims=True)

    kf = float(ksel)
    thr = jnp.where(count(lambda kk, pos: kk >= 0) >= kf, 0, INT_MIN).astype(I32)

    def bit_body(i, thr):
        cand = thr | lax.shift_left(jnp.int32(1), 30 - i)
        return jnp.where(count(lambda kk, pos: kk >= cand) >= kf, cand, thr)

    thr = lax.fori_loop(0, 31, bit_body, thr)

    c_gt = count(lambda kk, pos: kk > thr)
    c_eq = count(lambda kk, pos: kk == thr)
    need = kf - c_gt
    real = thr > KEY_NEGINF
    excess = jnp.logical_and(c_eq > need, real)
    any_excess = jnp.max(jnp.where(excess, 1.0, 0.0)) > 0.0
    nbits = seq.bit_length()

    def find_last():
        def jbody(i, y):
            cand = y | lax.shift_left(jnp.int32(1), (nbits - 1) - i)
            below = count(lambda kk, pos: jnp.logical_and(kk == thr, pos < cand))
            return jnp.where(below <= need - 1.0, cand, y)
        return lax.fori_loop(0, nbits, jbody, jnp.zeros((1, TQ), I32))

    last = lax.cond(any_excess, find_last, lambda: jnp.full((1, TQ), seq, I32))
    last = jnp.where(real, last, -1)
    thr_m = jnp.maximum(thr, KEY_NEGINF)

    aq = aq_ref[...] * ((ATT_DH ** -0.5) * LOG2E)
    qgt = []
    for g in range(ATT_KV_HEADS):
        blk = [aq[:, (g * ATT_GROUP + j) * ATT_DH:(g * ATT_GROUP + j + 1) * ATT_DH].T for j in range(ATT_GROUP)]
        qgt.append(jnp.concatenate(blk, axis=1).astype(BF16))

    m_ref[...] = jnp.full(m_ref.shape, NEG_BIG, F32)
    acc_ref[...] = jnp.zeros(acc_ref.shape, F32)

    def att_body(kt, carry):
        k0 = pl.multiple_of(kt * TK, TK)
        kk = key_ref[pl.ds(k0, TK), :]
        sel = jnp.logical_or(kk > thr_m, jnp.logical_and(kk == thr_m, k0 + krow <= last))
        bias = jnp.where(sel, 0.0, NEG_BIG)
        bias = jnp.concatenate([bias] * (ATT_QW // TQ), axis=1)
        for g in range(ATT_KV_HEADS):
            kt_ = k_ref[pl.ds(k0, TK), g * ATT_DH:(g + 1) * ATT_DH]
            vt_ = vt_ref[g, :, pl.ds(k0, TK)]
            for c in range(0, ATT_GROUP * TQ, ATT_QW):
                cs = slice(c, c + ATT_QW)
                s = jnp.dot(kt_, qgt[g][:, cs], preferred_element_type=F32) + bias
                m_old = m_ref[g, :, cs]
                m_new = jnp.maximum(m_old, jnp.max(s, axis=0, keepdims=True))
                alpha = jnp.exp2(m_old - m_new)
                p = jnp.exp2(s - m_new).astype(BF16)
                pv = jnp.dot(vt_, p, preferred_element_type=F32)
                acc_ref[g, :, cs] = alpha * acc_ref[g, :, cs] + pv
                m_ref[g, :, cs] = m_new
        return carry

    lax.fori_loop(0, nkt, att_body, 0)

    gout = gout_ref[...]
    for g in range(ATT_KV_HEADS):
        a = acc_ref[g]
        o = a[:ATT_DH] / a[ATT_DH:ATT_DH + 1]
        on = o * lax.rsqrt(jnp.mean(o * o, axis=0, keepdims=True) + EPS)
        for j in range(ATT_GROUP):
            hsl = slice((g * ATT_GROUP + j) * ATT_DH, (g * ATT_GROUP + j + 1) * ATT_DH)
            o_ref[:, hsl] = (on[:, j * TQ:(j + 1) * TQ].T * gout[:, hsl]).astype(o_ref.dtype)


def _dsa(proj, kb, vt, kxb, g_out, bsz, seq):
    n = proj.shape[0]
    nqb = seq // TQ
    ksel = min(TOPK_MAX, seq // 4)
    kb3 = kb.reshape(bsz, seq, 256)
    kx3 = kxb.reshape(bsz, seq, IDX_DIM)
    gq = ATT_GROUP * TQ
    return pl.pallas_call(
        functools.partial(_dsa_kernel, ksel=ksel, seq=seq),
        out_shape=jax.ShapeDtypeStruct((n, ATT_HEADS * ATT_DH), BF16),
        grid=(bsz, nqb),
        in_specs=[pl.BlockSpec((TQ, 512), lambda b, q: (b * nqb + q, OFF_IQ // 512)),
                  pl.BlockSpec((TQ, LANES), lambda b, q: (b * nqb + q, OFF_IK // LANES)),
                  pl.BlockSpec((TQ, 1024), lambda b, q: (b * nqb + q, OFF_AQ // 1024)),
                  pl.BlockSpec((None, seq, IDX_DIM), lambda b, q: (b, 0, 0)),
                  pl.BlockSpec((None, seq, 256), lambda b, q: (b, 0, 0)),
                  pl.BlockSpec((ATT_KV_HEADS, VT_ROWS, seq), lambda b, q: (0, 0, b)),
                  pl.BlockSpec((1, ATT_HEADS * ATT_DH), lambda b, q: (0, 0))],
        out_specs=pl.BlockSpec((TQ, ATT_HEADS * ATT_DH), lambda b, q: (b * nqb + q, 0)),
        scratch_shapes=[pltpu.VMEM((seq, TQ), I32),
                        pltpu.VMEM((ATT_KV_HEADS, 1, gq), F32),
                        pltpu.VMEM((ATT_KV_HEADS, VT_ROWS, gq), F32)],
        compiler_params=pltpu.CompilerParams(dimension_semantics=("arbitrary", "arbitrary"),
                                             vmem_limit_bytes=VMEM_LIMIT),
        name="dsa",
    )(proj, proj, proj, kx3, kb3, vt, g_out.reshape(1, -1))


def _outproj_kernel(rec_ref, att_ref, x_ref, gt_ref, sc_ref, sh_ref, g_ref, wo_ref, wr_ref, br_ref,
                    x1_ref, h2_ref, ids_ref, wts_ref):
    wo = wo_ref[...]
    half = rec_ref.shape[1]
    mixed = (jnp.dot(rec_ref[...], wo[:half], preferred_element_type=F32)
             + jnp.dot(att_ref[...], wo[half:], preferred_element_type=F32))
    x1 = x_ref[...] + gt_ref[0] * mixed
    x1_ref[...] = x1
    h2 = _norm_mod(x1, g_ref[...], sc_ref[0], sh_ref[0])
    h2_ref[...] = h2.astype(BF16)

    logits = jnp.dot(h2, wr_ref[...], preferred_element_type=F32,
                     precision=lax.Precision.HIGHEST) + br_ref[...]
    tm = logits.shape[0]
    lane = lax.broadcasted_iota(I32, (tm, LANES), 1)
    big = jnp.int32(LANES)

    def argmax_first(vals, mask):
        mv = jnp.where(mask, vals, -jnp.inf)
        top = jnp.max(mv, axis=1, keepdims=True)
        idx = jnp.min(jnp.where(jnp.logical_and(mask, mv == top), lane, big), axis=1, keepdims=True)
        return top, idx

    gmask = lane < N_GROUPS
    gtop, gidx = argmax_first(logits, gmask)
    p_g = 1.0 / jnp.sum(jnp.where(gmask, jnp.exp(logits - gtop), 0.0), axis=1, keepdims=True)
    e_lo = N_GROUPS + gidx * EXPERTS_PER_GROUP
    emask = jnp.logical_and(lane >= e_lo, lane < e_lo + EXPERTS_PER_GROUP)
    v1, i1 = argmax_first(logits, emask)
    v2, i2 = argmax_first(logits, jnp.logical_and(emask, lane != i1))
    r = jnp.exp(v2 - v1)
    w1 = p_g / (1.0 + r)
    w2 = p_g * r / (1.0 + r)
    ids_ref[...] = jnp.where(lane == 0, i1 - N_GROUPS, jnp.where(lane == 1, i2 - N_GROUPS, 0))
    wts_ref[...] = jnp.where(lane == 0, w1, jnp.where(lane == 1, w2, 0.0))


def _outproj(rec, att, x2, gt, sc, sh, g, wo_bf, wr, br, seq):
    n, d = x2.shape
    tm = min(512, seq)
    per_b = seq // tm
    half = rec.shape[1]
    bspec = pl.BlockSpec((1, 1, d), lambda i: (i // per_b, 0, 0))
    return pl.pallas_call(
        _outproj_kernel,
        out_shape=(jax.ShapeDtypeStruct((n, d), F32),
                   jax.ShapeDtypeStruct((n, d), BF16),
                   jax.ShapeDtypeStruct((n, LANES), I32),
                   jax.ShapeDtypeStruct((n, LANES), F32)),
        grid=(n // tm,),
        in_specs=[pl.BlockSpec((tm, half), lambda i: (i, 0)),
                  pl.BlockSpec((tm, half), lambda i: (i, 0)),
                  pl.BlockSpec((tm, d), lambda i: (i, 0)),
                  bspec, bspec, bspec,
                  pl.BlockSpec((1, d), lambda i: (0, 0)),
                  pl.BlockSpec((2 * half, d), lambda i: (0, 0)),
                  pl.BlockSpec((d, LANES), lambda i: (0, 0)),
                  pl.BlockSpec((1, LANES), lambda i: (0, 0))],
        out_specs=(pl.BlockSpec((tm, d), lambda i: (i, 0)),
                   pl.BlockSpec((tm, d), lambda i: (i, 0)),
                   pl.BlockSpec((tm, LANES), lambda i: (i, 0)),
                   pl.BlockSpec((tm, LANES), lambda i: (i, 0))),
        compiler_params=pltpu.CompilerParams(dimension_semantics=("arbitrary",),
                                             vmem_limit_bytes=VMEM_LIMIT),
        name="outproj",
    )(rec, att, x2, gt, sc, sh, g, wo_bf, wr, br)


MOE_TM = 256


def _moe_kernel(te_ref, nt_ref, hs_ref, rw_ref, wg_ref, wu_ref, wd_ref, o_ref):
    t = pl.program_id(0)

    @pl.when(t < nt_ref[0])
    def _():
        xs = hs_ref[...]
        gte = jnp.dot(xs, wg_ref[...].astype(BF16), preferred_element_type=F32)
        up = jnp.dot(xs, wu_ref[...].astype(BF16), preferred_element_type=F32)
        act = (_silu(gte) * up).astype(BF16)
        y = jnp.dot(act, wd_ref[...].astype(BF16), preferred_element_type=F32)
        o_ref[...] = y * rw_ref[...]

    @pl.when(t >= nt_ref[0])
    def _():
        o_ref[...] = jnp.zeros_like(o_ref)


def _moe(tile_expert, n_tiles, hs, rw, wg, wu, wd):
    p, d = hs.shape
    de = wg.shape[2]
    tm = MOE_TM
    grid_spec = pltpu.PrefetchScalarGridSpec(
        num_scalar_prefetch=2,
        grid=(p // tm,),
        in_specs=[pl.BlockSpec((tm, d), lambda t, te, nt: (t, 0)),
                  pl.BlockSpec((tm, 1), lambda t, te, nt: (t, 0)),
                  pl.BlockSpec((None, d, de), lambda t, te, nt: (te[t], 0, 0)),
                  pl.BlockSpec((None, d, de), lambda t, te, nt: (te[t], 0, 0)),
                  pl.BlockSpec((None, de, d), lambda t, te, nt: (te[t], 0, 0))],
        out_specs=pl.BlockSpec((tm, d), lambda t, te, nt: (t, 0)),
    )
    return pl.pallas_call(
        _moe_kernel,
        out_shape=jax.ShapeDtypeStruct((p, d), F32),
        grid_spec=grid_spec,
        compiler_params=pltpu.CompilerParams(dimension_semantics=("arbitrary",),
                                             vmem_limit_bytes=56 * 1024 * 1024),
        name="moe",
    )(tile_expert, n_tiles, hs, rw, wg, wu, wd)


def _final_kernel(x1_ref, ya_ref, yb_ref, gt_ref, g_ref, o_ref):
    xo = x1_ref[...] + gt_ref[0] * (ya_ref[...] + yb_ref[...])
    o_ref[...] = xo * lax.rsqrt(jnp.mean(xo * xo, axis=-1, keepdims=True) + EPS) * g_ref[...]


def _final(x1, ya, yb, gt, g, seq):
    n, d = x1.shape
    tm = min(512, seq)
    per_b = seq // tm
    row = pl.BlockSpec((tm, d), lambda i: (i, 0))
    return pl.pallas_call(
        _final_kernel,
        out_shape=jax.ShapeDtypeStruct((n, d), F32),
        grid=(n // tm,),
        in_specs=[row, row, row,
                  pl.BlockSpec((1, 1, d), lambda i: (i // per_b, 0, 0)),
                  pl.BlockSpec((1, d), lambda i: (0, 0))],
        out_specs=row,
        compiler_params=pltpu.CompilerParams(dimension_semantics=("arbitrary",),
                                             vmem_limit_bytes=VMEM_LIMIT),
        name="final",
    )(x1, ya, yb, gt, g)


def _route_tables(ids, wts, n_tok):
    tm = MOE_TM
    na = 2 * n_tok
    p_rows = na + N_EXPERTS * tm
    eid = ids.reshape(na)
    w = wts.reshape(na)
    order = jnp.argsort(eid, stable=True).astype(I32)
    counts = jnp.zeros((N_EXPERTS,), I32).at[eid].add(1)
    padded = ((counts + tm - 1) // tm) * tm
    pend = jnp.cumsum(padded)
    poff = pend - padded
    off = jnp.cumsum(counts) - counts
    es = eid[order]
    dest = poff[es] + (jnp.arange(na, dtype=I32) - off[es])
    row_token = jnp.zeros((p_rows,), I32).at[dest].set(order // 2)
    row_w = jnp.zeros((p_rows,), F32).at[dest].set(w[order])
    row_of_assign = jnp.zeros((na,), I32).at[order].set(dest)
    tile_start = jnp.arange(p_rows // tm, dtype=I32) * tm
    tile_expert = jnp.minimum(jnp.searchsorted(pend, tile_start, side="right"), N_EXPERTS - 1).astype(I32)
    n_tiles = (pend[-1] // tm).astype(I32).reshape(1)
    return row_token, row_w, row_of_assign, tile_expert, n_tiles


def kernel(x, c, w_ada, b_ada, g_norm_mix, w_in, lb_logits, g_rec_out, g_att_out, w_out, g_norm_ffn,
           w_router_group, b_router_group, w_router_expert, b_router_expert,
           w_expert_gate, w_expert_up, w_expert_down, g_final):
    bsz, seq, d = x.shape
    n = bsz * seq
    assert w_ada.shape[0] == 1, "single trunk layer"
    layer = 0
    x2 = x.reshape(n, d)

    mod = _adaln(c, w_ada[layer], b_ada[layer])
    sh1, sc1, gt1, sh2, sc2, gt2 = [m.reshape(bsz, 1, d) for m in jnp.split(mod, 6, axis=-1)]

    w_in_bf = jnp.pad(w_in[layer], ((0, 0), (0, IN_PAD - IN_COLS))).astype(BF16)
    proj = _inproj(x2, sc1, sh1, g_norm_mix[layer].reshape(1, d), w_in_bf, seq)
    kb, vb, kxb = _kvprep(proj)
    rec = _hgrn(proj, lb_logits, g_rec_out[layer], bsz, seq, layer)
    att = _dsa(proj, kb, vb, kxb, g_att_out[layer], bsz, seq)

    wr = jnp.concatenate([w_router_group[layer], w_router_expert[layer]], axis=1)
    wr = jnp.pad(wr, ((0, 0), (0, LANES - wr.shape[1])))
    br = jnp.concatenate([b_router_group[layer], b_router_expert[layer]])
    br = jnp.pad(br, (0, LANES - br.shape[0])).reshape(1, LANES)
    x1, h2, ids, wts = _outproj(rec, att, x2, gt1, sc2, sh2, g_norm_ffn[layer].reshape(1, d),
                                w_out[layer].astype(BF16), wr, br, seq)

    row_token, row_w, row_of_assign, tile_expert, n_tiles = _route_tables(ids[:, :2], wts[:, :2], n)
    hs = jnp.take(h2, row_token, axis=0)
    ys = _moe(tile_expert, n_tiles, hs, row_w.reshape(-1, 1),
              w_expert_gate[layer], w_expert_up[layer], w_expert_down[layer])
    ra = row_of_assign.reshape(n, 2)
    ya = jnp.take(ys, ra[:, 0], axis=0)
    yb = jnp.take(ys, ra[:, 1], axis=0)
    out = _final(x1, ya, yb, gt2, g_final.reshape(1, d), seq)
    return out.reshape(bsz, seq, d)
```

```python
import functools

import jax
import jax.numpy as jnp
import numpy as np
from jax import lax
from jax.experimental import pallas as pl
from jax.experimental.pallas import tpu as pltpu

F32 = jnp.float32
BF16 = jnp.bfloat16
I32 = jnp.int32

EPS = 1e-6
LANES = 128

REC_HEADS = 8
REC_D = 128
REC_CHUNK = 64
REC_SUB = 16
ATT_HEADS = 8
ATT_DH = 128
ATT_KV_HEADS = 2
ATT_GROUP = ATT_HEADS // ATT_KV_HEADS
IDX_HEADS = 8
IDX_DIM = 64
TOPK_MAX = 256
N_GROUPS = 4
EXPERTS_PER_GROUP = 8
N_EXPERTS = N_GROUPS * EXPERTS_PER_GROUP
D_EXPERT = 512

OFF_RQ = 0
OFF_RF = 1024
OFF_RI = 2048
OFF_RG = 3072
OFF_AQ = 4096
OFF_AK = 5120
OFF_AV = 5376
OFF_IQ = 5632
OFF_IK = 6144
OFF_IW = 6208
IN_COLS = 6216
IN_PAD = 6272

VMEM_LIMIT = 48 * 1024 * 1024

INT_MIN = -(2 ** 31)
KEY_NEGINF = int(np.array(-np.inf, np.float32).view(np.int32)) ^ 0x7FFFFFFF
NEG_BIG = -1e30
EXP_CLAMP = 80.0


def _silu(v):
    return v * jax.nn.sigmoid(v)


def _nt_dot(a, b):
    return lax.dot_general(a, b, (((1,), (1,)), ((), ())), preferred_element_type=F32)


def _tn_dot(a, b):
    return lax.dot_general(a, b, (((0,), (0,)), ((), ())), preferred_element_type=F32)


def _adaln_kernel(c_ref, w_ref, b_ref, o_ref):
    ca = _silu(c_ref[...])
    o_ref[...] = jnp.dot(ca, w_ref[...], preferred_element_type=F32,
                         precision=lax.Precision.HIGHEST) + b_ref[...]


def _adaln(c, w, b):
    bsz, d = c.shape
    n = w.shape[1]
    tn = 512
    return pl.pallas_call(
        _adaln_kernel,
        out_shape=jax.ShapeDtypeStruct((bsz, n), F32),
        grid=(n // tn,),
        in_specs=[pl.BlockSpec((bsz, d), lambda j: (0, 0)),
                  pl.BlockSpec((d, tn), lambda j: (0, j)),
                  pl.BlockSpec((1, tn), lambda j: (0, j))],
        out_specs=pl.BlockSpec((bsz, tn), lambda j: (0, j)),
        compiler_params=pltpu.CompilerParams(dimension_semantics=("arbitrary",),
                                             vmem_limit_bytes=VMEM_LIMIT),
        name="adaln",
    )(c, w, b.reshape(1, n))


def _norm_mod(x, g, sc, sh):
    xn = x * lax.rsqrt(jnp.mean(x * x, axis=-1, keepdims=True) + EPS)
    return xn * g * (1.0 + sc) + sh


def _inproj_kernel(x_ref, sc_ref, sh_ref, g_ref, w_ref, o_ref, h_ref):
    @pl.when(pl.program_id(1) == 0)
    def _():
        h_ref[...] = _norm_mod(x_ref[...], g_ref[...], sc_ref[0], sh_ref[0]).astype(BF16)

    o_ref[...] = jnp.dot(h_ref[...], w_ref[...], preferred_element_type=F32)


def _inproj(x2, sc, sh, g, w_bf, seq):
    n, d = x2.shape
    ncol = w_bf.shape[1]
    tm = min(1024, seq)
    tn = 896
    per_b = seq // tm
    return pl.pallas_call(
        _inproj_kernel,
        out_shape=jax.ShapeDtypeStruct((n, ncol), F32),
        grid=(n // tm, ncol // tn),
        in_specs=[pl.BlockSpec((tm, d), lambda i, j: (i, 0)),
                  pl.BlockSpec((1, 1, d), lambda i, j: (i // per_b, 0, 0)),
                  pl.BlockSpec((1, 1, d), lambda i, j: (i // per_b, 0, 0)),
                  pl.BlockSpec((1, d), lambda i, j: (0, 0)),
                  pl.BlockSpec((d, tn), lambda i, j: (0, j))],
        out_specs=pl.BlockSpec((tm, tn), lambda i, j: (i, j)),
        scratch_shapes=[pltpu.VMEM((tm, d), BF16)],
        compiler_params=pltpu.CompilerParams(dimension_semantics=("arbitrary", "arbitrary"),
                                             vmem_limit_bytes=VMEM_LIMIT),
        name="inproj",
    )(x2, sc, sh, g, w_bf)


VT_ROWS = ATT_DH + 16


def _kvprep_kernel(kv_ref, ik_ref, k_ref, vt_ref, kx_ref):
    kv = kv_ref[...]
    tm = kv.shape[0]
    k_ref[...] = kv[:, :256].astype(BF16)
    tail = jnp.where(lax.broadcasted_iota(I32, (16, tm), 0) == 0, 1.0, 0.0)
    for g in range(ATT_KV_HEADS):
        vt = kv[:, 256 + g * ATT_DH:256 + (g + 1) * ATT_DH].T
        vt_ref[g] = jnp.concatenate([vt, tail], axis=0).astype(BF16)
    kx_ref[...] = ik_ref[...][:, :IDX_DIM].astype(BF16)


def _kvprep(proj):
    n = proj.shape[0]
    tm = 512
    return pl.pallas_call(
        _kvprep_kernel,
        out_shape=(jax.ShapeDtypeStruct((n, 256), BF16),
                   jax.ShapeDtypeStruct((ATT_KV_HEADS, VT_ROWS, n), BF16),
                   jax.ShapeDtypeStruct((n, IDX_DIM), BF16)),
        grid=(n // tm,),
        in_specs=[pl.BlockSpec((tm, 512), lambda i: (i, OFF_AK // 512)),
                  pl.BlockSpec((tm, LANES), lambda i: (i, OFF_IK // LANES))],
        out_specs=(pl.BlockSpec((tm, 256), lambda i: (i, 0)),
                   pl.BlockSpec((ATT_KV_HEADS, VT_ROWS, tm), lambda i: (0, 0, i)),
                   pl.BlockSpec((tm, IDX_DIM), lambda i: (i, 0))),
        compiler_params=pltpu.CompilerParams(dimension_semantics=("arbitrary",),
                                             vmem_limit_bytes=VMEM_LIMIT),
        name="kvprep",
    )(proj, proj)


def _hgrn_kernel(q_ref, f_ref, i_ref, g_ref, lbl_ref, gout_ref, o_ref, st_ref, *, chunks, layer):
    @pl.when(pl.program_id(2) == 0)
    def _():
        st_ref[...] = jnp.zeros_like(st_ref)

    lbl = lbl_ref[...]
    e = jnp.exp(lbl - jnp.max(lbl, axis=0, keepdims=True))
    sm = e / jnp.sum(e, axis=0, keepdims=True)
    lb = jnp.sum(sm[: layer + 1], axis=0, keepdims=True)
    gout = gout_ref[...]

    c = REC_CHUNK
    nsub = c // REC_SUB
    rr = lax.broadcasted_iota(I32, (c, c), 0)
    cc = lax.broadcasted_iota(I32, (c, c), 1)
    causal = rr >= cc
    sub = lax.broadcasted_iota(I32, (c, REC_D), 0) // REC_SUB

    f = lb + (1.0 - lb) * jax.nn.sigmoid(f_ref[...])
    logf = jnp.log(f)
    k = 1.0 - f
    qf = _silu(q_ref[...]) * (REC_D ** -0.5)
    vb = i_ref[...].astype(BF16)

    logf_w = jnp.concatenate([logf[ci * c:(ci + 1) * c] for ci in range(chunks)], axis=1)
    b_all = jnp.dot(causal.astype(F32), logf_w, preferred_element_type=F32, precision=lax.Precision.HIGHEST)

    qxs, kxs, upd, q_in, decay = [], [], [], [], []
    for ci in range(chunks):
        sl = slice(ci * c, (ci + 1) * c)
        b = b_all[:, ci * REC_D:(ci + 1) * REC_D]
        kc, qc = k[sl], qf[sl]
        qparts, kparts = [], []
        for i in range(nsub - 1):
            r = b[(i + 1) * REC_SUB - 1:(i + 1) * REC_SUB, :]
            qparts.append(jnp.where(sub > i, qc * jnp.exp(b - r), 0.0))
            kparts.append(jnp.where(sub == i, kc * jnp.exp(r - b), 0.0))
        for j in range(nsub):
            r = jnp.zeros((1, REC_D), F32) if j == 0 else b[j * REC_SUB - 1:j * REC_SUB, :]
            qparts.append(jnp.where(sub == j, qc * jnp.exp(b - r), 0.0))
            kparts.append(jnp.where(sub == j, kc * jnp.exp(jnp.minimum(r - b, EXP_CLAMP)), 0.0))
        qxs.append(jnp.concatenate(qparts, axis=1).astype(BF16))
        kxs.append(jnp.concatenate(kparts, axis=1).astype(BF16))
        b_end = b[c - 1:c, :]
        upd.append((kc * jnp.exp(b_end - b)).astype(BF16))
        q_in.append((qc * jnp.exp(b)).astype(BF16))
        decay.append(jnp.exp(b_end))
    scores = [jnp.where(causal, _nt_dot(qxs[ci], kxs[ci]), 0.0).astype(BF16) for ci in range(chunks)]
    upd = [_tn_dot(vb[ci * c:(ci + 1) * c], upd[ci]) for ci in range(chunks)]
    intra = [jnp.dot(scores[ci], vb[ci * c:(ci + 1) * c], preferred_element_type=F32) for ci in range(chunks)]

    st = st_ref[...]
    outs = []
    for ci in range(chunks):
        outs.append(_nt_dot(q_in[ci], st.astype(BF16)) + intra[ci])
        st = st * decay[ci] + upd[ci]
    st_ref[...] = st

    o = jnp.concatenate(outs, axis=0)
    on = o * lax.rsqrt(jnp.mean(o * o, axis=-1, keepdims=True) + EPS)
    o_ref[...] = (on * gout * _silu(g_ref[...])).astype(o_ref.dtype)


def _hgrn(proj, lb_logits, g_out, bsz, seq, layer):
    n = proj.shape[0]
    tc = min(512, seq)
    per_b = seq // tc
    nl = lb_logits.shape[0]

    def col(off):
        return lambda b, h, c: (b * per_b + c, off // REC_D + h)

    return pl.pallas_call(
        functools.partial(_hgrn_kernel, chunks=tc // REC_CHUNK, layer=layer),
        out_shape=jax.ShapeDtypeStruct((n, REC_HEADS * REC_D), BF16),
        grid=(bsz, REC_HEADS, per_b),
        in_specs=[pl.BlockSpec((tc, REC_D), col(OFF_RQ)),
                  pl.BlockSpec((tc, REC_D), col(OFF_RF)),
                  pl.BlockSpec((tc, REC_D), col(OFF_RI)),
                  pl.BlockSpec((tc, REC_D), col(OFF_RG)),
                  pl.BlockSpec((nl, REC_D), lambda b, h, c: (0, h)),
                  pl.BlockSpec((1, REC_D), lambda b, h, c: (0, h))],
        out_specs=pl.BlockSpec((tc, REC_D), lambda b, h, c: (b * per_b + c, h)),
        scratch_shapes=[pltpu.VMEM((REC_D, REC_D), F32)],
        compiler_params=pltpu.CompilerParams(
            dimension_semantics=("arbitrary", "arbitrary", "arbitrary"),
            vmem_limit_bytes=VMEM_LIMIT),
        name="hgrn2",
    )(proj, proj, proj, proj, lb_logits, g_out.reshape(1, -1))


TQ = 128
TK = 1024
LOG2E = 1.4426950408889634
CNT_ROWS = 64
ATT_QW = 512


def _dsa_kernel(qi_ref, iw_ref, aq_ref, kx_ref, k_ref, vt_ref, gout_ref, o_ref,
                key_ref, m_ref, acc_ref, *, ksel, seq):
    t0 = pl.program_id(1) * TQ
    nkt = (t0 + TQ + TK - 1) // TK
    qpos = t0 + lax.broadcasted_iota(I32, (TK, TQ), 1)
    krow = lax.broadcasted_iota(I32, (TK, TQ), 0)

    qit = (qi_ref[...] * (IDX_DIM ** -0.5)).T
    qht = jnp.concatenate([qit[h * IDX_DIM:(h + 1) * IDX_DIM] for h in range(IDX_HEADS)],
                          axis=1).astype(BF16)
    iwt = iw_ref[...].T
    wrow = [iwt[IDX_DIM + h:IDX_DIM + h + 1] * (IDX_HEADS ** -0.5) for h in range(IDX_HEADS)]

    def score_body(kt, carry):
        k0 = pl.multiple_of(kt * TK, TK)
        rel = jnp.dot(kx_ref[pl.ds(k0, TK), :], qht, preferred_element_type=F32)
        sc = jnp.zeros((TK, TQ), F32)
        for h in range(IDX_HEADS):
            sc = sc + wrow[h] * jnp.maximum(rel[:, h * TQ:(h + 1) * TQ], 0.0)
        sc = jnp.where(k0 + krow <= qpos, sc, -jnp.inf)
        bits = lax.bitcast_convert_type(sc, I32)
        key_ref[pl.ds(k0, TK), :] = bits ^ ((bits >> 31) & 0x7FFFFFFF)
        return carry

    lax.fori_loop(0, nkt, score_body, 0)

    def count(pred):
        def body(kt, acc):
            k0 = pl.multiple_of(kt * TK, TK)
            hit = pred(key_ref[pl.ds(k0, TK), :], k0 + krow)
            return acc + jnp.sum(jnp.where(hit, 1.0, 0.0).reshape(TK // CNT_ROWS, CNT_ROWS, TQ), axis=0)
        acc = lax.fori_loop(0, nkt, body, jnp.zeros((CNT_ROWS, TQ), F32))
        return jnp.sum(acc, axis=0, keepdims=True)

    kf = float(ksel)
    thr = jnp.where(count(lambda kk, pos: kk >= 0) >= kf, 0, INT_MIN).astype(I32)

    def bit_body(i, thr):
        cand = thr | lax.shift_left(jnp.int32(1), 30 - i)
        return jnp.where(count(lambda kk, pos: kk >= cand) >= kf, cand, thr)

    thr = lax.fori_loop(0, 31, bit_body, thr)

    c_gt = count(lambda kk, pos: kk > thr)
    c_eq = count(lambda kk, pos: kk == thr)
    need = kf - c_gt
    real = thr > KEY_NEGINF
    excess = jnp.logical_and(c_eq > need, real)
    any_excess = jnp.max(jnp.where(excess, 1.0, 0.0)) > 0.0
    nbits = seq.bit_length()

    def find_last():
        def jbody(i, y):
            cand = y | lax.shift_left(jnp.int32(1), (nbits - 1) - i)
            below = count(lambda kk, pos: jnp.logical_and(kk == thr, pos < cand))
            return jnp.where(below <= need - 1.0, cand, y)
        return lax.fori_loop(0, nbits, jbody, jnp.zeros((1, TQ), I32))

    last = lax.cond(any_excess, find_last, lambda: jnp.full((1, TQ), seq, I32))
    last = jnp.where(real, last, -1)
    thr_m = jnp.maximum(thr, KEY_NEGINF)

    aq = aq_ref[...] * ((ATT_DH ** -0.5) * LOG2E)
    qgt = []
    for g in range(ATT_KV_HEADS):
        blk = [aq[:, (g * ATT_GROUP + j) * ATT_DH:(g * ATT_GROUP + j + 1) * ATT_DH].T for j in range(ATT_GROUP)]
        qgt.append(jnp.concatenate(blk, axis=1).astype(BF16))

    m_ref[...] = jnp.full(m_ref.shape, NEG_BIG, F32)
    acc_ref[...] = jnp.zeros(acc_ref.shape, F32)

    def att_body(kt, carry):
        k0 = pl.multiple_of(kt * TK, TK)
        kk = key_ref[pl.ds(k0, TK), :]
        sel = jnp.logical_or(kk > thr_m, jnp.logical_and(kk == thr_m, k0 + krow <= last))
        bias = jnp.where(sel, 0.0, NEG_BIG)
        bias = jnp.concatenate([bias] * (ATT_QW // TQ), axis=1)
        chains = [(g, slice(c, c + ATT_QW)) for g in range(ATT_KV_HEADS)
                  for c in range(0, ATT_GROUP * TQ, ATT_QW)]
        ss = [jnp.dot(k_ref[pl.ds(k0, TK), g * ATT_DH:(g + 1) * ATT_DH], qgt[g][:, cs],
                      preferred_element_type=F32) + bias for g, cs in chains]
        ps, alphas = [], []
        for (g, cs), s in zip(chains, ss):
            m_old = m_ref[g, :, cs]
            m_new = jnp.maximum(m_old, jnp.max(s, axis=0, keepdims=True))
            m_ref[g, :, cs] = m_new
            alphas.append(jnp.exp2(m_old - m_new))
            ps.append(jnp.exp2(s - m_new).astype(BF16))
        pvs = [jnp.dot(vt_ref[g, :, pl.ds(k0, TK)], p, preferred_element_type=F32)
               for (g, cs), p in zip(chains, ps)]
        for (g, cs), alpha, pv in zip(chains, alphas, pvs):
            acc_ref[g, :, cs] = alpha * acc_ref[g, :, cs] + pv
        return carry

    lax.fori_loop(0, nkt, att_body, 0)

    gout = gout_ref[...]
    for g in range(ATT_KV_HEADS):
        a = acc_ref[g]
        o = a[:ATT_DH] / a[ATT_DH:ATT_DH + 1]
        on = o * lax.rsqrt(jnp.mean(o * o, axis=0, keepdims=True) + EPS)
        for j in range(ATT_GROUP):
            hsl = slice((g * ATT_GROUP + j) * ATT_DH, (g * ATT_GROUP + j + 1) * ATT_DH)
            o_ref[:, hsl] = (on[:, j * TQ:(j + 1) * TQ].T * gout[:, hsl]).astype(o_ref.dtype)


def _dsa(proj, kb, vt, kxb, g_out, bsz, seq):
    n = proj.shape[0]
    nqb = seq // TQ
    ksel = min(TOPK_MAX, seq // 4)
    kb3 = kb.reshape(bsz, seq, 256)
    kx3 = kxb.reshape(bsz, seq, IDX_DIM)
    gq = ATT_GROUP * TQ
    return pl.pallas_call(
        functools.partial(_dsa_kernel, ksel=ksel, seq=seq),
        out_shape=jax.ShapeDtypeStruct((n, ATT_HEADS * ATT_DH), BF16),
        grid=(bsz, nqb),
        in_specs=[pl.BlockSpec((TQ, 512), lambda b, q: (b * nqb + q, OFF_IQ // 512)),
                  pl.BlockSpec((TQ, LANES), lambda b, q: (b * nqb + q, OFF_IK // LANES)),
                  pl.BlockSpec((TQ, 1024), lambda b, q: (b * nqb + q, OFF_AQ // 1024)),
                  pl.BlockSpec((None, seq, IDX_DIM), lambda b, q: (b, 0, 0)),
                  pl.BlockSpec((None, seq, 256), lambda b, q: (b, 0, 0)),
                  pl.BlockSpec((ATT_KV_HEADS, VT_ROWS, seq), lambda b, q: (0, 0, b)),
                  pl.BlockSpec((1, ATT_HEADS * ATT_DH), lambda b, q: (0, 0))],
        out_specs=pl.BlockSpec((TQ, ATT_HEADS * ATT_DH), lambda b, q: (b * nqb + q, 0)),
        scratch_shapes=[pltpu.VMEM((seq, TQ), I32),
                        pltpu.VMEM((ATT_KV_HEADS, 1, gq), F32),
                        pltpu.VMEM((ATT_KV_HEADS, VT_ROWS, gq), F32)],
        compiler_params=pltpu.CompilerParams(dimension_semantics=("arbitrary", "arbitrary"),
                                             vmem_limit_bytes=VMEM_LIMIT),
        name="dsa",
    )(proj, proj, proj, kx3, kb3, vt, g_out.reshape(1, -1))


OUT_RB = 128


def _outproj_kernel(rec_ref, att_ref, x_ref, gt_ref, sc_ref, sh_ref, g_ref, wo_ref, wrh_ref, wrl_ref, br_ref,
                    x1_ref, h2_ref, ids_ref, wts_ref):
    wo = wo_ref[...]
    half = rec_ref.shape[1]
    tm = x_ref.shape[0]
    blocks = [slice(r, r + OUT_RB) for r in range(0, tm, OUT_RB)]
    mixed = [jnp.dot(rec_ref[rs, :], wo[:half], preferred_element_type=F32)
             + jnp.dot(att_ref[rs, :], wo[half:], preferred_element_type=F32) for rs in blocks]
    his, los = [], []
    for rs, mx in zip(blocks, mixed):
        x1 = x_ref[rs, :] + gt_ref[0] * mx
        x1_ref[rs, :] = x1
        h2 = _norm_mod(x1, g_ref[...], sc_ref[0], sh_ref[0])
        hi = h2.astype(BF16)
        h2_ref[rs, :] = hi
        his.append(hi)
        los.append((h2 - hi.astype(F32)).astype(BF16))
    wrh, wrl = wrh_ref[...], wrl_ref[...]
    logits = jnp.concatenate(
        [jnp.dot(hi, wrh, preferred_element_type=F32) + jnp.dot(lo, wrh, preferred_element_type=F32)
         + jnp.dot(hi, wrl, preferred_element_type=F32) for hi, lo in zip(his, los)], axis=0) + br_ref[...]
    lane = lax.broadcasted_iota(I32, (tm, LANES), 1)
    big = jnp.int32(LANES)

    def argmax_first(vals, mask):
        mv = jnp.where(mask, vals, -jnp.inf)
        top = jnp.max(mv, axis=1, keepdims=True)
        idx = jnp.min(jnp.where(jnp.logical_and(mask, mv == top), lane, big), axis=1, keepdims=True)
        return top, idx

    gmask = lane < N_GROUPS
    gtop, gidx = argmax_first(logits, gmask)
    p_g = 1.0 / jnp.sum(jnp.where(gmask, jnp.exp(logits - gtop), 0.0), axis=1, keepdims=True)
    e_lo = N_GROUPS + gidx * EXPERTS_PER_GROUP
    emask = jnp.logical_and(lane >= e_lo, lane < e_lo + EXPERTS_PER_GROUP)
    v1, i1 = argmax_first(logits, emask)
    v2, i2 = argmax_first(logits, jnp.logical_and(emask, lane != i1))
    r = jnp.exp(v2 - v1)
    w1 = p_g / (1.0 + r)
    w2 = p_g * r / (1.0 + r)
    ids_ref[...] = jnp.where(lane == 0, i1 - N_GROUPS, jnp.where(lane == 1, i2 - N_GROUPS, 0))
    wts_ref[...] = jnp.where(lane == 0, w1, jnp.where(lane == 1, w2, 0.0))


def _outproj(rec, att, x2, gt, sc, sh, g, wo_bf, wr, br, seq):
    n, d = x2.shape
    wr_hi = wr.astype(BF16)
    wr_lo = (wr - wr_hi.astype(F32)).astype(BF16)
    tm = min(512, seq)
    per_b = seq // tm
    half = rec.shape[1]
    bspec = pl.BlockSpec((1, 1, d), lambda i: (i // per_b, 0, 0))
    return pl.pallas_call(
        _outproj_kernel,
        out_shape=(jax.ShapeDtypeStruct((n, d), F32),
                   jax.ShapeDtypeStruct((n, d), BF16),
                   jax.ShapeDtypeStruct((n, LANES), I32),
                   jax.ShapeDtypeStruct((n, LANES), F32)),
        grid=(n // tm,),
        in_specs=[pl.BlockSpec((tm, half), lambda i: (i, 0)),
                  pl.BlockSpec((tm, half), lambda i: (i, 0)),
                  pl.BlockSpec((tm, d), lambda i: (i, 0)),
                  bspec, bspec, bspec,
                  pl.BlockSpec((1, d), lambda i: (0, 0)),
                  pl.BlockSpec((2 * half, d), lambda i: (0, 0)),
                  pl.BlockSpec((d, LANES), lambda i: (0, 0)),
                  pl.BlockSpec((d, LANES), lambda i: (0, 0)),
                  pl.BlockSpec((1, LANES), lambda i: (0, 0))],
        out_specs=(pl.BlockSpec((tm, d), lambda i: (i, 0)),
                   pl.BlockSpec((tm, d), lambda i: (i, 0)),
                   pl.BlockSpec((tm, LANES), lambda i: (i, 0)),
                   pl.BlockSpec((tm, LANES), lambda i: (i, 0))),
        compiler_params=pltpu.CompilerParams(dimension_semantics=("arbitrary",),
                                             vmem_limit_bytes=VMEM_LIMIT),
        name="outproj",
    )(rec, att, x2, gt, sc, sh, g, wo_bf, wr_hi, wr_lo, br)


MOE_TM = 256


def _moe_kernel(te_ref, nt_ref, hs_ref, rw_ref, wg_ref, wu_ref, wd_ref, o_ref):
    t = pl.program_id(0)

    @pl.when(t < nt_ref[0])
    def _():
        xs = hs_ref[...]
        gte = jnp.dot(xs, wg_ref[...].astype(BF16), preferred_element_type=F32)
        up = jnp.dot(xs, wu_ref[...].astype(BF16), preferred_element_type=F32)
        act = (_silu(gte) * up).astype(BF16)
        y = jnp.dot(act, wd_ref[...].astype(BF16), preferred_element_type=F32)
        o_ref[...] = y * rw_ref[...]

    @pl.when(t >= nt_ref[0])
    def _():
        o_ref[...] = jnp.zeros_like(o_ref)


def _moe(tile_expert, n_tiles, hs, rw, wg, wu, wd):
    p, d = hs.shape
    de = wg.shape[2]
    tm = MOE_TM
    grid_spec = pltpu.PrefetchScalarGridSpec(
        num_scalar_prefetch=2,
        grid=(p // tm,),
        in_specs=[pl.BlockSpec((tm, d), lambda t, te, nt: (t, 0)),
                  pl.BlockSpec((tm, 1), lambda t, te, nt: (t, 0)),
                  pl.BlockSpec((None, d, de), lambda t, te, nt: (te[t], 0, 0)),
                  pl.BlockSpec((None, d, de), lambda t, te, nt: (te[t], 0, 0)),
                  pl.BlockSpec((None, de, d), lambda t, te, nt: (te[t], 0, 0))],
        out_specs=pl.BlockSpec((tm, d), lambda t, te, nt: (t, 0)),
    )
    return pl.pallas_call(
        _moe_kernel,
        out_shape=jax.ShapeDtypeStruct((p, d), F32),
        grid_spec=grid_spec,
        compiler_params=pltpu.CompilerParams(dimension_semantics=("arbitrary",),
                                             vmem_limit_bytes=56 * 1024 * 1024),
        name="moe",
    )(tile_expert, n_tiles, hs, rw, wg, wu, wd)


def _final_kernel(x1_ref, ya_ref, yb_ref, gt_ref, g_ref, o_ref):
    xo = x1_ref[...] + gt_ref[0] * (ya_ref[...] + yb_ref[...])
    o_ref[...] = xo * lax.rsqrt(jnp.mean(xo * xo, axis=-1, keepdims=True) + EPS) * g_ref[...]


def _final(x1, ya, yb, gt, g, seq):
    n, d = x1.shape
    tm = min(512, seq)
    per_b = seq // tm
    row = pl.BlockSpec((tm, d), lambda i: (i, 0))
    return pl.pallas_call(
        _final_kernel,
        out_shape=jax.ShapeDtypeStruct((n, d), F32),
        grid=(n // tm,),
        in_specs=[row, row, row,
                  pl.BlockSpec((1, 1, d), lambda i: (i // per_b, 0, 0)),
                  pl.BlockSpec((1, d), lambda i: (0, 0))],
        out_specs=row,
        compiler_params=pltpu.CompilerParams(dimension_semantics=("arbitrary",),
                                             vmem_limit_bytes=VMEM_LIMIT),
        name="final",
    )(x1, ya, yb, gt, g)


def _route_tables(ids, wts, n_tok):
    tm = MOE_TM
    na = 2 * n_tok
    p_rows = na + N_EXPERTS * tm
    eid = ids.reshape(na)
    w = wts.reshape(na)
    order = jnp.argsort(eid, stable=True).astype(I32)
    counts = jnp.zeros((N_EXPERTS,), I32).at[eid].add(1)
    padded = ((counts + tm - 1) // tm) * tm
    pend = jnp.cumsum(padded)
    poff = pend - padded
    off = jnp.cumsum(counts) - counts
    es = eid[order]
    dest = poff[es] + (jnp.arange(na, dtype=I32) - off[es])
    row_token = jnp.zeros((p_rows,), I32).at[dest].set(order // 2)
    row_w = jnp.zeros((p_rows,), F32).at[dest].set(w[order])
    row_of_assign = jnp.zeros((na,), I32).at[order].set(dest)
    tile_start = jnp.arange(p_rows // tm, dtype=I32) * tm
    tile_expert = jnp.minimum(jnp.searchsorted(pend, tile_start, side="right"), N_EXPERTS - 1).astype(I32)
    n_tiles = (pend[-1] // tm).astype(I32).reshape(1)
    return row_token, row_w, row_of_assign, tile_expert, n_tiles


def kernel(x, c, w_ada, b_ada, g_norm_mix, w_in, lb_logits, g_rec_out, g_att_out, w_out, g_norm_ffn,
           w_router_group, b_router_group, w_router_expert, b_router_expert,
           w_expert_gate, w_expert_up, w_expert_down, g_final):
    bsz, seq, d = x.shape
    n = bsz * seq
    assert w_ada.shape[0] == 1, "single trunk layer"
    layer = 0
    x2 = x.reshape(n, d)

    mod = _adaln(c, w_ada[layer], b_ada[layer])
    sh1, sc1, gt1, sh2, sc2, gt2 = [m.reshape(bsz, 1, d) for m in jnp.split(mod, 6, axis=-1)]

    w_in_bf = jnp.pad(w_in[layer], ((0, 0), (0, IN_PAD - IN_COLS))).astype(BF16)
    proj = _inproj(x2, sc1, sh1, g_norm_mix[layer].reshape(1, d), w_in_bf, seq)
    kb, vb, kxb = _kvprep(proj)
    rec = _hgrn(proj, lb_logits, g_rec_out[layer], bsz, seq, layer)
    att = _dsa(proj, kb, vb, kxb, g_att_out[layer], bsz, seq)

    wr = jnp.concatenate([w_router_group[layer], w_router_expert[layer]], axis=1)
    wr = jnp.pad(wr, ((0, 0), (0, LANES - wr.shape[1])))
    br = jnp.concatenate([b_router_group[layer], b_router_expert[layer]])
    br = jnp.pad(br, (0, LANES - br.shape[0])).reshape(1, LANES)
    x1, h2, ids, wts = _outproj(rec, att, x2, gt1, sc2, sh2, g_norm_ffn[layer].reshape(1, d),
                                w_out[layer].astype(BF16), wr, br, seq)

    row_token, row_w, row_of_assign, tile_expert, n_tiles = _route_tables(ids[:, :2], wts[:, :2], n)
    hs = jnp.take(h2, row_token, axis=0)
    ys = _moe(tile_expert, n_tiles, hs, row_w.reshape(-1, 1),
              w_expert_gate[layer], w_expert_up[layer], w_expert_down[layer])
    ra = row_of_assign.reshape(n, 2)
    ya = jnp.take(ys, ra[:, 0], axis=0)
    yb = jnp.take(ys, ra[:, 1], axis=0)
    out = _final(x1, ya, yb, gt2, g_final.reshape(1, d), seq)
    return out.reshape(bsz, seq, d)
```

```python
import functools

import jax
import jax.numpy as jnp
import numpy as np
from jax import lax
from jax.experimental import pallas as pl
from jax.experimental.pallas import tpu as pltpu

F32 = jnp.float32
BF16 = jnp.bfloat16
I32 = jnp.int32

EPS = 1e-6
LANES = 128

REC_HEADS = 8
REC_D = 128
REC_CHUNK = 64
REC_SUB = 16
ATT_HEADS = 8
ATT_DH = 128
ATT_KV_HEADS = 2
ATT_GROUP = ATT_HEADS // ATT_KV_HEADS
IDX_HEADS = 8
IDX_DIM = 64
TOPK_MAX = 256
N_GROUPS = 4
EXPERTS_PER_GROUP = 8
N_EXPERTS = N_GROUPS * EXPERTS_PER_GROUP
D_EXPERT = 512

OFF_RQ = 0
OFF_RF = 1024
OFF_RI = 2048
OFF_RG = 3072
OFF_AQ = 4096
OFF_AK = 5120
OFF_AV = 5376
OFF_IQ = 5632
OFF_IK = 6144
OFF_IW = 6208
IN_COLS = 6216
IN_PAD = 6272

VMEM_LIMIT = 48 * 1024 * 1024

INT_MIN = -(2 ** 31)
KEY_NEGINF = int(np.array(-np.inf, np.float32).view(np.int32)) ^ 0x7FFFFFFF
NEG_BIG = -1e30
EXP_CLAMP = 80.0


def _silu(v):
    return v * jax.nn.sigmoid(v)


def _nt_dot(a, b):
    return lax.dot_general(a, b, (((1,), (1,)), ((), ())), preferred_element_type=F32)


def _tn_dot(a, b):
    return lax.dot_general(a, b, (((0,), (0,)), ((), ())), preferred_element_type=F32)


def _pack_bf16_pairs(v):
    w = v.shape[1] // 2
    a = lax.bitcast_convert_type(v[:, :w].astype(F32), I32)
    b = lax.bitcast_convert_type(v[:, w:].astype(F32), I32)
    return a | lax.shift_right_logical(b, 16)


def _unpack_bf16_pairs(p):
    a = lax.bitcast_convert_type(p & jnp.int32(-65536), F32).astype(BF16)
    b = lax.bitcast_convert_type(lax.shift_left(p, 16), F32).astype(BF16)
    return jnp.concatenate([a, b], axis=1)


def _adaln_kernel(c_ref, w_ref, b_ref, o_ref):
    ca = _silu(c_ref[...])
    o_ref[...] = jnp.dot(ca, w_ref[...], preferred_element_type=F32,
                         precision=lax.Precision.HIGHEST) + b_ref[...]


def _adaln(c, w, b):
    bsz, d = c.shape
    n = w.shape[1]
    tn = 512
    return pl.pallas_call(
        _adaln_kernel,
        out_shape=jax.ShapeDtypeStruct((bsz, n), F32),
        grid=(n // tn,),
        in_specs=[pl.BlockSpec((bsz, d), lambda j: (0, 0)),
                  pl.BlockSpec((d, tn), lambda j: (0, j)),
                  pl.BlockSpec((1, tn), lambda j: (0, j))],
        out_specs=pl.BlockSpec((bsz, tn), lambda j: (0, j)),
        compiler_params=pltpu.CompilerParams(dimension_semantics=("arbitrary",),
                                             vmem_limit_bytes=VMEM_LIMIT),
        name="adaln",
    )(c, w, b.reshape(1, n))


def _norm_mod(x, g, sc, sh):
    xn = x * lax.rsqrt(jnp.mean(x * x, axis=-1, keepdims=True) + EPS)
    return xn * g * (1.0 + sc) + sh


def _inproj_kernel(x_ref, sc_ref, sh_ref, g_ref, w_ref, o_ref, h_ref):
    @pl.when(pl.program_id(1) == 0)
    def _():
        h_ref[...] = _norm_mod(x_ref[...], g_ref[...], sc_ref[0], sh_ref[0]).astype(BF16)

    o_ref[...] = jnp.dot(h_ref[...], w_ref[...], preferred_element_type=F32)


def _inproj(x2, sc, sh, g, w_bf, seq):
    n, d = x2.shape
    ncol = w_bf.shape[1]
    tm = min(1024, seq)
    tn = 896
    per_b = seq // tm
    return pl.pallas_call(
        _inproj_kernel,
        out_shape=jax.ShapeDtypeStruct((n, ncol), F32),
        grid=(n // tm, ncol // tn),
        in_specs=[pl.BlockSpec((tm, d), lambda i, j: (i, 0)),
                  pl.BlockSpec((1, 1, d), lambda i, j: (i // per_b, 0, 0)),
                  pl.BlockSpec((1, 1, d), lambda i, j: (i // per_b, 0, 0)),
                  pl.BlockSpec((1, d), lambda i, j: (0, 0)),
                  pl.BlockSpec((d, tn), lambda i, j: (0, j))],
        out_specs=pl.BlockSpec((tm, tn), lambda i, j: (i, j)),
        scratch_shapes=[pltpu.VMEM((tm, d), BF16)],
        compiler_params=pltpu.CompilerParams(dimension_semantics=("arbitrary", "arbitrary"),
                                             vmem_limit_bytes=VMEM_LIMIT),
        name="inproj",
    )(x2, sc, sh, g, w_bf)


VT_ROWS = ATT_DH + 16


def _kvprep_kernel(kv_ref, ik_ref, k_ref, vt_ref, kx_ref):
    kv = kv_ref[...]
    tm = kv.shape[0]
    k_ref[...] = kv[:, :256].astype(BF16)
    tail = jnp.where(lax.broadcasted_iota(I32, (16, tm), 0) == 0, 1.0, 0.0)
    for g in range(ATT_KV_HEADS):
        vt = kv[:, 256 + g * ATT_DH:256 + (g + 1) * ATT_DH].T
        vt_ref[g] = jnp.concatenate([vt, tail], axis=0).astype(BF16)
    kx_ref[...] = ik_ref[...][:, :IDX_DIM].astype(BF16)


def _kvprep(proj):
    n = proj.shape[0]
    tm = 512
    return pl.pallas_call(
        _kvprep_kernel,
        out_shape=(jax.ShapeDtypeStruct((n, 256), BF16),
                   jax.ShapeDtypeStruct((ATT_KV_HEADS, VT_ROWS, n), BF16),
                   jax.ShapeDtypeStruct((n, IDX_DIM), BF16)),
        grid=(n // tm,),
        in_specs=[pl.BlockSpec((tm, 512), lambda i: (i, OFF_AK // 512)),
                  pl.BlockSpec((tm, LANES), lambda i: (i, OFF_IK // LANES))],
        out_specs=(pl.BlockSpec((tm, 256), lambda i: (i, 0)),
                   pl.BlockSpec((ATT_KV_HEADS, VT_ROWS, tm), lambda i: (0, 0, i)),
                   pl.BlockSpec((tm, IDX_DIM), lambda i: (i, 0))),
        compiler_params=pltpu.CompilerParams(dimension_semantics=("arbitrary",),
                                             vmem_limit_bytes=VMEM_LIMIT),
        name="kvprep",
    )(proj, proj)


def _hgrn_kernel(q_ref, f_ref, i_ref, g_ref, lbl_ref, gout_ref, o_ref, st_ref, *, chunks, layer):
    @pl.when(pl.program_id(2) == 0)
    def _():
        st_ref[...] = jnp.zeros_like(st_ref)

    lbl = lbl_ref[...]
    e = jnp.exp(lbl - jnp.max(lbl, axis=0, keepdims=True))
    sm = e / jnp.sum(e, axis=0, keepdims=True)
    lb = jnp.sum(sm[: layer + 1], axis=0, keepdims=True)
    gout = gout_ref[...]

    c = REC_CHUNK
    nsub = c // REC_SUB
    rr = lax.broadcasted_iota(I32, (c, c), 0)
    cc = lax.broadcasted_iota(I32, (c, c), 1)
    causal = rr >= cc
    sub = lax.broadcasted_iota(I32, (c, REC_D), 0) // REC_SUB

    f = lb + (1.0 - lb) * jax.nn.sigmoid(f_ref[...])
    logf = jnp.log(f)
    k = 1.0 - f
    qf = _silu(q_ref[...]) * (REC_D ** -0.5)
    vb = i_ref[...].astype(BF16)

    logf_w = jnp.concatenate([logf[ci * c:(ci + 1) * c] for ci in range(chunks)], axis=1)
    b_all = jnp.dot(causal.astype(F32), logf_w, preferred_element_type=F32, precision=lax.Precision.HIGHEST)

    qxs, kxs, upd, q_in, decay = [], [], [], [], []
    for ci in range(chunks):
        sl = slice(ci * c, (ci + 1) * c)
        b = b_all[:, ci * REC_D:(ci + 1) * REC_D]
        kc, qc = k[sl], qf[sl]
        qparts, kparts = [], []
        for i in range(nsub - 1):
            r = b[(i + 1) * REC_SUB - 1:(i + 1) * REC_SUB, :]
            qparts.append(jnp.where(sub > i, qc * jnp.exp(b - r), 0.0))
            kparts.append(jnp.where(sub == i, kc * jnp.exp(r - b), 0.0))
        for j in range(nsub):
            r = jnp.zeros((1, REC_D), F32) if j == 0 else b[j * REC_SUB - 1:j * REC_SUB, :]
            qparts.append(jnp.where(sub == j, qc * jnp.exp(b - r), 0.0))
            kparts.append(jnp.where(sub == j, kc * jnp.exp(jnp.minimum(r - b, EXP_CLAMP)), 0.0))
        qxs.append(jnp.concatenate(qparts, axis=1).astype(BF16))
        kxs.append(jnp.concatenate(kparts, axis=1).astype(BF16))
        b_end = b[c - 1:c, :]
        upd.append((kc * jnp.exp(b_end - b)).astype(BF16))
        q_in.append((qc * jnp.exp(b)).astype(BF16))
        decay.append(jnp.exp(b_end))
    scores = [jnp.where(causal, _nt_dot(qxs[ci], kxs[ci]), 0.0).astype(BF16) for ci in range(chunks)]
    upd = [_tn_dot(vb[ci * c:(ci + 1) * c], upd[ci]) for ci in range(chunks)]
    intra = [jnp.dot(scores[ci], vb[ci * c:(ci + 1) * c], preferred_element_type=F32) for ci in range(chunks)]

    st = st_ref[...]
    outs = []
    for ci in range(chunks):
        outs.append(_nt_dot(q_in[ci], st.astype(BF16)) + intra[ci])
        st = st * decay[ci] + upd[ci]
    st_ref[...] = st

    o = jnp.concatenate(outs, axis=0)
    on = o * lax.rsqrt(jnp.mean(o * o, axis=-1, keepdims=True) + EPS)
    o_ref[...] = (on * gout * _silu(g_ref[...])).astype(o_ref.dtype)


def _hgrn(proj, lb_logits, g_out, bsz, seq, layer):
    n = proj.shape[0]
    tc = min(512, seq)
    per_b = seq // tc
    nl = lb_logits.shape[0]

    def col(off):
        return lambda b, h, c: (b * per_b + c, off // REC_D + h)

    return pl.pallas_call(
        functools.partial(_hgrn_kernel, chunks=tc // REC_CHUNK, layer=layer),
        out_shape=jax.ShapeDtypeStruct((n, REC_HEADS * REC_D), BF16),
        grid=(bsz, REC_HEADS, per_b),
        in_specs=[pl.BlockSpec((tc, REC_D), col(OFF_RQ)),
                  pl.BlockSpec((tc, REC_D), col(OFF_RF)),
                  pl.BlockSpec((tc, REC_D), col(OFF_RI)),
                  pl.BlockSpec((tc, REC_D), col(OFF_RG)),
                  pl.BlockSpec((nl, REC_D), lambda b, h, c: (0, h)),
                  pl.BlockSpec((1, REC_D), lambda b, h, c: (0, h))],
        out_specs=pl.BlockSpec((tc, REC_D), lambda b, h, c: (b * per_b + c, h)),
        scratch_shapes=[pltpu.VMEM((REC_D, REC_D), F32)],
        compiler_params=pltpu.CompilerParams(
            dimension_semantics=("arbitrary", "arbitrary", "arbitrary"),
            vmem_limit_bytes=VMEM_LIMIT),
        name="hgrn2",
    )(proj, proj, proj, proj, lb_logits, g_out.reshape(1, -1))


TQ = 128
TK = 1024
LOG2E = 1.4426950408889634
CNT_ROWS = 64
ATT_QW = 512


def _dsa_kernel(qi_ref, iw_ref, aq_ref, kx_ref, k_ref, vt_ref, gout_ref, o_ref,
                key_ref, m_ref, acc_ref, *, ksel, seq):
    t0 = pl.program_id(1) * TQ
    nkt = (t0 + TQ + TK - 1) // TK
    qpos = t0 + lax.broadcasted_iota(I32, (TK, TQ), 1)
    krow = lax.broadcasted_iota(I32, (TK, TQ), 0)

    qit = (qi_ref[...] * (IDX_DIM ** -0.5)).T
    qht = jnp.concatenate([qit[h * IDX_DIM:(h + 1) * IDX_DIM] for h in range(IDX_HEADS)],
                          axis=1).astype(BF16)
    iwt = iw_ref[...].T
    wrow = [iwt[IDX_DIM + h:IDX_DIM + h + 1] * (IDX_HEADS ** -0.5) for h in range(IDX_HEADS)]

    def score_body(kt, carry):
        k0 = pl.multiple_of(kt * TK, TK)
        rel = jnp.dot(kx_ref[pl.ds(k0, TK), :], qht, preferred_element_type=F32)
        sc = jnp.zeros((TK, TQ), F32)
        for h in range(IDX_HEADS):
            sc = sc + wrow[h] * jnp.maximum(rel[:, h * TQ:(h + 1) * TQ], 0.0)
        sc = jnp.where(k0 + krow <= qpos, sc, -jnp.inf)
        bits = lax.bitcast_convert_type(sc, I32)
        key_ref[pl.ds(k0, TK), :] = bits ^ ((bits >> 31) & 0x7FFFFFFF)
        return carry

    lax.fori_loop(0, nkt, score_body, 0)

    def count(pred):
        def body(kt, acc):
            k0 = pl.multiple_of(kt * TK, TK)
            hit = pred(key_ref[pl.ds(k0, TK), :], k0 + krow)
            return acc + jnp.sum(jnp.where(hit, 1.0, 0.0).reshape(TK // CNT_ROWS, CNT_ROWS, TQ), axis=0)
        acc = lax.fori_loop(0, nkt, body, jnp.zeros((CNT_ROWS, TQ), F32))
        return jnp.sum(acc, axis=0, keepdims=True)

    kf = float(ksel)
    thr = jnp.where(count(lambda kk, pos: kk >= 0) >= kf, 0, INT_MIN).astype(I32)

    def bit_body(i, thr):
        cand = thr | lax.shift_left(jnp.int32(1), 30 - i)
        return jnp.where(count(lambda kk, pos: kk >= cand) >= kf, cand, thr)

    thr = lax.fori_loop(0, 31, bit_body, thr)

    c_gt = count(lambda kk, pos: kk > thr)
    c_eq = count(lambda kk, pos: kk == thr)
    need = kf - c_gt
    real = thr > KEY_NEGINF
    excess = jnp.logical_and(c_eq > need, real)
    any_excess = jnp.max(jnp.where(excess, 1.0, 0.0)) > 0.0
    nbits = seq.bit_length()

    def find_last():
        def jbody(i, y):
            cand = y | lax.shift_left(jnp.int32(1), (nbits - 1) - i)
            below = count(lambda kk, pos: jnp.logical_and(kk == thr, pos < cand))
            return jnp.where(below <= need - 1.0, cand, y)
        return lax.fori_loop(0, nbits, jbody, jnp.zeros((1, TQ), I32))

    last = lax.cond(any_excess, find_last, lambda: jnp.full((1, TQ), seq, I32))
    last = jnp.where(real, last, -1)
    thr_m = jnp.maximum(thr, KEY_NEGINF)

    aq = aq_ref[...] * ((ATT_DH ** -0.5) * LOG2E)
    qgt = []
    for g in range(ATT_KV_HEADS):
        blk = [aq[:, (g * ATT_GROUP + j) * ATT_DH:(g * ATT_GROUP + j + 1) * ATT_DH].T for j in range(ATT_GROUP)]
        qgt.append(jnp.concatenate(blk, axis=1).astype(BF16))

    m_ref[...] = jnp.full(m_ref.shape, NEG_BIG, F32)
    acc_ref[...] = jnp.zeros(acc_ref.shape, F32)

    def att_body(kt, carry):
        k0 = pl.multiple_of(kt * TK, TK)
        kk = key_ref[pl.ds(k0, TK), :]
        sel = jnp.logical_or(kk > thr_m, jnp.logical_and(kk == thr_m, k0 + krow <= last))
        bias = jnp.where(sel, 0.0, NEG_BIG)
        bias = jnp.concatenate([bias] * (ATT_QW // TQ), axis=1)
        chains = [(g, slice(c, c + ATT_QW)) for g in range(ATT_KV_HEADS)
                  for c in range(0, ATT_GROUP * TQ, ATT_QW)]
        ss = [jnp.dot(k_ref[pl.ds(k0, TK), g * ATT_DH:(g + 1) * ATT_DH], qgt[g][:, cs],
                      preferred_element_type=F32) + bias for g, cs in chains]
        ps, alphas = [], []
        for (g, cs), s in zip(chains, ss):
            m_old = m_ref[g, :, cs]
            m_new = jnp.maximum(m_old, jnp.max(s, axis=0, keepdims=True))
            m_ref[g, :, cs] = m_new
            alphas.append(jnp.exp2(m_old - m_new))
            ps.append(jnp.exp2(s - m_new).astype(BF16))
        pvs = [jnp.dot(vt_ref[g, :, pl.ds(k0, TK)], p, preferred_element_type=F32)
               for (g, cs), p in zip(chains, ps)]
        for (g, cs), alpha, pv in zip(chains, alphas, pvs):
            acc_ref[g, :, cs] = alpha * acc_ref[g, :, cs] + pv
        return carry

    lax.fori_loop(0, nkt, att_body, 0)

    gout = gout_ref[...]
    for g in range(ATT_KV_HEADS):
        a = acc_ref[g]
        o = a[:ATT_DH] / a[ATT_DH:ATT_DH + 1]
        on = o * lax.rsqrt(jnp.mean(o * o, axis=0, keepdims=True) + EPS)
        for j in range(ATT_GROUP):
            hsl = slice((g * ATT_GROUP + j) * ATT_DH, (g * ATT_GROUP + j + 1) * ATT_DH)
            o_ref[:, hsl] = (on[:, j * TQ:(j + 1) * TQ].T * gout[:, hsl]).astype(o_ref.dtype)


def _dsa(proj, kb, vt, kxb, g_out, bsz, seq):
    n = proj.shape[0]
    nqb = seq // TQ
    ksel = min(TOPK_MAX, seq // 4)
    kb3 = kb.reshape(bsz, seq, 256)
    kx3 = kxb.reshape(bsz, seq, IDX_DIM)
    gq = ATT_GROUP * TQ
    return pl.pallas_call(
        functools.partial(_dsa_kernel, ksel=ksel, seq=seq),
        out_shape=jax.ShapeDtypeStruct((n, ATT_HEADS * ATT_DH), BF16),
        grid=(bsz, nqb),
        in_specs=[pl.BlockSpec((TQ, 512), lambda b, q: (b * nqb + q, OFF_IQ // 512)),
                  pl.BlockSpec((TQ, LANES), lambda b, q: (b * nqb + q, OFF_IK // LANES)),
                  pl.BlockSpec((TQ, 1024), lambda b, q: (b * nqb + q, OFF_AQ // 1024)),
                  pl.BlockSpec((None, seq, IDX_DIM), lambda b, q: (b, 0, 0)),
                  pl.BlockSpec((None, seq, 256), lambda b, q: (b, 0, 0)),
                  pl.BlockSpec((ATT_KV_HEADS, VT_ROWS, seq), lambda b, q: (0, 0, b)),
                  pl.BlockSpec((1, ATT_HEADS * ATT_DH), lambda b, q: (0, 0))],
        out_specs=pl.BlockSpec((TQ, ATT_HEADS * ATT_DH), lambda b, q: (b * nqb + q, 0)),
        scratch_shapes=[pltpu.VMEM((seq, TQ), I32),
                        pltpu.VMEM((ATT_KV_HEADS, 1, gq), F32),
                        pltpu.VMEM((ATT_KV_HEADS, VT_ROWS, gq), F32)],
        compiler_params=pltpu.CompilerParams(dimension_semantics=("arbitrary", "arbitrary"),
                                             vmem_limit_bytes=VMEM_LIMIT),
        name="dsa",
    )(proj, proj, proj, kx3, kb3, vt, g_out.reshape(1, -1))


OUT_RB = 128


def _outproj_kernel(rec_ref, att_ref, x_ref, gt_ref, sc_ref, sh_ref, g_ref, wo_ref, wrh_ref, wrl_ref, br_ref,
                    x1_ref, h2p_ref, oh0_ref, oh1_ref, wts_ref):
    wo = wo_ref[...]
    half = rec_ref.shape[1]
    tm = x_ref.shape[0]
    blocks = [slice(r, r + OUT_RB) for r in range(0, tm, OUT_RB)]
    mixed = [jnp.dot(rec_ref[rs, :], wo[:half], preferred_element_type=F32)
             + jnp.dot(att_ref[rs, :], wo[half:], preferred_element_type=F32) for rs in blocks]
    his, los = [], []
    for rs, mx in zip(blocks, mixed):
        x1 = x_ref[rs, :] + gt_ref[0] * mx
        x1_ref[rs, :] = x1
        h2 = _norm_mod(x1, g_ref[...], sc_ref[0], sh_ref[0])
        hi = h2.astype(BF16)
        h2p_ref[rs, :] = _pack_bf16_pairs(hi)
        his.append(hi)
        los.append((h2 - hi.astype(F32)).astype(BF16))
    wrh, wrl = wrh_ref[...], wrl_ref[...]
    logits = jnp.concatenate(
        [jnp.dot(hi, wrh, preferred_element_type=F32) + jnp.dot(lo, wrh, preferred_element_type=F32)
         + jnp.dot(hi, wrl, preferred_element_type=F32) for hi, lo in zip(his, los)], axis=0) + br_ref[...]
    lane = lax.broadcasted_iota(I32, (tm, LANES), 1)
    big = jnp.int32(LANES)

    def argmax_first(vals, mask):
        mv = jnp.where(mask, vals, -jnp.inf)
        top = jnp.max(mv, axis=1, keepdims=True)
        idx = jnp.min(jnp.where(jnp.logical_and(mask, mv == top), lane, big), axis=1, keepdims=True)
        return top, idx

    gmask = lane < N_GROUPS
    gtop, gidx = argmax_first(logits, gmask)
    p_g = 1.0 / jnp.sum(jnp.where(gmask, jnp.exp(logits - gtop), 0.0), axis=1, keepdims=True)
    e_lo = N_GROUPS + gidx * EXPERTS_PER_GROUP
    emask = jnp.logical_and(lane >= e_lo, lane < e_lo + EXPERTS_PER_GROUP)
    v1, i1 = argmax_first(logits, emask)
    v2, i2 = argmax_first(logits, jnp.logical_and(emask, lane != i1))
    r = jnp.exp(v2 - v1)
    w1 = p_g / (1.0 + r)
    w2 = p_g * r / (1.0 + r)
    oh0_ref[...] = jnp.where(lane + N_GROUPS == i1, 1.0, 0.0).astype(BF16)
    oh1_ref[...] = jnp.where(lane + N_GROUPS == i2, 1.0, 0.0).astype(BF16)
    wts_ref[...] = jnp.where(lane == 0, w1, jnp.where(lane == 1, w2, 0.0))


def _outproj(rec, att, x2, gt, sc, sh, g, wo_bf, wr, br, seq):
    n, d = x2.shape
    wr_hi = wr.astype(BF16)
    wr_lo = (wr - wr_hi.astype(F32)).astype(BF16)
    tm = min(512, seq)
    per_b = seq // tm
    half = rec.shape[1]
    bspec = pl.BlockSpec((1, 1, d), lambda i: (i // per_b, 0, 0))
    return pl.pallas_call(
        _outproj_kernel,
        out_shape=(jax.ShapeDtypeStruct((n, d), F32),
                   jax.ShapeDtypeStruct((n, d // 2), I32),
                   jax.ShapeDtypeStruct((n, LANES), BF16),
                   jax.ShapeDtypeStruct((n, LANES), BF16),
                   jax.ShapeDtypeStruct((n, LANES), F32)),
        grid=(n // tm,),
        in_specs=[pl.BlockSpec((tm, half), lambda i: (i, 0)),
                  pl.BlockSpec((tm, half), lambda i: (i, 0)),
                  pl.BlockSpec((tm, d), lambda i: (i, 0)),
                  bspec, bspec, bspec,
                  pl.BlockSpec((1, d), lambda i: (0, 0)),
                  pl.BlockSpec((2 * half, d), lambda i: (0, 0)),
                  pl.BlockSpec((d, LANES), lambda i: (0, 0)),
                  pl.BlockSpec((d, LANES), lambda i: (0, 0)),
                  pl.BlockSpec((1, LANES), lambda i: (0, 0))],
        out_specs=(pl.BlockSpec((tm, d), lambda i: (i, 0)),
                   pl.BlockSpec((tm, d // 2), lambda i: (i, 0)),
                   pl.BlockSpec((tm, LANES), lambda i: (i, 0)),
                   pl.BlockSpec((tm, LANES), lambda i: (i, 0)),
                   pl.BlockSpec((tm, LANES), lambda i: (i, 0))),
        compiler_params=pltpu.CompilerParams(dimension_semantics=("arbitrary",),
                                             vmem_limit_bytes=VMEM_LIMIT),
        name="outproj",
    )(rec, att, x2, gt, sc, sh, g, wo_bf, wr_hi, wr_lo, br)


MOE_TM = 256


def _moe_kernel(te_ref, nt_ref, hs_ref, wg_ref, wu_ref, wd_ref, o_ref):
    t = pl.program_id(0)

    @pl.when(t < nt_ref[0])
    def _():
        xs = _unpack_bf16_pairs(hs_ref[...])
        gte = jnp.dot(xs, wg_ref[...].astype(BF16), preferred_element_type=F32)
        up = jnp.dot(xs, wu_ref[...].astype(BF16), preferred_element_type=F32)
        act = (_silu(gte) * up).astype(BF16)
        o_ref[...] = jnp.dot(act, wd_ref[...].astype(BF16), preferred_element_type=F32)

    @pl.when(t >= nt_ref[0])
    def _():
        o_ref[...] = jnp.zeros_like(o_ref)


def _moe(tile_expert, n_tiles, hs, wg, wu, wd):
    p = hs.shape[0]
    d, de = wg.shape[1], wg.shape[2]
    tm = MOE_TM
    grid_spec = pltpu.PrefetchScalarGridSpec(
        num_scalar_prefetch=2,
        grid=(p // tm,),
        in_specs=[pl.BlockSpec((tm, d // 2), lambda t, te, nt: (t, 0)),
                  pl.BlockSpec((None, d, de), lambda t, te, nt: (te[t], 0, 0)),
                  pl.BlockSpec((None, d, de), lambda t, te, nt: (te[t], 0, 0)),
                  pl.BlockSpec((None, de, d), lambda t, te, nt: (te[t], 0, 0))],
        out_specs=pl.BlockSpec((tm, d), lambda t, te, nt: (t, 0)),
    )
    return pl.pallas_call(
        _moe_kernel,
        out_shape=jax.ShapeDtypeStruct((p, d), F32),
        grid_spec=grid_spec,
        compiler_params=pltpu.CompilerParams(dimension_semantics=("arbitrary",),
                                             vmem_limit_bytes=56 * 1024 * 1024),
        name="moe",
    )(tile_expert, n_tiles, hs, wg, wu, wd)


FIN_TM = 256
DMA_UNROLL = 8


def _final_kernel(dcur_ref, dnext_ref, x1_ref, wts_ref, gt_ref, g_ref, ys_ref, o_ref, buf, sem):
    i = pl.program_id(0)
    n = pl.num_programs(0)
    tm = x1_ref.shape[0]

    def row_copy(dref, slot, r, s):
        return pltpu.make_async_copy(ys_ref.at[pl.ds(dref[0, 0, 2 * r + s], 1)],
                                     buf.at[slot, s, pl.ds(r, 1)], sem.at[slot])

    def issue(dref, slot):
        def body(r, c):
            row_copy(dref, slot, r, 0).start()
            row_copy(dref, slot, r, 1).start()
            return c
        lax.fori_loop(0, tm, body, 0, unroll=DMA_UNROLL)

    @pl.when(i == 0)
    def _():
        issue(dcur_ref, 0)

    @pl.when(i + 1 < n)
    def _():
        issue(dnext_ref, (i + 1) % 2)

    slot = i % 2
    for s in range(2):
        pltpu.make_async_copy(ys_ref.at[pl.ds(0, tm)], buf.at[slot, s], sem.at[slot]).wait()
    w = wts_ref[...]
    y = w[:, 0:1] * buf[slot, 0] + w[:, 1:2] * buf[slot, 1]
    xo = x1_ref[...] + gt_ref[0] * y
    o_ref[...] = xo * lax.rsqrt(jnp.mean(xo * xo, axis=-1, keepdims=True) + EPS) * g_ref[...]


def _final(dest3, x1, wts, gt, g, ys, seq):
    n, d = x1.shape
    tm = FIN_TM
    per_b = seq // tm
    steps = n // tm
    row = pl.BlockSpec((tm, d), lambda i: (i, 0))
    smem = lambda f: pl.BlockSpec((1, 1, 2 * tm), f, memory_space=pltpu.SMEM)
    return pl.pallas_call(
        _final_kernel,
        out_shape=jax.ShapeDtypeStruct((n, d), F32),
        grid=(steps,),
        in_specs=[smem(lambda i: (i, 0, 0)),
                  smem(lambda i: (jnp.minimum(i + 1, steps - 1), 0, 0)),
                  row,
                  pl.BlockSpec((tm, LANES), lambda i: (i, 0)),
                  pl.BlockSpec((1, 1, d), lambda i: (i // per_b, 0, 0)),
                  pl.BlockSpec((1, d), lambda i: (0, 0)),
                  pl.BlockSpec(memory_space=pl.ANY)],
        out_specs=row,
        scratch_shapes=[pltpu.VMEM((2, 2, tm, d), F32), pltpu.SemaphoreType.DMA((2,))],
        compiler_params=pltpu.CompilerParams(dimension_semantics=("arbitrary",),
                                             vmem_limit_bytes=VMEM_LIMIT),
        name="final",
    )(dest3, dest3, x1, wts, gt, g, ys)


ROUTE_T = 512


def _route_kernel(oh0_ref, oh1_ref, dest_ref, cnt_ref, run_ref, tot_ref):
    ph = pl.program_id(0)
    i = pl.program_id(1)
    a0 = oh0_ref[...]
    a1 = oh1_ref[...]
    both = a0 + a1
    colsum = jnp.sum(both.astype(F32), axis=0, keepdims=True)

    @pl.when(jnp.logical_and(ph == 0, i == 0))
    def _():
        tot_ref[...] = jnp.zeros_like(tot_ref)

    @pl.when(ph == 0)
    def _():
        tot_ref[...] = tot_ref[...] + colsum

    @pl.when(ph == 1)
    def _():
        @pl.when(i == 0)
        def _():
            run_ref[...] = jnp.zeros_like(run_ref)

        tot = tot_ref[...]
        tiles = jnp.ceil(tot * (1.0 / MOE_TM))
        rr = lax.broadcasted_iota(I32, (LANES, LANES), 0)
        cc = lax.broadcasted_iota(I32, (LANES, LANES), 1)
        before = (rr < cc).astype(BF16)
        tiles8 = jnp.broadcast_to(tiles, (8, LANES)).astype(BF16)
        poff = jnp.dot(tiles8, before, preferred_element_type=F32)[0:1] * float(MOE_TM)
        t = a0.shape[0]
        r2 = lax.broadcasted_iota(I32, (t, t), 0)
        c2 = lax.broadcasted_iota(I32, (t, t), 1)
        earlier = (c2 < r2).astype(BF16)
        rank = jnp.dot(earlier, both, preferred_element_type=F32)
        tgt = poff + run_ref[...] + rank
        d0 = jnp.sum(a0.astype(F32) * tgt, axis=1, keepdims=True)
        d1 = jnp.sum(a1.astype(F32) * tgt, axis=1, keepdims=True)
        lane = lax.broadcasted_iota(I32, (t, LANES), 1)
        dest_ref[...] = jnp.where(lane == 0, d0, jnp.where(lane == 1, d1, 0.0)).astype(I32)
        run_ref[...] = run_ref[...] + colsum
        cnt_ref[...] = tot


def _route(oh0, oh1):
    n = oh0.shape[0]
    t = ROUTE_T
    blk = pl.BlockSpec((t, LANES), lambda ph, i: (i, 0))
    return pl.pallas_call(
        _route_kernel,
        out_shape=(jax.ShapeDtypeStruct((n, LANES), I32), jax.ShapeDtypeStruct((1, LANES), F32)),
        grid=(2, n // t),
        in_specs=[blk, blk],
        out_specs=(pl.BlockSpec((t, LANES), lambda ph, i: (i * ph, 0)),
                   pl.BlockSpec((1, LANES), lambda ph, i: (0, 0))),
        scratch_shapes=[pltpu.VMEM((1, LANES), F32), pltpu.VMEM((1, LANES), F32)],
        compiler_params=pltpu.CompilerParams(dimension_semantics=("arbitrary", "arbitrary"),
                                             vmem_limit_bytes=VMEM_LIMIT),
        name="route",
    )(oh0, oh1)


DISP_TM = 1024


def _dispatch_kernel(dest_ref, h_ref, hs_in_ref, hs_ref, sem):
    del hs_in_ref
    tm = h_ref.shape[0]

    def body(r, c):
        for s in range(2):
            pltpu.make_async_copy(h_ref.at[pl.ds(r, 1)], hs_ref.at[pl.ds(dest_ref[0, 0, 2 * r + s], 1)],
                                  sem).start()
        return c

    lax.fori_loop(0, tm, body, 0, unroll=DMA_UNROLL)
    for _ in range(2):
        pltpu.make_async_copy(h_ref, hs_ref.at[pl.ds(0, tm)], sem).wait()


def _dispatch(dest3, h2p, p_rows):
    n, w = h2p.shape
    tm = DISP_TM
    hs0 = jnp.zeros((p_rows, w), I32)
    return pl.pallas_call(
        _dispatch_kernel,
        out_shape=jax.ShapeDtypeStruct((p_rows, w), I32),
        grid=(n // tm,),
        in_specs=[pl.BlockSpec((1, 1, 2 * tm), lambda i: (i, 0, 0), memory_space=pltpu.SMEM),
                  pl.BlockSpec((tm, w), lambda i: (i, 0)),
                  pl.BlockSpec(memory_space=pl.ANY)],
        out_specs=pl.BlockSpec(memory_space=pl.ANY),
        scratch_shapes=[pltpu.SemaphoreType.DMA(())],
        input_output_aliases={2: 0},
        compiler_params=pltpu.CompilerParams(dimension_semantics=("arbitrary",),
                                             vmem_limit_bytes=VMEM_LIMIT),
        name="dispatch",
    )(dest3, h2p, hs0)


def _tile_tables(counts, n_tiles_max):
    cnt = counts[0, :N_EXPERTS].astype(I32)
    tiles_end = jnp.cumsum((cnt + MOE_TM - 1) // MOE_TM)
    tile = jnp.arange(n_tiles_max, dtype=I32)
    tile_expert = jnp.sum((tile[:, None] >= tiles_end[None, :]).astype(I32), axis=1)
    return jnp.minimum(tile_expert, N_EXPERTS - 1), tiles_end[-1:].astype(I32)


def kernel(x, c, w_ada, b_ada, g_norm_mix, w_in, lb_logits, g_rec_out, g_att_out, w_out, g_norm_ffn,
           w_router_group, b_router_group, w_router_expert, b_router_expert,
           w_expert_gate, w_expert_up, w_expert_down, g_final):
    bsz, seq, d = x.shape
    n = bsz * seq
    assert w_ada.shape[0] == 1, "single trunk layer"
    layer = 0
    x2 = x.reshape(n, d)

    mod = _adaln(c, w_ada[layer], b_ada[layer])
    sh1, sc1, gt1, sh2, sc2, gt2 = [m.reshape(bsz, 1, d) for m in jnp.split(mod, 6, axis=-1)]

    w_in_bf = jnp.pad(w_in[layer], ((0, 0), (0, IN_PAD - IN_COLS))).astype(BF16)
    proj = _inproj(x2, sc1, sh1, g_norm_mix[layer].reshape(1, d), w_in_bf, seq)
    kb, vb, kxb = _kvprep(proj)
    rec = _hgrn(proj, lb_logits, g_rec_out[layer], bsz, seq, layer)
    att = _dsa(proj, kb, vb, kxb, g_att_out[layer], bsz, seq)

    wr = jnp.concatenate([w_router_group[layer], w_router_expert[layer]], axis=1)
    wr = jnp.pad(wr, ((0, 0), (0, LANES - wr.shape[1])))
    br = jnp.concatenate([b_router_group[layer], b_router_expert[layer]])
    br = jnp.pad(br, (0, LANES - br.shape[0])).reshape(1, LANES)
    x1, h2p, oh0, oh1, wts = _outproj(rec, att, x2, gt1, sc2, sh2, g_norm_ffn[layer].reshape(1, d),
                                      w_out[layer].astype(BF16), wr, br, seq)

    dest, counts = _route(oh0, oh1)
    dest2 = dest[:, :2]
    p_rows = 2 * n + N_EXPERTS * MOE_TM
    tile_expert, n_tiles = _tile_tables(counts, p_rows // MOE_TM)
    hs = _dispatch(dest2.reshape(n // DISP_TM, 1, 2 * DISP_TM), h2p, p_rows)
    ys = _moe(tile_expert, n_tiles, hs, w_expert_gate[layer], w_expert_up[layer], w_expert_down[layer])
    out = _final(dest2.reshape(n // FIN_TM, 1, 2 * FIN_TM), x1, wts, gt2, g_final.reshape(1, d), ys, seq)
    return out.reshape(bsz, seq, d)
```

```python
import functools

import jax
import jax.numpy as jnp
import numpy as np
from jax import lax
from jax.experimental import pallas as pl
from jax.experimental.pallas import tpu as pltpu

F32 = jnp.float32
BF16 = jnp.bfloat16
I32 = jnp.int32

EPS = 1e-6
LANES = 128

REC_HEADS = 8
REC_D = 128
REC_CHUNK = 64
REC_SUB = 16
ATT_HEADS = 8
ATT_DH = 128
ATT_KV_HEADS = 2
ATT_GROUP = ATT_HEADS // ATT_KV_HEADS
IDX_HEADS = 8
IDX_DIM = 64
TOPK_MAX = 256
N_GROUPS = 4
EXPERTS_PER_GROUP = 8
N_EXPERTS = N_GROUPS * EXPERTS_PER_GROUP
D_EXPERT = 512

OFF_RQ = 0
OFF_RF = 1024
OFF_RI = 2048
OFF_RG = 3072
OFF_AQ = 4096
OFF_AK = 5120
OFF_AV = 5376
OFF_IQ = 5632
OFF_IK = 6144
OFF_IW = 6208
IN_COLS = 6216
IN_PAD = 6272

VMEM_LIMIT = 48 * 1024 * 1024

INT_MIN = -(2 ** 31)
KEY_NEGINF = int(np.array(-np.inf, np.float32).view(np.int32)) ^ 0x7FFFFFFF
NEG_BIG = -1e30
EXP_CLAMP = 80.0


def _silu(v):
    return v * jax.nn.sigmoid(v)


def _nt_dot(a, b):
    return lax.dot_general(a, b, (((1,), (1,)), ((), ())), preferred_element_type=F32)


def _tn_dot(a, b):
    return lax.dot_general(a, b, (((0,), (0,)), ((), ())), preferred_element_type=F32)


def _adaln_kernel(c_ref, w_ref, b_ref, o_ref):
    ca = _silu(c_ref[...])
    o_ref[...] = jnp.dot(ca, w_ref[...], preferred_element_type=F32,
                         precision=lax.Precision.HIGHEST) + b_ref[...]


def _adaln(c, w, b):
    bsz, d = c.shape
    n = w.shape[1]
    tn = 512
    return pl.pallas_call(
        _adaln_kernel,
        out_shape=jax.ShapeDtypeStruct((bsz, n), F32),
        grid=(n // tn,),
        in_specs=[pl.BlockSpec((bsz, d), lambda j: (0, 0)),
                  pl.BlockSpec((d, tn), lambda j: (0, j)),
                  pl.BlockSpec((1, tn), lambda j: (0, j))],
        out_specs=pl.BlockSpec((bsz, tn), lambda j: (0, j)),
        compiler_params=pltpu.CompilerParams(dimension_semantics=("arbitrary",),
                                             vmem_limit_bytes=VMEM_LIMIT),
        name="adaln",
    )(c, w, b.reshape(1, n))


def _norm_mod(x, g, sc, sh):
    xn = x * lax.rsqrt(jnp.mean(x * x, axis=-1, keepdims=True) + EPS)
    return xn * g * (1.0 + sc) + sh


def _inproj_kernel(x_ref, sc_ref, sh_ref, g_ref, w_ref, o_ref, h_ref):
    @pl.when(pl.program_id(1) == 0)
    def _():
        h_ref[...] = _norm_mod(x_ref[...], g_ref[...], sc_ref[0], sh_ref[0]).astype(BF16)

    o_ref[...] = jnp.dot(h_ref[...], w_ref[...], preferred_element_type=F32)


def _inproj(x2, sc, sh, g, w_bf, seq):
    n, d = x2.shape
    ncol = w_bf.shape[1]
    tm = min(1024, seq)
    tn = 896
    per_b = seq // tm
    return pl.pallas_call(
        _inproj_kernel,
        out_shape=jax.ShapeDtypeStruct((n, ncol), F32),
        grid=(n // tm, ncol // tn),
        in_specs=[pl.BlockSpec((tm, d), lambda i, j: (i, 0)),
                  pl.BlockSpec((1, 1, d), lambda i, j: (i // per_b, 0, 0)),
                  pl.BlockSpec((1, 1, d), lambda i, j: (i // per_b, 0, 0)),
                  pl.BlockSpec((1, d), lambda i, j: (0, 0)),
                  pl.BlockSpec((d, tn), lambda i, j: (0, j))],
        out_specs=pl.BlockSpec((tm, tn), lambda i, j: (i, j)),
        scratch_shapes=[pltpu.VMEM((tm, d), BF16)],
        compiler_params=pltpu.CompilerParams(dimension_semantics=("arbitrary", "arbitrary"),
                                             vmem_limit_bytes=VMEM_LIMIT),
        name="inproj",
    )(x2, sc, sh, g, w_bf)


VT_ROWS = ATT_DH + 16


def _kvprep_kernel(kv_ref, ik_ref, k_ref, vt_ref, kx_ref):
    kv = kv_ref[...]
    tm = kv.shape[0]
    k_ref[...] = kv[:, :256].astype(BF16)
    tail = jnp.where(lax.broadcasted_iota(I32, (16, tm), 0) == 0, 1.0, 0.0)
    for g in range(ATT_KV_HEADS):
        vt = kv[:, 256 + g * ATT_DH:256 + (g + 1) * ATT_DH].T
        vt_ref[g] = jnp.concatenate([vt, tail], axis=0).astype(BF16)
    kx_ref[...] = ik_ref[...][:, :IDX_DIM].astype(BF16)


def _kvprep(proj):
    n = proj.shape[0]
    tm = 512
    return pl.pallas_call(
        _kvprep_kernel,
        out_shape=(jax.ShapeDtypeStruct((n, 256), BF16),
                   jax.ShapeDtypeStruct((ATT_KV_HEADS, VT_ROWS, n), BF16),
                   jax.ShapeDtypeStruct((n, IDX_DIM), BF16)),
        grid=(n // tm,),
        in_specs=[pl.BlockSpec((tm, 512), lambda i: (i, OFF_AK // 512)),
                  pl.BlockSpec((tm, LANES), lambda i: (i, OFF_IK // LANES))],
        out_specs=(pl.BlockSpec((tm, 256), lambda i: (i, 0)),
                   pl.BlockSpec((ATT_KV_HEADS, VT_ROWS, tm), lambda i: (0, 0, i)),
                   pl.BlockSpec((tm, IDX_DIM), lambda i: (i, 0))),
        compiler_params=pltpu.CompilerParams(dimension_semantics=("arbitrary",),
                                             vmem_limit_bytes=VMEM_LIMIT),
        name="kvprep",
    )(proj, proj)


def _hgrn_kernel(q_ref, f_ref, i_ref, g_ref, lbl_ref, gout_ref, o_ref, st_ref, *, chunks, layer):
    @pl.when(pl.program_id(2) == 0)
    def _():
        st_ref[...] = jnp.zeros_like(st_ref)

    lbl = lbl_ref[...]
    e = jnp.exp(lbl - jnp.max(lbl, axis=0, keepdims=True))
    sm = e / jnp.sum(e, axis=0, keepdims=True)
    lb = jnp.sum(sm[: layer + 1], axis=0, keepdims=True)
    gout = gout_ref[...]

    c = REC_CHUNK
    nsub = c // REC_SUB
    rr = lax.broadcasted_iota(I32, (c, c), 0)
    cc = lax.broadcasted_iota(I32, (c, c), 1)
    causal = rr >= cc
    sub = lax.broadcasted_iota(I32, (c, REC_D), 0) // REC_SUB

    f = lb + (1.0 - lb) * jax.nn.sigmoid(f_ref[...])
    logf = jnp.log(f)
    k = 1.0 - f
    qf = _silu(q_ref[...]) * (REC_D ** -0.5)
    vb = i_ref[...].astype(BF16)

    logf_w = jnp.concatenate([logf[ci * c:(ci + 1) * c] for ci in range(chunks)], axis=1)
    b_all = jnp.dot(causal.astype(F32), logf_w, preferred_element_type=F32, precision=lax.Precision.HIGHEST)

    qxs, kxs, upd, q_in, decay = [], [], [], [], []
    for ci in range(chunks):
        sl = slice(ci * c, (ci + 1) * c)
        b = b_all[:, ci * REC_D:(ci + 1) * REC_D]
        kc, qc = k[sl], qf[sl]
        qparts, kparts = [], []
        for i in range(nsub - 1):
            r = b[(i + 1) * REC_SUB - 1:(i + 1) * REC_SUB, :]
            qparts.append(jnp.where(sub > i, qc * jnp.exp(b - r), 0.0))
            kparts.append(jnp.where(sub == i, kc * jnp.exp(r - b), 0.0))
        for j in range(nsub):
            r = jnp.zeros((1, REC_D), F32) if j == 0 else b[j * REC_SUB - 1:j * REC_SUB, :]
            qparts.append(jnp.where(sub == j, qc * jnp.exp(b - r), 0.0))
            kparts.append(jnp.where(sub == j, kc * jnp.exp(jnp.minimum(r - b, EXP_CLAMP)), 0.0))
        qxs.append(jnp.concatenate(qparts, axis=1).astype(BF16))
        kxs.append(jnp.concatenate(kparts, axis=1).astype(BF16))
        b_end = b[c - 1:c, :]
        upd.append((kc * jnp.exp(b_end - b)).astype(BF16))
        q_in.append((qc * jnp.exp(b)).astype(BF16))
        decay.append(jnp.exp(b_end))
    scores = [jnp.where(causal, _nt_dot(qxs[ci], kxs[ci]), 0.0).astype(BF16) for ci in range(chunks)]
    upd = [_tn_dot(vb[ci * c:(ci + 1) * c], upd[ci]) for ci in range(chunks)]
    intra = [jnp.dot(scores[ci], vb[ci * c:(ci + 1) * c], preferred_element_type=F32) for ci in range(chunks)]

    st = st_ref[...]
    outs = []
    for ci in range(chunks):
        outs.append(_nt_dot(q_in[ci], st.astype(BF16)) + intra[ci])
        st = st * decay[ci] + upd[ci]
    st_ref[...] = st

    o = jnp.concatenate(outs, axis=0)
    on = o * lax.rsqrt(jnp.mean(o * o, axis=-1, keepdims=True) + EPS)
    o_ref[...] = (on * gout * _silu(g_ref[...])).astype(o_ref.dtype)


def _hgrn(proj, lb_logits, g_out, bsz, seq, layer):
    n = proj.shape[0]
    tc = min(512, seq)
    per_b = seq // tc
    nl = lb_logits.shape[0]

    def col(off):
        return lambda b, h, c: (b * per_b + c, off // REC_D + h)

    return pl.pallas_call(
        functools.partial(_hgrn_kernel, chunks=tc // REC_CHUNK, layer=layer),
        out_shape=jax.ShapeDtypeStruct((n, REC_HEADS * REC_D), BF16),
        grid=(bsz, REC_HEADS, per_b),
        in_specs=[pl.BlockSpec((tc, REC_D), col(OFF_RQ)),
                  pl.BlockSpec((tc, REC_D), col(OFF_RF)),
                  pl.BlockSpec((tc, REC_D), col(OFF_RI)),
                  pl.BlockSpec((tc, REC_D), col(OFF_RG)),
                  pl.BlockSpec((nl, REC_D), lambda b, h, c: (0, h)),
                  pl.BlockSpec((1, REC_D), lambda b, h, c: (0, h))],
        out_specs=pl.BlockSpec((tc, REC_D), lambda b, h, c: (b * per_b + c, h)),
        scratch_shapes=[pltpu.VMEM((REC_D, REC_D), F32)],
        compiler_params=pltpu.CompilerParams(
            dimension_semantics=("arbitrary", "arbitrary", "arbitrary"),
            vmem_limit_bytes=VMEM_LIMIT),
        name="hgrn2",
    )(proj, proj, proj, proj, lb_logits, g_out.reshape(1, -1))


TQ = 128
TK = 1024
LOG2E = 1.4426950408889634
CNT_ROWS = 64
SETTLE_EVERY = 4


def _dsa_kernel(qi_ref, iw_ref, aq_ref, kx_ref, k_ref, vt_ref, gout_ref, o_ref,
                sc_ref, m_ref, acc_ref, *, ksel, seq):
    t0 = pl.program_id(1) * TQ
    nkt = (t0 + TQ + TK - 1) // TK
    qpos = t0 + lax.broadcasted_iota(I32, (TK, TQ), 1)
    krow = lax.broadcasted_iota(I32, (TK, TQ), 0)

    qit = (qi_ref[...] * (IDX_DIM ** -0.5)).T
    qht = jnp.concatenate([qit[h * IDX_DIM:(h + 1) * IDX_DIM] for h in range(IDX_HEADS)],
                          axis=1).astype(BF16)
    iwt = iw_ref[...].T
    wrow = [iwt[IDX_DIM + h:IDX_DIM + h + 1] * (IDX_HEADS ** -0.5) for h in range(IDX_HEADS)]

    def score_body(kt, carry):
        k0 = pl.multiple_of(kt * TK, TK)
        rel = jnp.dot(kx_ref[pl.ds(k0, TK), :], qht, preferred_element_type=F32)
        sc = jnp.zeros((TK, TQ), F32)
        for h in range(IDX_HEADS):
            sc = sc + wrow[h] * jnp.maximum(rel[:, h * TQ:(h + 1) * TQ], 0.0)
        sc_ref[pl.ds(k0, TK), :] = jnp.where(k0 + krow <= qpos, sc, -jnp.inf)
        return carry

    lax.fori_loop(0, nkt, score_body, 0)

    def u_to_float(u):
        key = u ^ INT_MIN
        return lax.bitcast_convert_type(key ^ ((key >> 31) & 0x7FFFFFFF), F32)

    def float_to_u(f):
        bits = lax.bitcast_convert_type(f, I32)
        return bits ^ ((bits >> 31) & 0x7FFFFFFF) ^ INT_MIN

    def count(pred):
        def body(kt, acc):
            k0 = pl.multiple_of(kt * TK, TK)
            hit = pred(sc_ref[pl.ds(k0, TK), :], k0 + krow)
            return acc + jnp.sum(jnp.where(hit, 1.0, 0.0).reshape(TK // CNT_ROWS, CNT_ROWS, TQ), axis=0)
        acc = lax.fori_loop(0, nkt, body, jnp.zeros((CNT_ROWS, TQ), F32))
        return jnp.sum(acc, axis=0, keepdims=True)

    def group_max(j, gm):
        r0 = pl.multiple_of(j * ksel, ksel)
        return jnp.maximum(gm, sc_ref[pl.ds(r0, ksel), :])

    gm = lax.fori_loop(0, nkt * (TK // ksel), group_max, jnp.full((ksel, TQ), -jnp.inf, F32))
    u_lo = float_to_u(jnp.min(gm, axis=0, keepdims=True))
    u_hi = float_to_u(jnp.max(gm, axis=0, keepdims=True))
    nlz = jnp.min(lax.clz(u_lo ^ u_hi).astype(F32)).astype(I32)
    low_mask = jnp.where(nlz >= 32, 0, lax.shift_right_logical(jnp.int32(-1), jnp.minimum(nlz, 31)))
    kf = float(ksel)

    def not_settled(cnt):
        return jnp.max(jnp.where(cnt == kf, 0.0, 1.0)) > 0.0

    def bit_cond(st):
        b, _, cnt = st
        return jnp.logical_and(b >= 0, not_settled(cnt))

    def bit_step(_, st):
        b, u, cnt = st
        cand = u | lax.shift_left(jnp.int32(1), b)
        cand_f = u_to_float(cand)
        c = count(lambda s, pos: s >= cand_f)
        take = c >= kf
        return b - 1, jnp.where(take, cand, u), jnp.where(take, c, cnt)

    def bit_body(st):
        return lax.fori_loop(0, jnp.minimum(SETTLE_EVERY, st[0] + 1), bit_step, st)

    _, u_thr, cnt_thr = lax.while_loop(
        bit_cond, bit_body, (31 - nlz, u_lo & ~low_mask, jnp.full((1, TQ), -1.0, F32)))
    thr = u_to_float(u_thr)
    real = (u_thr ^ INT_MIN) > KEY_NEGINF

    nbits = seq.bit_length()

    def resolve_ties():
        c_gt = count(lambda s, pos: s > thr)
        c_eq = count(lambda s, pos: s == thr)
        need = kf - c_gt
        excess = jnp.logical_and(c_eq > need, real)

        def find_last():
            def jbody(i, y):
                cand = y | lax.shift_left(jnp.int32(1), (nbits - 1) - i)
                below = count(lambda s, pos: jnp.logical_and(s == thr, pos < cand))
                return jnp.where(below <= need - 1.0, cand, y)
            return lax.fori_loop(0, nbits, jbody, jnp.zeros((1, TQ), I32))

        return lax.cond(jnp.max(jnp.where(excess, 1.0, 0.0)) > 0.0, find_last,
                        lambda: jnp.full((1, TQ), seq, I32))

    last = lax.cond(not_settled(cnt_thr), resolve_ties, lambda: jnp.full((1, TQ), seq, I32))
    last = jnp.where(real, last, -1)
    thr_m = jnp.where(real, thr, -jnp.inf)

    aq = aq_ref[...] * ((ATT_DH ** -0.5) * LOG2E)
    qgt = []
    for g in range(ATT_KV_HEADS):
        blk = [aq[:, (g * ATT_GROUP + j) * ATT_DH:(g * ATT_GROUP + j + 1) * ATT_DH].T for j in range(ATT_GROUP)]
        qgt.append(jnp.concatenate(blk, axis=1).astype(BF16))

    m_ref[...] = jnp.full(m_ref.shape, NEG_BIG, F32)
    acc_ref[...] = jnp.zeros(acc_ref.shape, F32)

    def att_body(kt, carry):
        k0 = pl.multiple_of(kt * TK, TK)
        sc = sc_ref[pl.ds(k0, TK), :]
        sel = jnp.logical_or(sc > thr_m, jnp.logical_and(sc == thr_m, k0 + krow <= last))
        bias = jnp.where(sel, 0.0, NEG_BIG)
        bias = jnp.concatenate([bias] * ATT_GROUP, axis=1)
        heads = range(ATT_KV_HEADS)
        ss = [jnp.dot(k_ref[pl.ds(k0, TK), g * ATT_DH:(g + 1) * ATT_DH], qgt[g],
                      preferred_element_type=F32) + bias for g in heads]
        ps, alphas = [], []
        for g in heads:
            m_old = m_ref[g]
            m_new = jnp.maximum(m_old, jnp.max(ss[g], axis=0, keepdims=True))
            m_ref[g] = m_new
            alphas.append(jnp.exp2(m_old - m_new))
            ps.append(jnp.exp2(ss[g] - m_new).astype(BF16))
        pvs = [jnp.dot(vt_ref[g, :, pl.ds(k0, TK)], ps[g], preferred_element_type=F32) for g in heads]
        for g in heads:
            acc_ref[g] = alphas[g] * acc_ref[g] + pvs[g]
        return carry

    lax.fori_loop(0, nkt, att_body, 0)

    gout = gout_ref[...]
    for g in range(ATT_KV_HEADS):
        a = acc_ref[g]
        o = a[:ATT_DH] / a[ATT_DH:ATT_DH + 1]
        on = o * lax.rsqrt(jnp.mean(o * o, axis=0, keepdims=True) + EPS)
        for j in range(ATT_GROUP):
            hsl = slice((g * ATT_GROUP + j) * ATT_DH, (g * ATT_GROUP + j + 1) * ATT_DH)
            o_ref[:, hsl] = (on[:, j * TQ:(j + 1) * TQ].T * gout[:, hsl]).astype(o_ref.dtype)


def _dsa(proj, kb, vt, kxb, g_out, bsz, seq):
    n = proj.shape[0]
    nqb = seq // TQ
    ksel = min(TOPK_MAX, seq // 4)
    assert TK % ksel == 0 and seq % TK == 0
    kb3 = kb.reshape(bsz, seq, 256)
    kx3 = kxb.reshape(bsz, seq, IDX_DIM)
    gq = ATT_GROUP * TQ
    return pl.pallas_call(
        functools.partial(_dsa_kernel, ksel=ksel, seq=seq),
        out_shape=jax.ShapeDtypeStruct((n, ATT_HEADS * ATT_DH), BF16),
        grid=(bsz, nqb),
        in_specs=[pl.BlockSpec((TQ, 512), lambda b, q: (b * nqb + q, OFF_IQ // 512)),
                  pl.BlockSpec((TQ, LANES), lambda b, q: (b * nqb + q, OFF_IK // LANES)),
                  pl.BlockSpec((TQ, 1024), lambda b, q: (b * nqb + q, OFF_AQ // 1024)),
                  pl.BlockSpec((None, seq, IDX_DIM), lambda b, q: (b, 0, 0)),
                  pl.BlockSpec((None, seq, 256), lambda b, q: (b, 0, 0)),
                  pl.BlockSpec((ATT_KV_HEADS, VT_ROWS, seq), lambda b, q: (0, 0, b)),
                  pl.BlockSpec((1, ATT_HEADS * ATT_DH), lambda b, q: (0, 0))],
        out_specs=pl.BlockSpec((TQ, ATT_HEADS * ATT_DH), lambda b, q: (b * nqb + q, 0)),
        scratch_shapes=[pltpu.VMEM((seq, TQ), F32),
                        pltpu.VMEM((ATT_KV_HEADS, 1, gq), F32),
                        pltpu.VMEM((ATT_KV_HEADS, VT_ROWS, gq), F32)],
        compiler_params=pltpu.CompilerParams(dimension_semantics=("arbitrary", "arbitrary"),
                                             vmem_limit_bytes=VMEM_LIMIT),
        name="dsa",
    )(proj, proj, proj, kx3, kb3, vt, g_out.reshape(1, -1))


OUT_RB = 128


def _outproj_kernel(rec_ref, att_ref, x_ref, gt_ref, sc_ref, sh_ref, g_ref, wo_ref, wrh_ref, wrl_ref, br_ref,
                    x1_ref, h2_ref, oh0_ref, oh1_ref, wts_ref):
    wo = wo_ref[...]
    half = rec_ref.shape[1]
    tm = x_ref.shape[0]
    blocks = [slice(r, r + OUT_RB) for r in range(0, tm, OUT_RB)]
    mixed = [jnp.dot(rec_ref[rs, :], wo[:half], preferred_element_type=F32)
             + jnp.dot(att_ref[rs, :], wo[half:], preferred_element_type=F32) for rs in blocks]
    his, los = [], []
    for rs, mx in zip(blocks, mixed):
        x1 = x_ref[rs, :] + gt_ref[0] * mx
        x1_ref[rs, :] = x1
        h2 = _norm_mod(x1, g_ref[...], sc_ref[0], sh_ref[0])
        h2_ref[rs, :] = h2
        hi = h2.astype(BF16)
        his.append(hi)
        los.append((h2 - hi.astype(F32)).astype(BF16))
    wrh, wrl = wrh_ref[...], wrl_ref[...]
    logits = jnp.concatenate(
        [jnp.dot(hi, wrh, preferred_element_type=F32) + jnp.dot(lo, wrh, preferred_element_type=F32)
         + jnp.dot(hi, wrl, preferred_element_type=F32) for hi, lo in zip(his, los)], axis=0) + br_ref[...]
    lane = lax.broadcasted_iota(I32, (tm, LANES), 1)
    big = jnp.int32(LANES)

    def argmax_first(vals, mask):
        mv = jnp.where(mask, vals, -jnp.inf)
        top = jnp.max(mv, axis=1, keepdims=True)
        idx = jnp.min(jnp.where(jnp.logical_and(mask, mv == top), lane, big), axis=1, keepdims=True)
        return top, idx

    gmask = lane < N_GROUPS
    gtop, gidx = argmax_first(logits, gmask)
    p_g = 1.0 / jnp.sum(jnp.where(gmask, jnp.exp(logits - gtop), 0.0), axis=1, keepdims=True)
    e_lo = N_GROUPS + gidx * EXPERTS_PER_GROUP
    emask = jnp.logical_and(lane >= e_lo, lane < e_lo + EXPERTS_PER_GROUP)
    v1, i1 = argmax_first(logits, emask)
    v2, i2 = argmax_first(logits, jnp.logical_and(emask, lane != i1))
    r = jnp.exp(v2 - v1)
    w1 = p_g / (1.0 + r)
    w2 = p_g * r / (1.0 + r)
    oh0_ref[...] = jnp.where(lane + N_GROUPS == i1, 1.0, 0.0).astype(BF16)
    oh1_ref[...] = jnp.where(lane + N_GROUPS == i2, 1.0, 0.0).astype(BF16)
    wts_ref[...] = jnp.where(lane == 0, w1, jnp.where(lane == 1, w2, 0.0))


def _outproj(rec, att, x2, gt, sc, sh, g, wo_bf, wr, br, seq):
    n, d = x2.shape
    wr_hi = wr.astype(BF16)
    wr_lo = (wr - wr_hi.astype(F32)).astype(BF16)
    tm = min(512, seq)
    per_b = seq // tm
    half = rec.shape[1]
    bspec = pl.BlockSpec((1, 1, d), lambda i: (i // per_b, 0, 0))
    return pl.pallas_call(
        _outproj_kernel,
        out_shape=(jax.ShapeDtypeStruct((n, d), F32),
                   jax.ShapeDtypeStruct((n, d), F32),
                   jax.ShapeDtypeStruct((n, LANES), BF16),
                   jax.ShapeDtypeStruct((n, LANES), BF16),
                   jax.ShapeDtypeStruct((n, LANES), F32)),
        grid=(n // tm,),
        in_specs=[pl.BlockSpec((tm, half), lambda i: (i, 0)),
                  pl.BlockSpec((tm, half), lambda i: (i, 0)),
                  pl.BlockSpec((tm, d), lambda i: (i, 0)),
                  bspec, bspec, bspec,
                  pl.BlockSpec((1, d), lambda i: (0, 0)),
                  pl.BlockSpec((2 * half, d), lambda i: (0, 0)),
                  pl.BlockSpec((d, LANES), lambda i: (0, 0)),
                  pl.BlockSpec((d, LANES), lambda i: (0, 0)),
                  pl.BlockSpec((1, LANES), lambda i: (0, 0))],
        out_specs=(pl.BlockSpec((tm, d), lambda i: (i, 0)),
                   pl.BlockSpec((tm, d), lambda i: (i, 0)),
                   pl.BlockSpec((tm, LANES), lambda i: (i, 0)),
                   pl.BlockSpec((tm, LANES), lambda i: (i, 0)),
                   pl.BlockSpec((tm, LANES), lambda i: (i, 0))),
        compiler_params=pltpu.CompilerParams(dimension_semantics=("arbitrary",),
                                             vmem_limit_bytes=VMEM_LIMIT),
        name="outproj",
    )(rec, att, x2, gt, sc, sh, g, wo_bf, wr_hi, wr_lo, br)


MOE_TM = 256


def _moe_kernel(te_ref, nt_ref, hs_ref, wg_ref, wu_ref, wd_ref, o_ref):
    t = pl.program_id(0)

    @pl.when(t < nt_ref[0])
    def _():
        xs = hs_ref[...].astype(BF16)
        gte = jnp.dot(xs, wg_ref[...].astype(BF16), preferred_element_type=F32)
        up = jnp.dot(xs, wu_ref[...].astype(BF16), preferred_element_type=F32)
        act = (_silu(gte) * up).astype(BF16)
        o_ref[...] = jnp.dot(act, wd_ref[...].astype(BF16), preferred_element_type=F32)

    @pl.when(t >= nt_ref[0])
    def _():
        o_ref[...] = jnp.zeros_like(o_ref)


def _moe(tile_expert, n_tiles, hs, wg, wu, wd):
    p = hs.shape[0]
    d, de = wg.shape[1], wg.shape[2]
    tm = MOE_TM
    grid_spec = pltpu.PrefetchScalarGridSpec(
        num_scalar_prefetch=2,
        grid=(p // tm,),
        in_specs=[pl.BlockSpec((tm, d), lambda t, te, nt: (t, 0)),
                  pl.BlockSpec((None, d, de), lambda t, te, nt: (te[t], 0, 0)),
                  pl.BlockSpec((None, d, de), lambda t, te, nt: (te[t], 0, 0)),
                  pl.BlockSpec((None, de, d), lambda t, te, nt: (te[t], 0, 0))],
        out_specs=pl.BlockSpec((tm, d), lambda t, te, nt: (t, 0)),
    )
    return pl.pallas_call(
        _moe_kernel,
        out_shape=jax.ShapeDtypeStruct((p, d), F32),
        grid_spec=grid_spec,
        compiler_params=pltpu.CompilerParams(dimension_semantics=("arbitrary",),
                                             vmem_limit_bytes=56 * 1024 * 1024),
        name="moe",
    )(tile_expert, n_tiles, hs, wg, wu, wd)


FIN_TM = 256
DMA_UNROLL = 8


def _final_kernel(dcur_ref, dnext_ref, x1_ref, wts_ref, gt_ref, g_ref, ys_ref, o_ref, buf, sem):
    i = pl.program_id(0)
    n = pl.num_programs(0)
    tm = x1_ref.shape[0]

    def row_copy(dref, slot, r, s):
        return pltpu.make_async_copy(ys_ref.at[pl.ds(dref[0, 0, 2 * r + s], 1)],
                                     buf.at[slot, s, pl.ds(r, 1)], sem.at[slot])

    def issue(dref, slot):
        def body(r, c):
            row_copy(dref, slot, r, 0).start()
            row_copy(dref, slot, r, 1).start()
            return c
        lax.fori_loop(0, tm, body, 0, unroll=DMA_UNROLL)

    @pl.when(i == 0)
    def _():
        issue(dcur_ref, 0)

    @pl.when(i + 1 < n)
    def _():
        issue(dnext_ref, (i + 1) % 2)

    slot = i % 2
    for s in range(2):
        pltpu.make_async_copy(ys_ref.at[pl.ds(0, tm)], buf.at[slot, s], sem.at[slot]).wait()
    w = wts_ref[...]
    y = w[:, 0:1] * buf[slot, 0] + w[:, 1:2] * buf[slot, 1]
    xo = x1_ref[...] + gt_ref[0] * y
    o_ref[...] = xo * lax.rsqrt(jnp.mean(xo * xo, axis=-1, keepdims=True) + EPS) * g_ref[...]


def _final(dest3, x1, wts, gt, g, ys, seq):
    n, d = x1.shape
    tm = FIN_TM
    per_b = seq // tm
    steps = n // tm
    row = pl.BlockSpec((tm, d), lambda i: (i, 0))
    smem = lambda f: pl.BlockSpec((1, 1, 2 * tm), f, memory_space=pltpu.SMEM)
    return pl.pallas_call(
        _final_kernel,
        out_shape=jax.ShapeDtypeStruct((n, d), F32),
        grid=(steps,),
        in_specs=[smem(lambda i: (i, 0, 0)),
                  smem(lambda i: (jnp.minimum(i + 1, steps - 1), 0, 0)),
                  row,
                  pl.BlockSpec((tm, LANES), lambda i: (i, 0)),
                  pl.BlockSpec((1, 1, d), lambda i: (i // per_b, 0, 0)),
                  pl.BlockSpec((1, d), lambda i: (0, 0)),
                  pl.BlockSpec(memory_space=pl.ANY)],
        out_specs=row,
        scratch_shapes=[pltpu.VMEM((2, 2, tm, d), F32), pltpu.SemaphoreType.DMA((2,))],
        compiler_params=pltpu.CompilerParams(dimension_semantics=("arbitrary",),
                                             vmem_limit_bytes=VMEM_LIMIT),
        name="final",
    )(dest3, dest3, x1, wts, gt, g, ys)


ROUTE_T = 512


def _route_kernel(oh0_ref, oh1_ref, dest_ref, cnt_ref, run_ref, tot_ref):
    ph = pl.program_id(0)
    i = pl.program_id(1)
    a0 = oh0_ref[...]
    a1 = oh1_ref[...]
    both = a0 + a1
    colsum = jnp.sum(both.astype(F32), axis=0, keepdims=True)

    @pl.when(jnp.logical_and(ph == 0, i == 0))
    def _():
        tot_ref[...] = jnp.zeros_like(tot_ref)

    @pl.when(ph == 0)
    def _():
        tot_ref[...] = tot_ref[...] + colsum

    @pl.when(ph == 1)
    def _():
        @pl.when(i == 0)
        def _():
            run_ref[...] = jnp.zeros_like(run_ref)

        tot = tot_ref[...]
        tiles = jnp.ceil(tot * (1.0 / MOE_TM))
        rr = lax.broadcasted_iota(I32, (LANES, LANES), 0)
        cc = lax.broadcasted_iota(I32, (LANES, LANES), 1)
        before = (rr < cc).astype(BF16)
        tiles8 = jnp.broadcast_to(tiles, (8, LANES)).astype(BF16)
        poff = jnp.dot(tiles8, before, preferred_element_type=F32)[0:1] * float(MOE_TM)
        t = a0.shape[0]
        r2 = lax.broadcasted_iota(I32, (t, t), 0)
        c2 = lax.broadcasted_iota(I32, (t, t), 1)
        earlier = (c2 < r2).astype(BF16)
        rank = jnp.dot(earlier, both, preferred_element_type=F32)
        tgt = poff + run_ref[...] + rank
        d0 = jnp.sum(a0.astype(F32) * tgt, axis=1, keepdims=True)
        d1 = jnp.sum(a1.astype(F32) * tgt, axis=1, keepdims=True)
        lane = lax.broadcasted_iota(I32, (t, LANES), 1)
        dest_ref[...] = jnp.where(lane == 0, d0, jnp.where(lane == 1, d1, 0.0)).astype(I32)
        run_ref[...] = run_ref[...] + colsum
        cnt_ref[...] = tot


def _route(oh0, oh1):
    n = oh0.shape[0]
    t = ROUTE_T
    blk = pl.BlockSpec((t, LANES), lambda ph, i: (i, 0))
    return pl.pallas_call(
        _route_kernel,
        out_shape=(jax.ShapeDtypeStruct((n, LANES), I32), jax.ShapeDtypeStruct((1, LANES), F32)),
        grid=(2, n // t),
        in_specs=[blk, blk],
        out_specs=(pl.BlockSpec((t, LANES), lambda ph, i: (i * ph, 0)),
                   pl.BlockSpec((1, LANES), lambda ph, i: (0, 0))),
        scratch_shapes=[pltpu.VMEM((1, LANES), F32), pltpu.VMEM((1, LANES), F32)],
        compiler_params=pltpu.CompilerParams(dimension_semantics=("arbitrary", "arbitrary"),
                                             vmem_limit_bytes=VMEM_LIMIT),
        name="route",
    )(oh0, oh1)


DISP_TM = 1024


def _dispatch_kernel(dest_ref, h_ref, hs_in_ref, hs_ref, sem):
    del hs_in_ref
    tm = h_ref.shape[0]

    def body(r, c):
        for s in range(2):
            pltpu.make_async_copy(h_ref.at[pl.ds(r, 1)], hs_ref.at[pl.ds(dest_ref[0, 0, 2 * r + s], 1)],
                                  sem).start()
        return c

    lax.fori_loop(0, tm, body, 0, unroll=DMA_UNROLL)
    for _ in range(2):
        pltpu.make_async_copy(h_ref, hs_ref.at[pl.ds(0, tm)], sem).wait()


def _dispatch(dest3, h2, p_rows):
    n, w = h2.shape
    tm = DISP_TM
    hs0 = jnp.zeros((p_rows, w), h2.dtype)
    return pl.pallas_call(
        _dispatch_kernel,
        out_shape=jax.ShapeDtypeStruct((p_rows, w), h2.dtype),
        grid=(n // tm,),
        in_specs=[pl.BlockSpec((1, 1, 2 * tm), lambda i: (i, 0, 0), memory_space=pltpu.SMEM),
                  pl.BlockSpec((tm, w), lambda i: (i, 0)),
                  pl.BlockSpec(memory_space=pl.ANY)],
        out_specs=pl.BlockSpec(memory_space=pl.ANY),
        scratch_shapes=[pltpu.SemaphoreType.DMA(())],
        input_output_aliases={2: 0},
        compiler_params=pltpu.CompilerParams(dimension_semantics=("arbitrary",),
                                             vmem_limit_bytes=VMEM_LIMIT),
        name="dispatch",
    )(dest3, h2, hs0)


def _tile_tables(counts, n_tiles_max):
    cnt = counts[0, :N_EXPERTS].astype(I32)
    tiles_end = jnp.cumsum((cnt + MOE_TM - 1) // MOE_TM)
    tile = jnp.arange(n_tiles_max, dtype=I32)
    tile_expert = jnp.sum((tile[:, None] >= tiles_end[None, :]).astype(I32), axis=1)
    return jnp.minimum(tile_expert, N_EXPERTS - 1), tiles_end[-1:].astype(I32)


def kernel(x, c, w_ada, b_ada, g_norm_mix, w_in, lb_logits, g_rec_out, g_att_out, w_out, g_norm_ffn,
           w_router_group, b_router_group, w_router_expert, b_router_expert,
           w_expert_gate, w_expert_up, w_expert_down, g_final):
    bsz, seq, d = x.shape
    n = bsz * seq
    assert w_ada.shape[0] == 1, "single trunk layer"
    layer = 0
    x2 = x.reshape(n, d)

    mod = _adaln(c, w_ada[layer], b_ada[layer])
    sh1, sc1, gt1, sh2, sc2, gt2 = [m.reshape(bsz, 1, d) for m in jnp.split(mod, 6, axis=-1)]

    w_in_bf = jnp.pad(w_in[layer], ((0, 0), (0, IN_PAD - IN_COLS))).astype(BF16)
    proj = _inproj(x2, sc1, sh1, g_norm_mix[layer].reshape(1, d), w_in_bf, seq)
    kb, vb, kxb = _kvprep(proj)
    rec = _hgrn(proj, lb_logits, g_rec_out[layer], bsz, seq, layer)
    att = _dsa(proj, kb, vb, kxb, g_att_out[layer], bsz, seq)

    wr = jnp.concatenate([w_router_group[layer], w_router_expert[layer]], axis=1)
    wr = jnp.pad(wr, ((0, 0), (0, LANES - wr.shape[1])))
    br = jnp.concatenate([b_router_group[layer], b_router_expert[layer]])
    br = jnp.pad(br, (0, LANES - br.shape[0])).reshape(1, LANES)
    x1, h2, oh0, oh1, wts = _outproj(rec, att, x2, gt1, sc2, sh2, g_norm_ffn[layer].reshape(1, d),
                                      w_out[layer].astype(BF16), wr, br, seq)

    dest, counts = _route(oh0, oh1)
    dest2 = dest[:, :2]
    p_rows = 2 * n + N_EXPERTS * MOE_TM
    tile_expert, n_tiles = _tile_tables(counts, p_rows // MOE_TM)
    hs = _dispatch(dest2.reshape(n // DISP_TM, 1, 2 * DISP_TM), h2, p_rows)
    ys = _moe(tile_expert, n_tiles, hs, w_expert_gate[layer], w_expert_up[layer], w_expert_down[layer])
    out = _final(dest2.reshape(n // FIN_TM, 1, 2 * FIN_TM), x1, wts, gt2, g_final.reshape(1, d), ys, seq)
    return out.reshape(bsz, seq, d)
```

```python
import functools

import jax
import jax.numpy as jnp
import numpy as np
from jax import lax
from jax.experimental import pallas as pl
from jax.experimental.pallas import tpu as pltpu

F32 = jnp.float32
BF16 = jnp.bfloat16
I32 = jnp.int32

EPS = 1e-6
LANES = 128

REC_HEADS = 8
REC_D = 128
REC_CHUNK = 64
REC_SUB = 16
ATT_HEADS = 8
ATT_DH = 128
ATT_KV_HEADS = 2
ATT_GROUP = ATT_HEADS // ATT_KV_HEADS
IDX_HEADS = 8
IDX_DIM = 64
TOPK_MAX = 256
N_GROUPS = 4
EXPERTS_PER_GROUP = 8
N_EXPERTS = N_GROUPS * EXPERTS_PER_GROUP
D_EXPERT = 512

OFF_RQ = 0
OFF_RF = 1024
OFF_RI = 2048
OFF_RG = 3072
OFF_AQ = 4096
OFF_AK = 5120
OFF_AV = 5376
OFF_IQ = 5632
OFF_IK = 6144
OFF_IW = 6208
IN_COLS = 6216
IN_PAD = 6272

VMEM_LIMIT = 48 * 1024 * 1024

INT_MIN = -(2 ** 31)
KEY_NEGINF = int(np.array(-np.inf, np.float32).view(np.int32)) ^ 0x7FFFFFFF
NEG_BIG = -1e30
EXP_CLAMP = 80.0


def _silu(v):
    return v * jax.nn.sigmoid(v)


def _nt_dot(a, b):
    return lax.dot_general(a, b, (((1,), (1,)), ((), ())), preferred_element_type=F32)


def _tn_dot(a, b):
    return lax.dot_general(a, b, (((0,), (0,)), ((), ())), preferred_element_type=F32)


def _adaln_kernel(c_ref, w_ref, b_ref, o_ref):
    ca = _silu(c_ref[...])
    o_ref[...] = jnp.dot(ca, w_ref[...], preferred_element_type=F32,
                         precision=lax.Precision.HIGHEST) + b_ref[...]


def _adaln(c, w, b):
    bsz, d = c.shape
    n = w.shape[1]
    tn = 512
    return pl.pallas_call(
        _adaln_kernel,
        out_shape=jax.ShapeDtypeStruct((bsz, n), F32),
        grid=(n // tn,),
        in_specs=[pl.BlockSpec((bsz, d), lambda j: (0, 0)),
                  pl.BlockSpec((d, tn), lambda j: (0, j)),
                  pl.BlockSpec((1, tn), lambda j: (0, j))],
        out_specs=pl.BlockSpec((bsz, tn), lambda j: (0, j)),
        compiler_params=pltpu.CompilerParams(dimension_semantics=("arbitrary",),
                                             vmem_limit_bytes=VMEM_LIMIT),
        name="adaln",
    )(c, w, b.reshape(1, n))


def _norm_mod(x, g, sc, sh):
    xn = x * lax.rsqrt(jnp.mean(x * x, axis=-1, keepdims=True) + EPS)
    return xn * g * (1.0 + sc) + sh


def _inproj_kernel(x_ref, sc_ref, sh_ref, g_ref, w_ref, o_ref, h_ref):
    @pl.when(pl.program_id(1) == 0)
    def _():
        h_ref[...] = _norm_mod(x_ref[...], g_ref[...], sc_ref[0], sh_ref[0]).astype(BF16)

    o_ref[...] = jnp.dot(h_ref[...], w_ref[...], preferred_element_type=F32)


def _inproj(x2, sc, sh, g, w_bf, seq):
    n, d = x2.shape
    ncol = w_bf.shape[1]
    tm = min(1024, seq)
    tn = 896
    per_b = seq // tm
    return pl.pallas_call(
        _inproj_kernel,
        out_shape=jax.ShapeDtypeStruct((n, ncol), F32),
        grid=(n // tm, ncol // tn),
        in_specs=[pl.BlockSpec((tm, d), lambda i, j: (i, 0)),
                  pl.BlockSpec((1, 1, d), lambda i, j: (i // per_b, 0, 0)),
                  pl.BlockSpec((1, 1, d), lambda i, j: (i // per_b, 0, 0)),
                  pl.BlockSpec((1, d), lambda i, j: (0, 0)),
                  pl.BlockSpec((d, tn), lambda i, j: (0, j))],
        out_specs=pl.BlockSpec((tm, tn), lambda i, j: (i, j)),
        scratch_shapes=[pltpu.VMEM((tm, d), BF16)],
        compiler_params=pltpu.CompilerParams(dimension_semantics=("arbitrary", "arbitrary"),
                                             vmem_limit_bytes=VMEM_LIMIT),
        name="inproj",
    )(x2, sc, sh, g, w_bf)


VT_ROWS = ATT_DH + 16


def _kvprep_kernel(kv_ref, ik_ref, k_ref, vt_ref, kx_ref):
    kv = kv_ref[...]
    tm = kv.shape[0]
    k_ref[...] = kv[:, :256].astype(BF16)
    tail = jnp.where(lax.broadcasted_iota(I32, (16, tm), 0) == 0, 1.0, 0.0)
    for g in range(ATT_KV_HEADS):
        vt = kv[:, 256 + g * ATT_DH:256 + (g + 1) * ATT_DH].T
        vt_ref[g] = jnp.concatenate([vt, tail], axis=0).astype(BF16)
    kx_ref[...] = ik_ref[...][:, :IDX_DIM].astype(BF16)


def _kvprep(proj):
    n = proj.shape[0]
    tm = 512
    return pl.pallas_call(
        _kvprep_kernel,
        out_shape=(jax.ShapeDtypeStruct((n, 256), BF16),
                   jax.ShapeDtypeStruct((ATT_KV_HEADS, VT_ROWS, n), BF16),
                   jax.ShapeDtypeStruct((n, IDX_DIM), BF16)),
        grid=(n // tm,),
        in_specs=[pl.BlockSpec((tm, 512), lambda i: (i, OFF_AK // 512)),
                  pl.BlockSpec((tm, LANES), lambda i: (i, OFF_IK // LANES))],
        out_specs=(pl.BlockSpec((tm, 256), lambda i: (i, 0)),
                   pl.BlockSpec((ATT_KV_HEADS, VT_ROWS, tm), lambda i: (0, 0, i)),
                   pl.BlockSpec((tm, IDX_DIM), lambda i: (i, 0))),
        compiler_params=pltpu.CompilerParams(dimension_semantics=("arbitrary",),
                                             vmem_limit_bytes=VMEM_LIMIT),
        name="kvprep",
    )(proj, proj)


def _hgrn_kernel(q_ref, f_ref, i_ref, g_ref, lbl_ref, gout_ref, o_ref, st_ref, *, chunks, layer):
    @pl.when(pl.program_id(2) == 0)
    def _():
        st_ref[...] = jnp.zeros_like(st_ref)

    lbl = lbl_ref[...]
    e = jnp.exp(lbl - jnp.max(lbl, axis=0, keepdims=True))
    sm = e / jnp.sum(e, axis=0, keepdims=True)
    lb = jnp.sum(sm[: layer + 1], axis=0, keepdims=True)
    gout = gout_ref[...]

    c = REC_CHUNK
    nsub = c // REC_SUB
    rr = lax.broadcasted_iota(I32, (c, c), 0)
    cc = lax.broadcasted_iota(I32, (c, c), 1)
    causal = rr >= cc
    sub = lax.broadcasted_iota(I32, (c, REC_D), 0) // REC_SUB

    f = lb + (1.0 - lb) * jax.nn.sigmoid(f_ref[...])
    logf = jnp.log(f)
    k = 1.0 - f
    qf = _silu(q_ref[...]) * (REC_D ** -0.5)
    vb = i_ref[...].astype(BF16)

    logf_w = jnp.concatenate([logf[ci * c:(ci + 1) * c] for ci in range(chunks)], axis=1)
    b_all = jnp.dot(causal.astype(F32), logf_w, preferred_element_type=F32, precision=lax.Precision.HIGHEST)

    qxs, kxs, upd, q_in, decay = [], [], [], [], []
    for ci in range(chunks):
        sl = slice(ci * c, (ci + 1) * c)
        b = b_all[:, ci * REC_D:(ci + 1) * REC_D]
        kc, qc = k[sl], qf[sl]
        qparts, kparts = [], []
        for i in range(nsub - 1):
            r = b[(i + 1) * REC_SUB - 1:(i + 1) * REC_SUB, :]
            qparts.append(jnp.where(sub > i, qc * jnp.exp(b - r), 0.0))
            kparts.append(jnp.where(sub == i, kc * jnp.exp(r - b), 0.0))
        for j in range(nsub):
            r = jnp.zeros((1, REC_D), F32) if j == 0 else b[j * REC_SUB - 1:j * REC_SUB, :]
            qparts.append(jnp.where(sub == j, qc * jnp.exp(b - r), 0.0))
            kparts.append(jnp.where(sub == j, kc * jnp.exp(jnp.minimum(r - b, EXP_CLAMP)), 0.0))
        qxs.append(jnp.concatenate(qparts, axis=1).astype(BF16))
        kxs.append(jnp.concatenate(kparts, axis=1).astype(BF16))
        b_end = b[c - 1:c, :]
        upd.append((kc * jnp.exp(b_end - b)).astype(BF16))
        q_in.append((qc * jnp.exp(b)).astype(BF16))
        decay.append(jnp.exp(b_end))
    scores = [jnp.where(causal, _nt_dot(qxs[ci], kxs[ci]), 0.0).astype(BF16) for ci in range(chunks)]
    upd = [_tn_dot(vb[ci * c:(ci + 1) * c], upd[ci]) for ci in range(chunks)]
    intra = [jnp.dot(scores[ci], vb[ci * c:(ci + 1) * c], preferred_element_type=F32) for ci in range(chunks)]

    st = st_ref[...]
    outs = []
    for ci in range(chunks):
        outs.append(_nt_dot(q_in[ci], st.astype(BF16)) + intra[ci])
        st = st * decay[ci] + upd[ci]
    st_ref[...] = st

    o = jnp.concatenate(outs, axis=0)
    on = o * lax.rsqrt(jnp.mean(o * o, axis=-1, keepdims=True) + EPS)
    o_ref[...] = (on * gout * _silu(g_ref[...])).astype(o_ref.dtype)


def _hgrn(proj, lb_logits, g_out, bsz, seq, layer):
    n = proj.shape[0]
    tc = min(512, seq)
    per_b = seq // tc
    nl = lb_logits.shape[0]

    def col(off):
        return lambda b, h, c: (b * per_b + c, off // REC_D + h)

    return pl.pallas_call(
        functools.partial(_hgrn_kernel, chunks=tc // REC_CHUNK, layer=layer),
        out_shape=jax.ShapeDtypeStruct((n, REC_HEADS * REC_D), BF16),
        grid=(bsz, REC_HEADS, per_b),
        in_specs=[pl.BlockSpec((tc, REC_D), col(OFF_RQ)),
                  pl.BlockSpec((tc, REC_D), col(OFF_RF)),
                  pl.BlockSpec((tc, REC_D), col(OFF_RI)),
                  pl.BlockSpec((tc, REC_D), col(OFF_RG)),
                  pl.BlockSpec((nl, REC_D), lambda b, h, c: (0, h)),
                  pl.BlockSpec((1, REC_D), lambda b, h, c: (0, h))],
        out_specs=pl.BlockSpec((tc, REC_D), lambda b, h, c: (b * per_b + c, h)),
        scratch_shapes=[pltpu.VMEM((REC_D, REC_D), F32)],
        compiler_params=pltpu.CompilerParams(
            dimension_semantics=("arbitrary", "arbitrary", "arbitrary"),
            vmem_limit_bytes=VMEM_LIMIT),
        name="hgrn2",
    )(proj, proj, proj, proj, lb_logits, g_out.reshape(1, -1))


TQ = 128
TK = 1024
LOG2E = 1.4426950408889634
CNT_ROWS = 64
SETTLE_EVERY = 4
VALUE_SPLITS = 4


def _dsa_kernel(qi_ref, iw_ref, aq_ref, kx_ref, k_ref, vt_ref, gout_ref, o_ref,
                sc_ref, m_ref, acc_ref, *, ksel, seq):
    t0 = pl.program_id(1) * TQ
    nkt = (t0 + TQ + TK - 1) // TK
    qpos = t0 + lax.broadcasted_iota(I32, (TK, TQ), 1)
    krow = lax.broadcasted_iota(I32, (TK, TQ), 0)

    qit = (qi_ref[...] * (IDX_DIM ** -0.5)).T
    qht = jnp.concatenate([qit[h * IDX_DIM:(h + 1) * IDX_DIM] for h in range(IDX_HEADS)],
                          axis=1).astype(BF16)
    iwt = iw_ref[...].T
    wrow = [iwt[IDX_DIM + h:IDX_DIM + h + 1] * (IDX_HEADS ** -0.5) for h in range(IDX_HEADS)]

    def score_body(kt, carry):
        k0 = pl.multiple_of(kt * TK, TK)
        rel = jnp.dot(kx_ref[pl.ds(k0, TK), :], qht, preferred_element_type=F32)
        sc = jnp.zeros((TK, TQ), F32)
        for h in range(IDX_HEADS):
            sc = sc + wrow[h] * jnp.maximum(rel[:, h * TQ:(h + 1) * TQ], 0.0)
        sc_ref[pl.ds(k0, TK), :] = jnp.where(k0 + krow <= qpos, sc, -jnp.inf)
        return carry

    lax.fori_loop(0, nkt, score_body, 0)

    def u_to_float(u):
        key = u ^ INT_MIN
        return lax.bitcast_convert_type(key ^ ((key >> 31) & 0x7FFFFFFF), F32)

    def float_to_u(f):
        bits = lax.bitcast_convert_type(f, I32)
        return bits ^ ((bits >> 31) & 0x7FFFFFFF) ^ INT_MIN

    def count(pred):
        def body(kt, acc):
            k0 = pl.multiple_of(kt * TK, TK)
            hit = pred(sc_ref[pl.ds(k0, TK), :], k0 + krow)
            return acc + jnp.sum(jnp.where(hit, 1.0, 0.0).reshape(TK // CNT_ROWS, CNT_ROWS, TQ), axis=0)
        acc = lax.fori_loop(0, nkt, body, jnp.zeros((CNT_ROWS, TQ), F32))
        return jnp.sum(acc, axis=0, keepdims=True)

    def group_max(j, gm):
        r0 = pl.multiple_of(j * ksel, ksel)
        return jnp.maximum(gm, sc_ref[pl.ds(r0, ksel), :])

    gm = lax.fori_loop(0, nkt * (TK // ksel), group_max, jnp.full((ksel, TQ), -jnp.inf, F32))
    kf = float(ksel)

    def ult(a, b):
        return (a ^ INT_MIN) < (b ^ INT_MIN)

    def is_open(lo, hi):
        return ult(jnp.int32(1), hi - lo)

    def not_settled(cnt):
        return jnp.where(cnt == kf, 0.0, 1.0)

    def search_cond(st):
        _, lo, hi, cnt = st
        return jnp.max(jnp.where(is_open(lo, hi), not_settled(cnt), 0.0)) > 0.0

    def search_step(_, st):
        it, lo, hi, cnt = st
        mid_u = lo + lax.shift_right_logical(hi - lo, 1)
        mid_v = float_to_u(0.5 * (u_to_float(lo) + u_to_float(hi)))
        use_v = jnp.logical_and(it < VALUE_SPLITS, jnp.logical_and(ult(lo, mid_v), ult(mid_v, hi)))
        cand = jnp.where(is_open(lo, hi), jnp.where(use_v, mid_v, mid_u), lo)
        cand_f = u_to_float(cand)
        c = count(lambda s, pos: s >= cand_f)
        take = c >= kf
        return it + 1, jnp.where(take, cand, lo), jnp.where(take, hi, cand), jnp.where(take, c, cnt)

    def search_body(st):
        return lax.fori_loop(0, SETTLE_EVERY, search_step, st)

    _, u_thr, _, cnt_thr = lax.while_loop(
        search_cond, search_body,
        (jnp.int32(0), float_to_u(jnp.min(gm, axis=0, keepdims=True)),
         float_to_u(jnp.max(gm, axis=0, keepdims=True)) + 1, jnp.full((1, TQ), -1.0, F32)))
    thr = u_to_float(u_thr)
    real = (u_thr ^ INT_MIN) > KEY_NEGINF

    nbits = seq.bit_length()

    def resolve_ties():
        c_gt = count(lambda s, pos: s > thr)
        c_eq = count(lambda s, pos: s == thr)
        need = kf - c_gt
        excess = jnp.logical_and(c_eq > need, real)

        def find_last():
            def jbody(i, y):
                cand = y | lax.shift_left(jnp.int32(1), (nbits - 1) - i)
                below = count(lambda s, pos: jnp.logical_and(s == thr, pos < cand))
                return jnp.where(below <= need - 1.0, cand, y)
            return lax.fori_loop(0, nbits, jbody, jnp.zeros((1, TQ), I32))

        return lax.cond(jnp.max(jnp.where(excess, 1.0, 0.0)) > 0.0, find_last,
                        lambda: jnp.full((1, TQ), seq, I32))

    last = lax.cond(jnp.max(not_settled(cnt_thr)) > 0.0, resolve_ties, lambda: jnp.full((1, TQ), seq, I32))
    last = jnp.where(real, last, -1)
    thr_m = jnp.where(real, thr, -jnp.inf)

    aq = aq_ref[...] * ((ATT_DH ** -0.5) * LOG2E)
    qgt = []
    for g in range(ATT_KV_HEADS):
        blk = [aq[:, (g * ATT_GROUP + j) * ATT_DH:(g * ATT_GROUP + j + 1) * ATT_DH].T for j in range(ATT_GROUP)]
        qgt.append(jnp.concatenate(blk, axis=1).astype(BF16))

    m_ref[...] = jnp.full(m_ref.shape, NEG_BIG, F32)
    acc_ref[...] = jnp.zeros(acc_ref.shape, F32)

    def att_body(kt, carry):
        k0 = pl.multiple_of(kt * TK, TK)
        sc = sc_ref[pl.ds(k0, TK), :]
        sel = jnp.logical_or(sc > thr_m, jnp.logical_and(sc == thr_m, k0 + krow <= last))
        bias = jnp.where(sel, 0.0, NEG_BIG)
        bias = jnp.concatenate([bias] * ATT_GROUP, axis=1)
        heads = range(ATT_KV_HEADS)
        ss = [jnp.dot(k_ref[pl.ds(k0, TK), g * ATT_DH:(g + 1) * ATT_DH], qgt[g],
                      preferred_element_type=F32) + bias for g in heads]
        ps, alphas = [], []
        for g in heads:
            m_old = m_ref[g]
            m_new = jnp.maximum(m_old, jnp.max(ss[g], axis=0, keepdims=True))
            m_ref[g] = m_new
            alphas.append(jnp.exp2(m_old - m_new))
            ps.append(jnp.exp2(ss[g] - m_new).astype(BF16))
        pvs = [jnp.dot(vt_ref[g, :, pl.ds(k0, TK)], ps[g], preferred_element_type=F32) for g in heads]
        for g in heads:
            acc_ref[g] = alphas[g] * acc_ref[g] + pvs[g]
        return carry

    lax.fori_loop(0, nkt, att_body, 0)

    gout = gout_ref[...]
    for g in range(ATT_KV_HEADS):
        a = acc_ref[g]
        o = a[:ATT_DH] / a[ATT_DH:ATT_DH + 1]
        on = o * lax.rsqrt(jnp.mean(o * o, axis=0, keepdims=True) + EPS)
        for j in range(ATT_GROUP):
            hsl = slice((g * ATT_GROUP + j) * ATT_DH, (g * ATT_GROUP + j + 1) * ATT_DH)
            o_ref[:, hsl] = (on[:, j * TQ:(j + 1) * TQ].T * gout[:, hsl]).astype(o_ref.dtype)


def _dsa(proj, kb, vt, kxb, g_out, bsz, seq):
    n = proj.shape[0]
    nqb = seq // TQ
    ksel = min(TOPK_MAX, seq // 4)
    assert TK % ksel == 0 and seq % TK == 0
    kb3 = kb.reshape(bsz, seq, 256)
    kx3 = kxb.reshape(bsz, seq, IDX_DIM)
    gq = ATT_GROUP * TQ
    return pl.pallas_call(
        functools.partial(_dsa_kernel, ksel=ksel, seq=seq),
        out_shape=jax.ShapeDtypeStruct((n, ATT_HEADS * ATT_DH), BF16),
        grid=(bsz, nqb),
        in_specs=[pl.BlockSpec((TQ, 512), lambda b, q: (b * nqb + q, OFF_IQ // 512)),
                  pl.BlockSpec((TQ, LANES), lambda b, q: (b * nqb + q, OFF_IK // LANES)),
                  pl.BlockSpec((TQ, 1024), lambda b, q: (b * nqb + q, OFF_AQ // 1024)),
                  pl.BlockSpec((None, seq, IDX_DIM), lambda b, q: (b, 0, 0)),
                  pl.BlockSpec((None, seq, 256), lambda b, q: (b, 0, 0)),
                  pl.BlockSpec((ATT_KV_HEADS, VT_ROWS, seq), lambda b, q: (0, 0, b)),
                  pl.BlockSpec((1, ATT_HEADS * ATT_DH), lambda b, q: (0, 0))],
        out_specs=pl.BlockSpec((TQ, ATT_HEADS * ATT_DH), lambda b, q: (b * nqb + q, 0)),
        scratch_shapes=[pltpu.VMEM((seq, TQ), F32),
                        pltpu.VMEM((ATT_KV_HEADS, 1, gq), F32),
                        pltpu.VMEM((ATT_KV_HEADS, VT_ROWS, gq), F32)],
        compiler_params=pltpu.CompilerParams(dimension_semantics=("arbitrary", "arbitrary"),
                                             vmem_limit_bytes=VMEM_LIMIT),
        name="dsa",
    )(proj, proj, proj, kx3, kb3, vt, g_out.reshape(1, -1))


OUT_RB = 128


def _outproj_kernel(rec_ref, att_ref, x_ref, gt_ref, sc_ref, sh_ref, g_ref, wo_ref, wrh_ref, wrl_ref, br_ref,
                    x1_ref, h2_ref, oh0_ref, oh1_ref, wts_ref):
    wo = wo_ref[...]
    half = rec_ref.shape[1]
    tm = x_ref.shape[0]
    blocks = [slice(r, r + OUT_RB) for r in range(0, tm, OUT_RB)]
    mixed = [jnp.dot(rec_ref[rs, :], wo[:half], preferred_element_type=F32)
             + jnp.dot(att_ref[rs, :], wo[half:], preferred_element_type=F32) for rs in blocks]
    his, los = [], []
    for rs, mx in zip(blocks, mixed):
        x1 = x_ref[rs, :] + gt_ref[0] * mx
        x1_ref[rs, :] = x1
        h2 = _norm_mod(x1, g_ref[...], sc_ref[0], sh_ref[0])
        h2_ref[rs, :] = h2
        hi = h2.astype(BF16)
        his.append(hi)
        los.append((h2 - hi.astype(F32)).astype(BF16))
    wrh, wrl = wrh_ref[...], wrl_ref[...]
    logits = jnp.concatenate(
        [jnp.dot(hi, wrh, preferred_element_type=F32) + jnp.dot(lo, wrh, preferred_element_type=F32)
         + jnp.dot(hi, wrl, preferred_element_type=F32) for hi, lo in zip(his, los)], axis=0) + br_ref[...]
    lane = lax.broadcasted_iota(I32, (tm, LANES), 1)
    big = jnp.int32(LANES)

    def argmax_first(vals, mask):
        mv = jnp.where(mask, vals, -jnp.inf)
        top = jnp.max(mv, axis=1, keepdims=True)
        idx = jnp.min(jnp.where(jnp.logical_and(mask, mv == top), lane, big), axis=1, keepdims=True)
        return top, idx

    gmask = lane < N_GROUPS
    gtop, gidx = argmax_first(logits, gmask)
    p_g = 1.0 / jnp.sum(jnp.where(gmask, jnp.exp(logits - gtop), 0.0), axis=1, keepdims=True)
    e_lo = N_GROUPS + gidx * EXPERTS_PER_GROUP
    emask = jnp.logical_and(lane >= e_lo, lane < e_lo + EXPERTS_PER_GROUP)
    v1, i1 = argmax_first(logits, emask)
    v2, i2 = argmax_first(logits, jnp.logical_and(emask, lane != i1))
    r = jnp.exp(v2 - v1)
    w1 = p_g / (1.0 + r)
    w2 = p_g * r / (1.0 + r)
    oh0_ref[...] = jnp.where(lane + N_GROUPS == i1, 1.0, 0.0).astype(BF16)
    oh1_ref[...] = jnp.where(lane + N_GROUPS == i2, 1.0, 0.0).astype(BF16)
    wts_ref[...] = jnp.where(lane == 0, w1, jnp.where(lane == 1, w2, 0.0))


def _outproj(rec, att, x2, gt, sc, sh, g, wo_bf, wr, br, seq):
    n, d = x2.shape
    wr_hi = wr.astype(BF16)
    wr_lo = (wr - wr_hi.astype(F32)).astype(BF16)
    tm = min(512, seq)
    per_b = seq // tm
    half = rec.shape[1]
    bspec = pl.BlockSpec((1, 1, d), lambda i: (i // per_b, 0, 0))
    return pl.pallas_call(
        _outproj_kernel,
        out_shape=(jax.ShapeDtypeStruct((n, d), F32),
                   jax.ShapeDtypeStruct((n, d), F32),
                   jax.ShapeDtypeStruct((n, LANES), BF16),
                   jax.ShapeDtypeStruct((n, LANES), BF16),
                   jax.ShapeDtypeStruct((n, LANES), F32)),
        grid=(n // tm,),
        in_specs=[pl.BlockSpec((tm, half), lambda i: (i, 0)),
                  pl.BlockSpec((tm, half), lambda i: (i, 0)),
                  pl.BlockSpec((tm, d), lambda i: (i, 0)),
                  bspec, bspec, bspec,
                  pl.BlockSpec((1, d), lambda i: (0, 0)),
                  pl.BlockSpec((2 * half, d), lambda i: (0, 0)),
                  pl.BlockSpec((d, LANES), lambda i: (0, 0)),
                  pl.BlockSpec((d, LANES), lambda i: (0, 0)),
                  pl.BlockSpec((1, LANES), lambda i: (0, 0))],
        out_specs=(pl.BlockSpec((tm, d), lambda i: (i, 0)),
                   pl.BlockSpec((tm, d), lambda i: (i, 0)),
                   pl.BlockSpec((tm, LANES), lambda i: (i, 0)),
                   pl.BlockSpec((tm, LANES), lambda i: (i, 0)),
                   pl.BlockSpec((tm, LANES), lambda i: (i, 0))),
        compiler_params=pltpu.CompilerParams(dimension_semantics=("arbitrary",),
                                             vmem_limit_bytes=VMEM_LIMIT),
        name="outproj",
    )(rec, att, x2, gt, sc, sh, g, wo_bf, wr_hi, wr_lo, br)


MOE_TM = 256


def _moe_kernel(te_ref, nt_ref, hs_ref, wg_ref, wu_ref, wd_ref, o_ref):
    t = pl.program_id(0)

    @pl.when(t < nt_ref[0])
    def _():
        xs = hs_ref[...].astype(BF16)
        gte = jnp.dot(xs, wg_ref[...].astype(BF16), preferred_element_type=F32)
        up = jnp.dot(xs, wu_ref[...].astype(BF16), preferred_element_type=F32)
        act = (_silu(gte) * up).astype(BF16)
        o_ref[...] = jnp.dot(act, wd_ref[...].astype(BF16), preferred_element_type=F32)

    @pl.when(t >= nt_ref[0])
    def _():
        o_ref[...] = jnp.zeros_like(o_ref)


def _moe(tile_expert, n_tiles, hs, wg, wu, wd):
    p = hs.shape[0]
    d, de = wg.shape[1], wg.shape[2]
    tm = MOE_TM
    grid_spec = pltpu.PrefetchScalarGridSpec(
        num_scalar_prefetch=2,
        grid=(p // tm,),
        in_specs=[pl.BlockSpec((tm, d), lambda t, te, nt: (jnp.minimum(t, nt[0] - 1), 0)),
                  pl.BlockSpec((None, d, de), lambda t, te, nt: (te[t], 0, 0)),
                  pl.BlockSpec((None, d, de), lambda t, te, nt: (te[t], 0, 0)),
                  pl.BlockSpec((None, de, d), lambda t, te, nt: (te[t], 0, 0))],
        out_specs=pl.BlockSpec((tm, d), lambda t, te, nt: (t, 0)),
    )
    return pl.pallas_call(
        _moe_kernel,
        out_shape=jax.ShapeDtypeStruct((p, d), F32),
        grid_spec=grid_spec,
        compiler_params=pltpu.CompilerParams(dimension_semantics=("arbitrary",),
                                             vmem_limit_bytes=56 * 1024 * 1024),
        name="moe",
    )(tile_expert, n_tiles, hs, wg, wu, wd)


FIN_TM = 256
DMA_UNROLL = 8


def _final_kernel(dcur_ref, dnext_ref, x1_ref, wts_ref, gt_ref, g_ref, ys_ref, o_ref, buf, sem):
    i = pl.program_id(0)
    n = pl.num_programs(0)
    tm = x1_ref.shape[0]

    def row_copy(dref, slot, r, s):
        return pltpu.make_async_copy(ys_ref.at[pl.ds(dref[0, 0, 2 * r + s], 1)],
                                     buf.at[slot, s, pl.ds(r, 1)], sem.at[slot])

    def issue(dref, slot):
        def body(r, c):
            row_copy(dref, slot, r, 0).start()
            row_copy(dref, slot, r, 1).start()
            return c
        lax.fori_loop(0, tm, body, 0, unroll=DMA_UNROLL)

    @pl.when(i == 0)
    def _():
        issue(dcur_ref, 0)

    @pl.when(i + 1 < n)
    def _():
        issue(dnext_ref, (i + 1) % 2)

    slot = i % 2
    for s in range(2):
        pltpu.make_async_copy(ys_ref.at[pl.ds(0, tm)], buf.at[slot, s], sem.at[slot]).wait()
    w = wts_ref[...]
    y = w[:, 0:1] * buf[slot, 0] + w[:, 1:2] * buf[slot, 1]
    xo = x1_ref[...] + gt_ref[0] * y
    o_ref[...] = xo * lax.rsqrt(jnp.mean(xo * xo, axis=-1, keepdims=True) + EPS) * g_ref[...]


def _final(dest3, x1, wts, gt, g, ys, seq):
    n, d = x1.shape
    tm = FIN_TM
    per_b = seq // tm
    steps = n // tm
    row = pl.BlockSpec((tm, d), lambda i: (i, 0))
    smem = lambda f: pl.BlockSpec((1, 1, 2 * tm), f, memory_space=pltpu.SMEM)
    return pl.pallas_call(
        _final_kernel,
        out_shape=jax.ShapeDtypeStruct((n, d), F32),
        grid=(steps,),
        in_specs=[smem(lambda i: (i, 0, 0)),
                  smem(lambda i: (jnp.minimum(i + 1, steps - 1), 0, 0)),
                  row,
                  pl.BlockSpec((tm, LANES), lambda i: (i, 0)),
                  pl.BlockSpec((1, 1, d), lambda i: (i // per_b, 0, 0)),
                  pl.BlockSpec((1, d), lambda i: (0, 0)),
                  pl.BlockSpec(memory_space=pl.ANY)],
        out_specs=row,
        scratch_shapes=[pltpu.VMEM((2, 2, tm, d), F32), pltpu.SemaphoreType.DMA((2,))],
        compiler_params=pltpu.CompilerParams(dimension_semantics=("arbitrary",),
                                             vmem_limit_bytes=VMEM_LIMIT),
        name="final",
    )(dest3, dest3, x1, wts, gt, g, ys)


ROUTE_T = 512


def _route_kernel(oh0_ref, oh1_ref, dest_ref, cnt_ref, run_ref, tot_ref):
    ph = pl.program_id(0)
    i = pl.program_id(1)
    a0 = oh0_ref[...]
    a1 = oh1_ref[...]
    both = a0 + a1
    colsum = jnp.sum(both.astype(F32), axis=0, keepdims=True)

    @pl.when(jnp.logical_and(ph == 0, i == 0))
    def _():
        tot_ref[...] = jnp.zeros_like(tot_ref)

    @pl.when(ph == 0)
    def _():
        tot_ref[...] = tot_ref[...] + colsum

    @pl.when(ph == 1)
    def _():
        @pl.when(i == 0)
        def _():
            run_ref[...] = jnp.zeros_like(run_ref)

        tot = tot_ref[...]
        tiles = jnp.ceil(tot * (1.0 / MOE_TM))
        rr = lax.broadcasted_iota(I32, (LANES, LANES), 0)
        cc = lax.broadcasted_iota(I32, (LANES, LANES), 1)
        before = (rr < cc).astype(BF16)
        tiles8 = jnp.broadcast_to(tiles, (8, LANES)).astype(BF16)
        poff = jnp.dot(tiles8, before, preferred_element_type=F32)[0:1] * float(MOE_TM)
        t = a0.shape[0]
        r2 = lax.broadcasted_iota(I32, (t, t), 0)
        c2 = lax.broadcasted_iota(I32, (t, t), 1)
        earlier = (c2 < r2).astype(BF16)
        rank = jnp.dot(earlier, both, preferred_element_type=F32)
        tgt = poff + run_ref[...] + rank
        d0 = jnp.sum(a0.astype(F32) * tgt, axis=1, keepdims=True)
        d1 = jnp.sum(a1.astype(F32) * tgt, axis=1, keepdims=True)
        lane = lax.broadcasted_iota(I32, (t, LANES), 1)
        dest_ref[...] = jnp.where(lane == 0, d0, jnp.where(lane == 1, d1, 0.0)).astype(I32)
        run_ref[...] = run_ref[...] + colsum
        cnt_ref[...] = tot


def _route(oh0, oh1):
    n = oh0.shape[0]
    t = ROUTE_T
    blk = pl.BlockSpec((t, LANES), lambda ph, i: (i, 0))
    return pl.pallas_call(
        _route_kernel,
        out_shape=(jax.ShapeDtypeStruct((n, LANES), I32), jax.ShapeDtypeStruct((1, LANES), F32)),
        grid=(2, n // t),
        in_specs=[blk, blk],
        out_specs=(pl.BlockSpec((t, LANES), lambda ph, i: (i * ph, 0)),
                   pl.BlockSpec((1, LANES), lambda ph, i: (0, 0))),
        scratch_shapes=[pltpu.VMEM((1, LANES), F32), pltpu.VMEM((1, LANES), F32)],
        compiler_params=pltpu.CompilerParams(dimension_semantics=("arbitrary", "arbitrary"),
                                             vmem_limit_bytes=VMEM_LIMIT),
        name="route",
    )(oh0, oh1)


DISP_TM = 1024


def _dispatch_kernel(pad_ref, dest_ref, h_ref, hs_ref, zero_ref, sem, zsem):
    tm = h_ref.shape[0]

    @pl.when(pl.program_id(0) == 0)
    def _():
        zero_ref[...] = jnp.zeros_like(zero_ref)

        def zero_copy(row0):
            return pltpu.make_async_copy(zero_ref, hs_ref.at[pl.ds(pl.multiple_of(row0, MOE_TM), MOE_TM)], zsem)

        def fill(e, c):
            @pl.when(pad_ref[e] >= 0)
            def _():
                zero_copy(pad_ref[e]).start()
            return c

        def drain(e, c):
            @pl.when(pad_ref[e] >= 0)
            def _():
                zero_copy(pad_ref[e]).wait()
            return c

        lax.fori_loop(0, N_EXPERTS, fill, 0)
        lax.fori_loop(0, N_EXPERTS, drain, 0)
        used = pad_ref[N_EXPERTS]
        total = hs_ref.shape[0] // MOE_TM
        lax.fori_loop(used, total, lambda t, c: (zero_copy(t * MOE_TM).start(), c)[1], 0)
        lax.fori_loop(used, total, lambda t, c: (zero_copy(t * MOE_TM).wait(), c)[1], 0)

    def body(r, c):
        for s in range(2):
            pltpu.make_async_copy(h_ref.at[pl.ds(r, 1)], hs_ref.at[pl.ds(dest_ref[0, 0, 2 * r + s], 1)],
                                  sem).start()
        return c

    lax.fori_loop(0, tm, body, 0, unroll=DMA_UNROLL)
    for _ in range(2):
        pltpu.make_async_copy(h_ref, hs_ref.at[pl.ds(0, tm)], sem).wait()


def _dispatch(pad_start, dest3, h2, p_rows):
    n, w = h2.shape
    tm = DISP_TM
    grid_spec = pltpu.PrefetchScalarGridSpec(
        num_scalar_prefetch=1,
        grid=(n // tm,),
        in_specs=[pl.BlockSpec((1, 1, 2 * tm), lambda i, pad: (i, 0, 0), memory_space=pltpu.SMEM),
                  pl.BlockSpec((tm, w), lambda i, pad: (i, 0))],
        out_specs=pl.BlockSpec(memory_space=pl.ANY),
        scratch_shapes=[pltpu.VMEM((MOE_TM, w), h2.dtype), pltpu.SemaphoreType.DMA(()),
                        pltpu.SemaphoreType.DMA(())],
    )
    return pl.pallas_call(
        _dispatch_kernel,
        out_shape=jax.ShapeDtypeStruct((p_rows, w), h2.dtype),
        grid_spec=grid_spec,
        compiler_params=pltpu.CompilerParams(dimension_semantics=("arbitrary",),
                                             vmem_limit_bytes=VMEM_LIMIT),
        name="dispatch",
    )(pad_start, dest3, h2)


def _tile_tables(counts, n_tiles_max):
    cnt = counts[0, :N_EXPERTS].astype(I32)
    tiles = (cnt + MOE_TM - 1) // MOE_TM
    tiles_end = jnp.cumsum(tiles)
    tile = jnp.arange(n_tiles_max, dtype=I32)
    tile_expert = jnp.sum((tile[:, None] >= tiles_end[None, :]).astype(I32), axis=1)
    n_used = tiles_end[-1:].astype(I32)
    pad_start = jnp.where(tiles > 0, (tiles_end - 1) * MOE_TM, -1).astype(I32)
    return jnp.minimum(tile_expert, N_EXPERTS - 1), n_used, jnp.concatenate([pad_start, n_used])


def kernel(x, c, w_ada, b_ada, g_norm_mix, w_in, lb_logits, g_rec_out, g_att_out, w_out, g_norm_ffn,
           w_router_group, b_router_group, w_router_expert, b_router_expert,
           w_expert_gate, w_expert_up, w_expert_down, g_final):
    bsz, seq, d = x.shape
    n = bsz * seq
    assert w_ada.shape[0] == 1, "single trunk layer"
    layer = 0
    x2 = x.reshape(n, d)

    mod = _adaln(c, w_ada[layer], b_ada[layer])
    sh1, sc1, gt1, sh2, sc2, gt2 = [m.reshape(bsz, 1, d) for m in jnp.split(mod, 6, axis=-1)]

    w_in_bf = jnp.pad(w_in[layer], ((0, 0), (0, IN_PAD - IN_COLS))).astype(BF16)
    proj = _inproj(x2, sc1, sh1, g_norm_mix[layer].reshape(1, d), w_in_bf, seq)
    kb, vb, kxb = _kvprep(proj)
    rec = _hgrn(proj, lb_logits, g_rec_out[layer], bsz, seq, layer)
    att = _dsa(proj, kb, vb, kxb, g_att_out[layer], bsz, seq)

    wr = jnp.concatenate([w_router_group[layer], w_router_expert[layer]], axis=1)
    wr = jnp.pad(wr, ((0, 0), (0, LANES - wr.shape[1])))
    br = jnp.concatenate([b_router_group[layer], b_router_expert[layer]])
    br = jnp.pad(br, (0, LANES - br.shape[0])).reshape(1, LANES)
    x1, h2, oh0, oh1, wts = _outproj(rec, att, x2, gt1, sc2, sh2, g_norm_ffn[layer].reshape(1, d),
                                      w_out[layer].astype(BF16), wr, br, seq)

    dest, counts = _route(oh0, oh1)
    dest2 = dest[:, :2]
    p_rows = 2 * n + N_EXPERTS * MOE_TM
    tile_expert, n_tiles, pad_start = _tile_tables(counts, p_rows // MOE_TM)
    hs = _dispatch(pad_start, dest2.reshape(n // DISP_TM, 1, 2 * DISP_TM), h2, p_rows)
    ys = _moe(tile_expert, n_tiles, hs, w_expert_gate[layer], w_expert_up[layer], w_expert_down[layer])
    out = _final(dest2.reshape(n // FIN_TM, 1, 2 * FIN_TM), x1, wts, gt2, g_final.reshape(1, d), ys, seq)
    return out.reshape(bsz, seq, d)
```

```python
import functools

import jax
import jax.numpy as jnp
import numpy as np
from jax import lax
from jax.experimental import pallas as pl
from jax.experimental.pallas import tpu as pltpu

F32 = jnp.float32
BF16 = jnp.bfloat16
I32 = jnp.int32

EPS = 1e-6
LANES = 128

REC_HEADS = 8
REC_D = 128
REC_CHUNK = 64
REC_SUB = 16
ATT_HEADS = 8
ATT_DH = 128
ATT_KV_HEADS = 2
ATT_GROUP = ATT_HEADS // ATT_KV_HEADS
IDX_HEADS = 8
IDX_DIM = 64
TOPK_MAX = 256
N_GROUPS = 4
EXPERTS_PER_GROUP = 8
N_EXPERTS = N_GROUPS * EXPERTS_PER_GROUP
D_EXPERT = 512

OFF_RQ = 0
OFF_RF = 1024
OFF_RI = 2048
OFF_RG = 3072
OFF_AQ = 4096
OFF_AK = 5120
OFF_AV = 5376
OFF_IQ = 5632
OFF_IK = 6144
OFF_IW = 6208
IN_COLS = 6216
IN_PAD = 6272

VMEM_LIMIT = 48 * 1024 * 1024

INT_MIN = -(2 ** 31)
KEY_NEGINF = int(np.array(-np.inf, np.float32).view(np.int32)) ^ 0x7FFFFFFF
NEG_BIG = -1e30


def _silu(v):
    return v * jax.nn.sigmoid(v)


def _nt_dot(a, b):
    return lax.dot_general(a, b, (((1,), (1,)), ((), ())), preferred_element_type=F32)


def _tn_dot(a, b):
    return lax.dot_general(a, b, (((0,), (0,)), ((), ())), preferred_element_type=F32)


def _adaln_kernel(c_ref, w_ref, b_ref, o_ref):
    ca = _silu(c_ref[...])
    o_ref[...] = jnp.dot(ca, w_ref[...], preferred_element_type=F32,
                         precision=lax.Precision.HIGHEST) + b_ref[...]


def _adaln(c, w, b):
    bsz, d = c.shape
    n = w.shape[1]
    tn = 512
    return pl.pallas_call(
        _adaln_kernel,
        out_shape=jax.ShapeDtypeStruct((bsz, n), F32),
        grid=(n // tn,),
        in_specs=[pl.BlockSpec((bsz, d), lambda j: (0, 0)),
                  pl.BlockSpec((d, tn), lambda j: (0, j)),
                  pl.BlockSpec((1, tn), lambda j: (0, j))],
        out_specs=pl.BlockSpec((bsz, tn), lambda j: (0, j)),
        compiler_params=pltpu.CompilerParams(dimension_semantics=("arbitrary",),
                                             vmem_limit_bytes=VMEM_LIMIT),
        name="adaln",
    )(c, w, b.reshape(1, n))


def _norm_mod(x, g, sc, sh):
    xn = x * lax.rsqrt(jnp.mean(x * x, axis=-1, keepdims=True) + EPS)
    return xn * g * (1.0 + sc) + sh


def _inproj_kernel(x_ref, sc_ref, sh_ref, g_ref, w_ref, o_ref, h_ref):
    @pl.when(pl.program_id(1) == 0)
    def _():
        h_ref[...] = _norm_mod(x_ref[...], g_ref[...], sc_ref[0], sh_ref[0]).astype(BF16)

    o_ref[...] = jnp.dot(h_ref[...], w_ref[...], preferred_element_type=F32)


def _inproj(x2, sc, sh, g, w_bf, seq):
    n, d = x2.shape
    ncol = w_bf.shape[1]
    tm = min(1024, seq)
    tn = 896
    per_b = seq // tm
    return pl.pallas_call(
        _inproj_kernel,
        out_shape=jax.ShapeDtypeStruct((n, ncol), F32),
        grid=(n // tm, ncol // tn),
        in_specs=[pl.BlockSpec((tm, d), lambda i, j: (i, 0)),
                  pl.BlockSpec((1, 1, d), lambda i, j: (i // per_b, 0, 0)),
                  pl.BlockSpec((1, 1, d), lambda i, j: (i // per_b, 0, 0)),
                  pl.BlockSpec((1, d), lambda i, j: (0, 0)),
                  pl.BlockSpec((d, tn), lambda i, j: (0, j))],
        out_specs=pl.BlockSpec((tm, tn), lambda i, j: (i, j)),
        scratch_shapes=[pltpu.VMEM((tm, d), BF16)],
        compiler_params=pltpu.CompilerParams(dimension_semantics=("arbitrary", "arbitrary"),
                                             vmem_limit_bytes=VMEM_LIMIT),
        name="inproj",
    )(x2, sc, sh, g, w_bf)


VT_ROWS = ATT_DH + 16


def _kvprep_kernel(kv_ref, ik_ref, k_ref, vt_ref, kx_ref):
    kv = kv_ref[...]
    tm = kv.shape[0]
    k_ref[...] = kv[:, :256].astype(BF16)
    tail = jnp.where(lax.broadcasted_iota(I32, (16, tm), 0) == 0, 1.0, 0.0)
    for g in range(ATT_KV_HEADS):
        vt = kv[:, 256 + g * ATT_DH:256 + (g + 1) * ATT_DH].T
        vt_ref[g] = jnp.concatenate([vt, tail], axis=0).astype(BF16)
    kx_ref[...] = ik_ref[...][:, :IDX_DIM].astype(BF16)


def _kvprep(proj):
    n = proj.shape[0]
    tm = 512
    return pl.pallas_call(
        _kvprep_kernel,
        out_shape=(jax.ShapeDtypeStruct((n, 256), BF16),
                   jax.ShapeDtypeStruct((ATT_KV_HEADS, VT_ROWS, n), BF16),
                   jax.ShapeDtypeStruct((n, IDX_DIM), BF16)),
        grid=(n // tm,),
        in_specs=[pl.BlockSpec((tm, 512), lambda i: (i, OFF_AK // 512)),
                  pl.BlockSpec((tm, LANES), lambda i: (i, OFF_IK // LANES))],
        out_specs=(pl.BlockSpec((tm, 256), lambda i: (i, 0)),
                   pl.BlockSpec((ATT_KV_HEADS, VT_ROWS, tm), lambda i: (0, 0, i)),
                   pl.BlockSpec((tm, IDX_DIM), lambda i: (i, 0))),
        compiler_params=pltpu.CompilerParams(dimension_semantics=("arbitrary",),
                                             vmem_limit_bytes=VMEM_LIMIT),
        name="kvprep",
    )(proj, proj)


def _hgrn_kernel(q_ref, f_ref, i_ref, g_ref, lbl_ref, gout_ref, o_ref, st_ref, *, chunks, layer):
    @pl.when(pl.program_id(2) == 0)
    def _():
        st_ref[...] = jnp.zeros_like(st_ref)

    lbl = lbl_ref[...]
    e = jnp.exp(lbl - jnp.max(lbl, axis=0, keepdims=True))
    sm = e / jnp.sum(e, axis=0, keepdims=True)
    lb = jnp.sum(sm[: layer + 1], axis=0, keepdims=True)
    gout = gout_ref[...]

    c = REC_CHUNK
    nsub = c // REC_SUB
    rr = lax.broadcasted_iota(I32, (c, c), 0)
    cc = lax.broadcasted_iota(I32, (c, c), 1)
    row = lax.broadcasted_iota(I32, (c, REC_D), 0)
    sub = lax.shift_right_logical(row, REC_SUB.bit_length() - 1)

    f = lb + (1.0 - lb) * jax.nn.sigmoid(f_ref[...])
    logf = jnp.log2(f)
    k = 1.0 - f
    qf = _silu(q_ref[...]) * (REC_D ** -0.5)
    vb = i_ref[...].astype(BF16)

    halves = [1 << l for l in range(REC_SUB.bit_length() - 1)]
    logf_w = jnp.concatenate([logf[ci * c:(ci + 1) * c] for ci in range(chunks)], axis=1)
    b_all = jnp.dot((rr >= cc).astype(F32), logf_w, preferred_element_type=F32,
                    precision=lax.Precision.HIGHEST)
    row_w = lax.broadcasted_iota(I32, b_all.shape, 0)
    blk_end, split = b_all, {}
    for h in halves:
        first = (row_w & (2 * h - 1)) < h
        split[h] = jnp.where(first, blk_end, pltpu.roll(blk_end, h, 0))
        blk_end = jnp.where(first, pltpu.roll(blk_end, c - h, 0), blk_end)
    lvl_mask = [jnp.logical_and((rr ^ cc) < 2 * h, jnp.logical_and((rr & h) != 0, (cc & h) == 0)) for h in halves]
    eye = rr == cc
    in_second = {h: (row & h) != 0 for h in halves}
    after_sub = [sub > i for i in range(nsub - 1)]
    in_sub = [sub == i for i in range(nsub - 1)]

    qxs, kxs, qls, kls, diag, upd, q_in, decay = [], [], [], [], [], [], [], []
    for ci in range(chunks):
        sl = slice(ci * c, (ci + 1) * c)
        cols = slice(ci * REC_D, (ci + 1) * REC_D)
        b = b_all[:, cols]
        kc, qc = k[sl], qf[sl]
        qparts, kparts = [], []
        for i in range(nsub - 1):
            r = b[(i + 1) * REC_SUB - 1:(i + 1) * REC_SUB, :]
            qparts.append(jnp.where(after_sub[i], qc * jnp.exp2(b - r), 0.0))
            kparts.append(jnp.where(in_sub[i], kc * jnp.exp2(r - b), 0.0))
        qxs.append(jnp.concatenate(qparts, axis=1).astype(BF16))
        kxs.append(jnp.concatenate(kparts, axis=1).astype(BF16))
        for h in halves:
            e = jnp.exp2(-jnp.abs(b - split[h][:, cols]))
            qls.append(jnp.where(in_second[h], qc * e, 0.0).astype(BF16))
            kls.append(jnp.where(in_second[h], 0.0, kc * e).astype(BF16))
        diag.append(jnp.sum(qc * kc, axis=1, keepdims=True))
        b_end = b[c - 1:c, :]
        upd.append((kc * jnp.exp2(b_end - b)).astype(BF16))
        q_in.append((qc * jnp.exp2(b)).astype(BF16))
        decay.append(jnp.exp2(b_end))
    nl = len(halves)
    cross = [_nt_dot(qxs[ci], kxs[ci]) for ci in range(chunks)]
    within = [[_nt_dot(qls[ci * nl + l], kls[ci * nl + l]) for l in range(nl)] for ci in range(chunks)]
    scores = []
    for ci in range(chunks):
        s = cross[ci] + jnp.where(eye, diag[ci], 0.0)
        for l in range(nl):
            s = s + jnp.where(lvl_mask[l], within[ci][l], 0.0)
        scores.append(s.astype(BF16))
    upd = [_tn_dot(vb[ci * c:(ci + 1) * c], upd[ci]) for ci in range(chunks)]
    intra = [jnp.dot(scores[ci], vb[ci * c:(ci + 1) * c], preferred_element_type=F32) for ci in range(chunks)]

    st = st_ref[...]
    outs = []
    for ci in range(chunks):
        outs.append(_nt_dot(q_in[ci], st.astype(BF16)) + intra[ci])
        st = st * decay[ci] + upd[ci]
    st_ref[...] = st

    o = jnp.concatenate(outs, axis=0)
    on = o * lax.rsqrt(jnp.mean(o * o, axis=-1, keepdims=True) + EPS)
    o_ref[...] = (on * gout * _silu(g_ref[...])).astype(o_ref.dtype)


def _hgrn(proj, lb_logits, g_out, bsz, seq, layer):
    n = proj.shape[0]
    tc = min(512, seq)
    per_b = seq // tc
    nl = lb_logits.shape[0]

    def col(off):
        return lambda b, h, c: (b * per_b + c, off // REC_D + h)

    return pl.pallas_call(
        functools.partial(_hgrn_kernel, chunks=tc // REC_CHUNK, layer=layer),
        out_shape=jax.ShapeDtypeStruct((n, REC_HEADS * REC_D), BF16),
        grid=(bsz, REC_HEADS, per_b),
        in_specs=[pl.BlockSpec((tc, REC_D), col(OFF_RQ)),
                  pl.BlockSpec((tc, REC_D), col(OFF_RF)),
                  pl.BlockSpec((tc, REC_D), col(OFF_RI)),
                  pl.BlockSpec((tc, REC_D), col(OFF_RG)),
                  pl.BlockSpec((nl, REC_D), lambda b, h, c: (0, h)),
                  pl.BlockSpec((1, REC_D), lambda b, h, c: (0, h))],
        out_specs=pl.BlockSpec((tc, REC_D), lambda b, h, c: (b * per_b + c, h)),
        scratch_shapes=[pltpu.VMEM((REC_D, REC_D), F32)],
        compiler_params=pltpu.CompilerParams(
            dimension_semantics=("arbitrary", "arbitrary", "arbitrary"),
            vmem_limit_bytes=VMEM_LIMIT),
        name="hgrn2",
    )(proj, proj, proj, proj, lb_logits, g_out.reshape(1, -1))


TQ = 128
TK = 1024
LOG2E = 1.4426950408889634
CNT_ROWS = 64
CNT_BLOCK = 256
SETTLE_EVERY = 4
VALUE_SPLITS = 4


def _dsa_kernel(qi_ref, iw_ref, aq_ref, kx_ref, k_ref, vt_ref, gout_ref, o_ref,
                sc_ref, m_ref, acc_ref, *, ksel, seq):
    t0 = pl.program_id(1) * TQ
    nkt = (t0 + TQ + TK - 1) // TK
    qpos = t0 + lax.broadcasted_iota(I32, (TK, TQ), 1)
    krow = lax.broadcasted_iota(I32, (TK, TQ), 0)

    qit = (qi_ref[...] * (IDX_DIM ** -0.5)).T
    qht = jnp.concatenate([qit[h * IDX_DIM:(h + 1) * IDX_DIM] for h in range(IDX_HEADS)],
                          axis=1).astype(BF16)
    iwt = iw_ref[...].T
    wrow = [iwt[IDX_DIM + h:IDX_DIM + h + 1] * (IDX_HEADS ** -0.5) for h in range(IDX_HEADS)]

    def score_body(kt, carry):
        k0 = pl.multiple_of(kt * TK, TK)
        rel = jnp.dot(kx_ref[pl.ds(k0, TK), :], qht, preferred_element_type=F32)
        sc = jnp.zeros((TK, TQ), F32)
        for h in range(IDX_HEADS):
            sc = sc + wrow[h] * jnp.maximum(rel[:, h * TQ:(h + 1) * TQ], 0.0)
        sc_ref[pl.ds(k0, TK), :] = jnp.where(k0 + krow <= qpos, sc, -jnp.inf)
        return carry

    lax.fori_loop(0, nkt, score_body, 0)

    def u_to_float(u):
        key = u ^ INT_MIN
        return lax.bitcast_convert_type(key ^ ((key >> 31) & 0x7FFFFFFF), F32)

    def float_to_u(f):
        bits = lax.bitcast_convert_type(f, I32)
        return bits ^ ((bits >> 31) & 0x7FFFFFFF) ^ INT_MIN

    def count(pred):
        def body(kt, acc):
            for u in range(0, TK, CNT_BLOCK):
                r0 = pl.multiple_of(kt * TK + u, CNT_BLOCK)
                hit = pred(sc_ref[pl.ds(r0, CNT_BLOCK), :], r0 + krow[:CNT_BLOCK])
                acc = acc + jnp.sum(jnp.where(hit, 1.0, 0.0).reshape(CNT_BLOCK // CNT_ROWS, CNT_ROWS, TQ), axis=0)
            return acc
        acc = lax.fori_loop(0, nkt, body, jnp.zeros((CNT_ROWS, TQ), F32))
        return jnp.sum(acc, axis=0, keepdims=True)

    def group_max(j, gm):
        r0 = pl.multiple_of(j * ksel, ksel)
        return jnp.maximum(gm, sc_ref[pl.ds(r0, ksel), :])

    gm = lax.fori_loop(0, nkt * (TK // ksel), group_max, jnp.full((ksel, TQ), -jnp.inf, F32))
    kf = float(ksel)

    def ult(a, b):
        return (a ^ INT_MIN) < (b ^ INT_MIN)

    def is_open(lo, hi):
        return ult(jnp.int32(1), hi - lo)

    def not_settled(cnt):
        return jnp.where(cnt == kf, 0.0, 1.0)

    def search_cond(st):
        _, lo, hi, cnt = st
        return jnp.max(jnp.where(is_open(lo, hi), not_settled(cnt), 0.0)) > 0.0

    def search_step(_, st):
        it, lo, hi, cnt = st
        mid_u = lo + lax.shift_right_logical(hi - lo, 1)
        mid_v = float_to_u(0.5 * (u_to_float(lo) + u_to_float(hi)))
        use_v = jnp.logical_and(it < VALUE_SPLITS, jnp.logical_and(ult(lo, mid_v), ult(mid_v, hi)))
        cand = jnp.where(is_open(lo, hi), jnp.where(use_v, mid_v, mid_u), lo)
        cand_f = u_to_float(cand)
        c = count(lambda s, pos: s >= cand_f)
        take = c >= kf
        return it + 1, jnp.where(take, cand, lo), jnp.where(take, hi, cand), jnp.where(take, c, cnt)

    def search_body(st):
        return lax.fori_loop(0, SETTLE_EVERY, search_step, st)

    _, u_thr, _, cnt_thr = lax.while_loop(
        search_cond, search_body,
        (jnp.int32(0), float_to_u(jnp.min(gm, axis=0, keepdims=True)),
         float_to_u(jnp.max(gm, axis=0, keepdims=True)) + 1, jnp.full((1, TQ), -1.0, F32)))
    thr = u_to_float(u_thr)
    real = (u_thr ^ INT_MIN) > KEY_NEGINF

    nbits = seq.bit_length()

    def resolve_ties():
        c_gt = count(lambda s, pos: s > thr)
        c_eq = count(lambda s, pos: s == thr)
        need = kf - c_gt
        excess = jnp.logical_and(c_eq > need, real)

        def find_last():
            def jbody(i, y):
                cand = y | lax.shift_left(jnp.int32(1), (nbits - 1) - i)
                below = count(lambda s, pos: jnp.logical_and(s == thr, pos < cand))
                return jnp.where(below <= need - 1.0, cand, y)
            return lax.fori_loop(0, nbits, jbody, jnp.zeros((1, TQ), I32))

        return lax.cond(jnp.max(jnp.where(excess, 1.0, 0.0)) > 0.0, find_last,
                        lambda: jnp.full((1, TQ), seq, I32))

    last = lax.cond(jnp.max(not_settled(cnt_thr)) > 0.0, resolve_ties, lambda: jnp.full((1, TQ), seq, I32))
    last = jnp.where(real, last, -1)
    thr_m = jnp.where(real, thr, -jnp.inf)

    aq = aq_ref[...] * ((ATT_DH ** -0.5) * LOG2E)
    qgt = []
    for g in range(ATT_KV_HEADS):
        blk = [aq[:, (g * ATT_GROUP + j) * ATT_DH:(g * ATT_GROUP + j + 1) * ATT_DH].T for j in range(ATT_GROUP)]
        qgt.append(jnp.concatenate(blk, axis=1).astype(BF16))

    m_ref[...] = jnp.full(m_ref.shape, NEG_BIG, F32)
    acc_ref[...] = jnp.zeros(acc_ref.shape, F32)

    def att_body(kt, carry):
        k0 = pl.multiple_of(kt * TK, TK)
        sc = sc_ref[pl.ds(k0, TK), :]
        sel = jnp.logical_or(sc > thr_m, jnp.logical_and(sc == thr_m, k0 + krow <= last))
        bias = jnp.where(sel, 0.0, NEG_BIG)
        bias = jnp.concatenate([bias] * ATT_GROUP, axis=1)
        heads = range(ATT_KV_HEADS)
        ss = [jnp.dot(k_ref[pl.ds(k0, TK), g * ATT_DH:(g + 1) * ATT_DH], qgt[g],
                      preferred_element_type=F32) + bias for g in heads]
        ps, alphas = [], []
        for g in heads:
            m_old = m_ref[g]
            m_new = jnp.maximum(m_old, jnp.max(ss[g], axis=0, keepdims=True))
            m_ref[g] = m_new
            alphas.append(jnp.exp2(m_old - m_new))
            ps.append(jnp.exp2(ss[g] - m_new).astype(BF16))
        pvs = [jnp.dot(vt_ref[g, :, pl.ds(k0, TK)], ps[g], preferred_element_type=F32) for g in heads]
        for g in heads:
            acc_ref[g] = alphas[g] * acc_ref[g] + pvs[g]
        return carry

    lax.fori_loop(0, nkt, att_body, 0)

    gout = gout_ref[...]
    for g in range(ATT_KV_HEADS):
        a = acc_ref[g]
        o = a[:ATT_DH] / a[ATT_DH:ATT_DH + 1]
        on = o * lax.rsqrt(jnp.mean(o * o, axis=0, keepdims=True) + EPS)
        for j in range(ATT_GROUP):
            hsl = slice((g * ATT_GROUP + j) * ATT_DH, (g * ATT_GROUP + j + 1) * ATT_DH)
            o_ref[:, hsl] = (on[:, j * TQ:(j + 1) * TQ].T * gout[:, hsl]).astype(o_ref.dtype)


def _dsa(proj, kb, vt, kxb, g_out, bsz, seq):
    n = proj.shape[0]
    nqb = seq // TQ
    ksel = min(TOPK_MAX, seq // 4)
    assert TK % ksel == 0 and seq % TK == 0
    kb3 = kb.reshape(bsz, seq, 256)
    kx3 = kxb.reshape(bsz, seq, IDX_DIM)
    gq = ATT_GROUP * TQ
    return pl.pallas_call(
        functools.partial(_dsa_kernel, ksel=ksel, seq=seq),
        out_shape=jax.ShapeDtypeStruct((n, ATT_HEADS * ATT_DH), BF16),
        grid=(bsz, nqb),
        in_specs=[pl.BlockSpec((TQ, 512), lambda b, q: (b * nqb + q, OFF_IQ // 512)),
                  pl.BlockSpec((TQ, LANES), lambda b, q: (b * nqb + q, OFF_IK // LANES)),
                  pl.BlockSpec((TQ, 1024), lambda b, q: (b * nqb + q, OFF_AQ // 1024)),
                  pl.BlockSpec((None, seq, IDX_DIM), lambda b, q: (b, 0, 0)),
                  pl.BlockSpec((None, seq, 256), lambda b, q: (b, 0, 0)),
                  pl.BlockSpec((ATT_KV_HEADS, VT_ROWS, seq), lambda b, q: (0, 0, b)),
                  pl.BlockSpec((1, ATT_HEADS * ATT_DH), lambda b, q: (0, 0))],
        out_specs=pl.BlockSpec((TQ, ATT_HEADS * ATT_DH), lambda b, q: (b * nqb + q, 0)),
        scratch_shapes=[pltpu.VMEM((seq, TQ), F32),
                        pltpu.VMEM((ATT_KV_HEADS, 1, gq), F32),
                        pltpu.VMEM((ATT_KV_HEADS, VT_ROWS, gq), F32)],
        compiler_params=pltpu.CompilerParams(dimension_semantics=("arbitrary", "arbitrary"),
                                             vmem_limit_bytes=VMEM_LIMIT),
        name="dsa",
    )(proj, proj, proj, kx3, kb3, vt, g_out.reshape(1, -1))


OUT_RB = 128


def _outproj_kernel(rec_ref, att_ref, x_ref, gt_ref, sc_ref, sh_ref, g_ref, wo_ref, wrh_ref, wrl_ref, br_ref,
                    x1_ref, h2_ref, oh0_ref, oh1_ref, wts_ref):
    wo = wo_ref[...]
    half = rec_ref.shape[1]
    tm = x_ref.shape[0]
    blocks = [slice(r, r + OUT_RB) for r in range(0, tm, OUT_RB)]
    mixed = [jnp.dot(rec_ref[rs, :], wo[:half], preferred_element_type=F32)
             + jnp.dot(att_ref[rs, :], wo[half:], preferred_element_type=F32) for rs in blocks]
    his, los = [], []
    for rs, mx in zip(blocks, mixed):
        x1 = x_ref[rs, :] + gt_ref[0] * mx
        x1_ref[rs, :] = x1
        h2 = _norm_mod(x1, g_ref[...], sc_ref[0], sh_ref[0])
        h2_ref[rs, :] = h2
        hi = h2.astype(BF16)
        his.append(hi)
        los.append((h2 - hi.astype(F32)).astype(BF16))
    wrh, wrl = wrh_ref[...], wrl_ref[...]
    logits = jnp.concatenate(
        [jnp.dot(hi, wrh, preferred_element_type=F32) + jnp.dot(lo, wrh, preferred_element_type=F32)
         + jnp.dot(hi, wrl, preferred_element_type=F32) for hi, lo in zip(his, los)], axis=0) + br_ref[...]
    lane = lax.broadcasted_iota(I32, (tm, LANES), 1)
    big = jnp.int32(LANES)

    def argmax_first(vals, mask):
        mv = jnp.where(mask, vals, -jnp.inf)
        top = jnp.max(mv, axis=1, keepdims=True)
        idx = jnp.min(jnp.where(jnp.logical_and(mask, mv == top), lane, big), axis=1, keepdims=True)
        return top, idx

    gmask = lane < N_GROUPS
    gtop, gidx = argmax_first(logits, gmask)
    p_g = 1.0 / jnp.sum(jnp.where(gmask, jnp.exp(logits - gtop), 0.0), axis=1, keepdims=True)
    e_lo = N_GROUPS + gidx * EXPERTS_PER_GROUP
    emask = jnp.logical_and(lane >= e_lo, lane < e_lo + EXPERTS_PER_GROUP)
    v1, i1 = argmax_first(logits, emask)
    v2, i2 = argmax_first(logits, jnp.logical_and(emask, lane != i1))
    r = jnp.exp(v2 - v1)
    w1 = p_g / (1.0 + r)
    w2 = p_g * r / (1.0 + r)
    oh0_ref[...] = jnp.where(lane + N_GROUPS == i1, 1.0, 0.0).astype(BF16)
    oh1_ref[...] = jnp.where(lane + N_GROUPS == i2, 1.0, 0.0).astype(BF16)
    wts_ref[...] = jnp.where(lane == 0, w1, jnp.where(lane == 1, w2, 0.0))


def _outproj(rec, att, x2, gt, sc, sh, g, wo_bf, wr, br, seq):
    n, d = x2.shape
    wr_hi = wr.astype(BF16)
    wr_lo = (wr - wr_hi.astype(F32)).astype(BF16)
    tm = min(512, seq)
    per_b = seq // tm
    half = rec.shape[1]
    bspec = pl.BlockSpec((1, 1, d), lambda i: (i // per_b, 0, 0))
    return pl.pallas_call(
        _outproj_kernel,
        out_shape=(jax.ShapeDtypeStruct((n, d), F32),
                   jax.ShapeDtypeStruct((n, d), F32),
                   jax.ShapeDtypeStruct((n, LANES), BF16),
                   jax.ShapeDtypeStruct((n, LANES), BF16),
                   jax.ShapeDtypeStruct((n, LANES), F32)),
        grid=(n // tm,),
        in_specs=[pl.BlockSpec((tm, half), lambda i: (i, 0)),
                  pl.BlockSpec((tm, half), lambda i: (i, 0)),
                  pl.BlockSpec((tm, d), lambda i: (i, 0)),
                  bspec, bspec, bspec,
                  pl.BlockSpec((1, d), lambda i: (0, 0)),
                  pl.BlockSpec((2 * half, d), lambda i: (0, 0)),
                  pl.BlockSpec((d, LANES), lambda i: (0, 0)),
                  pl.BlockSpec((d, LANES), lambda i: (0, 0)),
                  pl.BlockSpec((1, LANES), lambda i: (0, 0))],
        out_specs=(pl.BlockSpec((tm, d), lambda i: (i, 0)),
                   pl.BlockSpec((tm, d), lambda i: (i, 0)),
                   pl.BlockSpec((tm, LANES), lambda i: (i, 0)),
                   pl.BlockSpec((tm, LANES), lambda i: (i, 0)),
                   pl.BlockSpec((tm, LANES), lambda i: (i, 0))),
        compiler_params=pltpu.CompilerParams(dimension_semantics=("arbitrary",),
                                             vmem_limit_bytes=VMEM_LIMIT),
        name="outproj",
    )(rec, att, x2, gt, sc, sh, g, wo_bf, wr_hi, wr_lo, br)


MOE_TM = 256


def _moe_kernel(te_ref, nt_ref, hs_ref, wg_ref, wu_ref, wd_ref, o_ref):
    t = pl.program_id(0)

    @pl.when(t < nt_ref[0])
    def _():
        xs = hs_ref[...].astype(BF16)
        gte = jnp.dot(xs, wg_ref[...].astype(BF16), preferred_element_type=F32)
        up = jnp.dot(xs, wu_ref[...].astype(BF16), preferred_element_type=F32)
        act = (_silu(gte) * up).astype(BF16)
        o_ref[...] = jnp.dot(act, wd_ref[...].astype(BF16), preferred_element_type=F32)

    @pl.when(t >= nt_ref[0])
    def _():
        o_ref[...] = jnp.zeros_like(o_ref)


def _moe(tile_expert, n_tiles, hs, wg, wu, wd):
    p = hs.shape[0]
    d, de = wg.shape[1], wg.shape[2]
    tm = MOE_TM
    grid_spec = pltpu.PrefetchScalarGridSpec(
        num_scalar_prefetch=2,
        grid=(p // tm,),
        in_specs=[pl.BlockSpec((tm, d), lambda t, te, nt: (jnp.minimum(t, nt[0] - 1), 0)),
                  pl.BlockSpec((None, d, de), lambda t, te, nt: (te[t], 0, 0)),
                  pl.BlockSpec((None, d, de), lambda t, te, nt: (te[t], 0, 0)),
                  pl.BlockSpec((None, de, d), lambda t, te, nt: (te[t], 0, 0))],
        out_specs=pl.BlockSpec((tm, d), lambda t, te, nt: (t, 0)),
    )
    return pl.pallas_call(
        _moe_kernel,
        out_shape=jax.ShapeDtypeStruct((p, d), F32),
        grid_spec=grid_spec,
        compiler_params=pltpu.CompilerParams(dimension_semantics=("arbitrary",),
                                             vmem_limit_bytes=56 * 1024 * 1024),
        name="moe",
    )(tile_expert, n_tiles, hs, wg, wu, wd)


FIN_TM = 256
DMA_UNROLL = 8


def _final_kernel(dcur_ref, dnext_ref, x1_ref, wts_ref, gt_ref, g_ref, ys_ref, o_ref, buf, sem):
    i = pl.program_id(0)
    n = pl.num_programs(0)
    tm = x1_ref.shape[0]

    def row_copy(dref, slot, r, s):
        return pltpu.make_async_copy(ys_ref.at[pl.ds(dref[0, 0, 2 * r + s], 1)],
                                     buf.at[slot, s, pl.ds(r, 1)], sem.at[slot])

    def issue(dref, slot):
        def body(r, c):
            row_copy(dref, slot, r, 0).start()
            row_copy(dref, slot, r, 1).start()
            return c
        lax.fori_loop(0, tm, body, 0, unroll=DMA_UNROLL)

    @pl.when(i == 0)
    def _():
        issue(dcur_ref, 0)

    @pl.when(i + 1 < n)
    def _():
        issue(dnext_ref, (i + 1) % 2)

    slot = i % 2
    for s in range(2):
        pltpu.make_async_copy(ys_ref.at[pl.ds(0, tm)], buf.at[slot, s], sem.at[slot]).wait()
    w = wts_ref[...]
    y = w[:, 0:1] * buf[slot, 0] + w[:, 1:2] * buf[slot, 1]
    xo = x1_ref[...] + gt_ref[0] * y
    o_ref[...] = xo * lax.rsqrt(jnp.mean(xo * xo, axis=-1, keepdims=True) + EPS) * g_ref[...]


def _final(dest3, x1, wts, gt, g, ys, seq):
    n, d = x1.shape
    tm = FIN_TM
    per_b = seq // tm
    steps = n // tm
    row = pl.BlockSpec((tm, d), lambda i: (i, 0))
    smem = lambda f: pl.BlockSpec((1, 1, 2 * tm), f, memory_space=pltpu.SMEM)
    return pl.pallas_call(
        _final_kernel,
        out_shape=jax.ShapeDtypeStruct((n, d), F32),
        grid=(steps,),
        in_specs=[smem(lambda i: (i, 0, 0)),
                  smem(lambda i: (jnp.minimum(i + 1, steps - 1), 0, 0)),
                  row,
                  pl.BlockSpec((tm, LANES), lambda i: (i, 0)),
                  pl.BlockSpec((1, 1, d), lambda i: (i // per_b, 0, 0)),
                  pl.BlockSpec((1, d), lambda i: (0, 0)),
                  pl.BlockSpec(memory_space=pl.ANY)],
        out_specs=row,
        scratch_shapes=[pltpu.VMEM((2, 2, tm, d), F32), pltpu.SemaphoreType.DMA((2,))],
        compiler_params=pltpu.CompilerParams(dimension_semantics=("arbitrary",),
                                             vmem_limit_bytes=VMEM_LIMIT),
        name="final",
    )(dest3, dest3, x1, wts, gt, g, ys)


ROUTE_T = 512


def _route_kernel(oh0_ref, oh1_ref, dest_ref, cnt_ref, run_ref, tot_ref):
    ph = pl.program_id(0)
    i = pl.program_id(1)
    a0 = oh0_ref[...]
    a1 = oh1_ref[...]
    both = a0 + a1
    colsum = jnp.sum(both.astype(F32), axis=0, keepdims=True)

    @pl.when(jnp.logical_and(ph == 0, i == 0))
    def _():
        tot_ref[...] = jnp.zeros_like(tot_ref)

    @pl.when(ph == 0)
    def _():
        tot_ref[...] = tot_ref[...] + colsum

    @pl.when(ph == 1)
    def _():
        @pl.when(i == 0)
        def _():
            run_ref[...] = jnp.zeros_like(run_ref)

        tot = tot_ref[...]
        tiles = jnp.ceil(tot * (1.0 / MOE_TM))
        rr = lax.broadcasted_iota(I32, (LANES, LANES), 0)
        cc = lax.broadcasted_iota(I32, (LANES, LANES), 1)
        before = (rr < cc).astype(BF16)
        tiles8 = jnp.broadcast_to(tiles, (8, LANES)).astype(BF16)
        poff = jnp.dot(tiles8, before, preferred_element_type=F32)[0:1] * float(MOE_TM)
        t = a0.shape[0]
        r2 = lax.broadcasted_iota(I32, (t, t), 0)
        c2 = lax.broadcasted_iota(I32, (t, t), 1)
        earlier = (c2 < r2).astype(BF16)
        rank = jnp.dot(earlier, both, preferred_element_type=F32)
        tgt = poff + run_ref[...] + rank
        d0 = jnp.sum(a0.astype(F32) * tgt, axis=1, keepdims=True)
        d1 = jnp.sum(a1.astype(F32) * tgt, axis=1, keepdims=True)
        lane = lax.broadcasted_iota(I32, (t, LANES), 1)
        dest_ref[...] = jnp.where(lane == 0, d0, jnp.where(lane == 1, d1, 0.0)).astype(I32)
        run_ref[...] = run_ref[...] + colsum
        cnt_ref[...] = tot


def _route(oh0, oh1):
    n = oh0.shape[0]
    t = ROUTE_T
    blk = pl.BlockSpec((t, LANES), lambda ph, i: (i, 0))
    return pl.pallas_call(
        _route_kernel,
        out_shape=(jax.ShapeDtypeStruct((n, LANES), I32), jax.ShapeDtypeStruct((1, LANES), F32)),
        grid=(2, n // t),
        in_specs=[blk, blk],
        out_specs=(pl.BlockSpec((t, LANES), lambda ph, i: (i * ph, 0)),
                   pl.BlockSpec((1, LANES), lambda ph, i: (0, 0))),
        scratch_shapes=[pltpu.VMEM((1, LANES), F32), pltpu.VMEM((1, LANES), F32)],
        compiler_params=pltpu.CompilerParams(dimension_semantics=("arbitrary", "arbitrary"),
                                             vmem_limit_bytes=VMEM_LIMIT),
        name="route",
    )(oh0, oh1)


DISP_TM = 1024


def _dispatch_kernel(pad_ref, dest_ref, h_ref, hs_ref, zero_ref, sem, zsem):
    tm = h_ref.shape[0]

    @pl.when(pl.program_id(0) == 0)
    def _():
        zero_ref[...] = jnp.zeros_like(zero_ref)

        def zero_copy(row0):
            return pltpu.make_async_copy(zero_ref, hs_ref.at[pl.ds(pl.multiple_of(row0, MOE_TM), MOE_TM)], zsem)

        def fill(e, c):
            @pl.when(pad_ref[e] >= 0)
            def _():
                zero_copy(pad_ref[e]).start()
            return c

        def drain(e, c):
            @pl.when(pad_ref[e] >= 0)
            def _():
                zero_copy(pad_ref[e]).wait()
            return c

        lax.fori_loop(0, N_EXPERTS, fill, 0)
        lax.fori_loop(0, N_EXPERTS, drain, 0)
        used = pad_ref[N_EXPERTS]
        total = hs_ref.shape[0] // MOE_TM
        lax.fori_loop(used, total, lambda t, c: (zero_copy(t * MOE_TM).start(), c)[1], 0)
        lax.fori_loop(used, total, lambda t, c: (zero_copy(t * MOE_TM).wait(), c)[1], 0)

    def body(r, c):
        for s in range(2):
            pltpu.make_async_copy(h_ref.at[pl.ds(r, 1)], hs_ref.at[pl.ds(dest_ref[0, 0, 2 * r + s], 1)],
                                  sem).start()
        return c

    lax.fori_loop(0, tm, body, 0, unroll=DMA_UNROLL)
    for _ in range(2):
        pltpu.make_async_copy(h_ref, hs_ref.at[pl.ds(0, tm)], sem).wait()


def _dispatch(pad_start, dest3, h2, p_rows):
    n, w = h2.shape
    tm = DISP_TM
    grid_spec = pltpu.PrefetchScalarGridSpec(
        num_scalar_prefetch=1,
        grid=(n // tm,),
        in_specs=[pl.BlockSpec((1, 1, 2 * tm), lambda i, pad: (i, 0, 0), memory_space=pltpu.SMEM),
                  pl.BlockSpec((tm, w), lambda i, pad: (i, 0))],
        out_specs=pl.BlockSpec(memory_space=pl.ANY),
        scratch_shapes=[pltpu.VMEM((MOE_TM, w), h2.dtype), pltpu.SemaphoreType.DMA(()),
                        pltpu.SemaphoreType.DMA(())],
    )
    return pl.pallas_call(
        _dispatch_kernel,
        out_shape=jax.ShapeDtypeStruct((p_rows, w), h2.dtype),
        grid_spec=grid_spec,
        compiler_params=pltpu.CompilerParams(dimension_semantics=("arbitrary",),
                                             vmem_limit_bytes=VMEM_LIMIT),
        name="dispatch",
    )(pad_start, dest3, h2)


def _tile_tables(counts, n_tiles_max):
    cnt = counts[0, :N_EXPERTS].astype(I32)
    tiles = (cnt + MOE_TM - 1) // MOE_TM
    tiles_end = jnp.cumsum(tiles)
    tile = jnp.arange(n_tiles_max, dtype=I32)
    tile_expert = jnp.sum((tile[:, None] >= tiles_end[None, :]).astype(I32), axis=1)
    n_used = tiles_end[-1:].astype(I32)
    pad_start = jnp.where(tiles > 0, (tiles_end - 1) * MOE_TM, -1).astype(I32)
    return jnp.minimum(tile_expert, N_EXPERTS - 1), n_used, jnp.concatenate([pad_start, n_used])


def kernel(x, c, w_ada, b_ada, g_norm_mix, w_in, lb_logits, g_rec_out, g_att_out, w_out, g_norm_ffn,
           w_router_group, b_router_group, w_router_expert, b_router_expert,
           w_expert_gate, w_expert_up, w_expert_down, g_final):
    bsz, seq, d = x.shape
    n = bsz * seq
    assert w_ada.shape[0] == 1, "single trunk layer"
    layer = 0
    x2 = x.reshape(n, d)

    mod = _adaln(c, w_ada[layer], b_ada[layer])
    sh1, sc1, gt1, sh2, sc2, gt2 = [m.reshape(bsz, 1, d) for m in jnp.split(mod, 6, axis=-1)]

    w_in_bf = jnp.pad(w_in[layer], ((0, 0), (0, IN_PAD - IN_COLS))).astype(BF16)
    proj = _inproj(x2, sc1, sh1, g_norm_mix[layer].reshape(1, d), w_in_bf, seq)
    kb, vb, kxb = _kvprep(proj)
    rec = _hgrn(proj, lb_logits, g_rec_out[layer], bsz, seq, layer)
    att = _dsa(proj, kb, vb, kxb, g_att_out[layer], bsz, seq)

    wr = jnp.concatenate([w_router_group[layer], w_router_expert[layer]], axis=1)
    wr = jnp.pad(wr, ((0, 0), (0, LANES - wr.shape[1])))
    br = jnp.concatenate([b_router_group[layer], b_router_expert[layer]])
    br = jnp.pad(br, (0, LANES - br.shape[0])).reshape(1, LANES)
    x1, h2, oh0, oh1, wts = _outproj(rec, att, x2, gt1, sc2, sh2, g_norm_ffn[layer].reshape(1, d),
                                      w_out[layer].astype(BF16), wr, br, seq)

    dest, counts = _route(oh0, oh1)
    dest2 = dest[:, :2]
    p_rows = 2 * n + N_EXPERTS * MOE_TM
    tile_expert, n_tiles, pad_start = _tile_tables(counts, p_rows // MOE_TM)
    hs = _dispatch(pad_start, dest2.reshape(n // DISP_TM, 1, 2 * DISP_TM), h2, p_rows)
    ys = _moe(tile_expert, n_tiles, hs, w_expert_gate[layer], w_expert_up[layer], w_expert_down[layer])
    out = _final(dest2.reshape(n // FIN_TM, 1, 2 * FIN_TM), x1, wts, gt2, g_final.reshape(1, d), ys, seq)
    return out.reshape(bsz, seq, d)
```

```python
import functools

import jax
import jax.numpy as jnp
import numpy as np
from jax import lax
from jax.experimental import pallas as pl
from jax.experimental.pallas import tpu as pltpu

F32 = jnp.float32
BF16 = jnp.bfloat16
I32 = jnp.int32

EPS = 1e-6
LANES = 128

REC_HEADS = 8
REC_D = 128
REC_CHUNK = 64
REC_SUB = 16
ATT_HEADS = 8
ATT_DH = 128
ATT_KV_HEADS = 2
ATT_GROUP = ATT_HEADS // ATT_KV_HEADS
IDX_HEADS = 8
IDX_DIM = 64
TOPK_MAX = 256
N_GROUPS = 4
EXPERTS_PER_GROUP = 8
N_EXPERTS = N_GROUPS * EXPERTS_PER_GROUP
D_EXPERT = 512

OFF_RQ = 0
OFF_RF = 1024
OFF_RI = 2048
OFF_RG = 3072
OFF_AQ = 4096
OFF_AK = 5120
OFF_AV = 5376
OFF_IQ = 5632
OFF_IK = 6144
OFF_IW = 6208
IN_COLS = 6216
IN_PAD = 6272

VMEM_LIMIT = 48 * 1024 * 1024

INT_MIN = -(2 ** 31)
KEY_NEGINF = int(np.array(-np.inf, np.float32).view(np.int32)) ^ 0x7FFFFFFF
NEG_BIG = -1e30


def _silu(v):
    return v * jax.nn.sigmoid(v)


def _nt_dot(a, b):
    return lax.dot_general(a, b, (((1,), (1,)), ((), ())), preferred_element_type=F32)


def _tn_dot(a, b):
    return lax.dot_general(a, b, (((0,), (0,)), ((), ())), preferred_element_type=F32)


def _adaln_kernel(c_ref, w_ref, b_ref, o_ref):
    ca = _silu(c_ref[...])
    o_ref[...] = jnp.dot(ca, w_ref[...], preferred_element_type=F32,
                         precision=lax.Precision.HIGHEST) + b_ref[...]


def _adaln(c, w, b):
    bsz, d = c.shape
    n = w.shape[1]
    tn = 512
    return pl.pallas_call(
        _adaln_kernel,
        out_shape=jax.ShapeDtypeStruct((bsz, n), F32),
        grid=(n // tn,),
        in_specs=[pl.BlockSpec((bsz, d), lambda j: (0, 0)),
                  pl.BlockSpec((d, tn), lambda j: (0, j)),
                  pl.BlockSpec((1, tn), lambda j: (0, j))],
        out_specs=pl.BlockSpec((bsz, tn), lambda j: (0, j)),
        compiler_params=pltpu.CompilerParams(dimension_semantics=("arbitrary",),
                                             vmem_limit_bytes=VMEM_LIMIT),
        name="adaln",
    )(c, w, b.reshape(1, n))


def _norm_mod(x, g, sc, sh):
    xn = x * lax.rsqrt(jnp.mean(x * x, axis=-1, keepdims=True) + EPS)
    return xn * g * (1.0 + sc) + sh


IN_RB = 256


def _inproj_kernel(x_ref, sc_ref, sh_ref, g_ref, w_ref, o_ref, h_ref):
    @pl.when(pl.program_id(1) == 0)
    def _():
        tm = x_ref.shape[0]
        rb = min(IN_RB, tm)
        hs = []
        for r in range(0, tm, rb):
            h = _norm_mod(x_ref[r:r + rb, :], g_ref[...], sc_ref[0], sh_ref[0]).astype(BF16)
            h_ref[r:r + rb, :] = h
            hs.append(h)
        w = w_ref[...]
        for i, r in enumerate(range(0, tm, rb)):
            o_ref[r:r + rb, :] = jnp.dot(hs[i], w, preferred_element_type=F32)

    @pl.when(pl.program_id(1) != 0)
    def _():
        o_ref[...] = jnp.dot(h_ref[...], w_ref[...], preferred_element_type=F32)


def _inproj(x2, sc, sh, g, w_bf, seq):
    n, d = x2.shape
    ncol = w_bf.shape[1]
    tm = min(1024, seq)
    tn = 896
    per_b = seq // tm
    return pl.pallas_call(
        _inproj_kernel,
        out_shape=jax.ShapeDtypeStruct((n, ncol), F32),
        grid=(n // tm, ncol // tn),
        in_specs=[pl.BlockSpec((tm, d), lambda i, j: (i, 0)),
                  pl.BlockSpec((1, 1, d), lambda i, j: (i // per_b, 0, 0)),
                  pl.BlockSpec((1, 1, d), lambda i, j: (i // per_b, 0, 0)),
                  pl.BlockSpec((1, d), lambda i, j: (0, 0)),
                  pl.BlockSpec((d, tn), lambda i, j: (0, j))],
        out_specs=pl.BlockSpec((tm, tn), lambda i, j: (i, j)),
        scratch_shapes=[pltpu.VMEM((tm, d), BF16)],
        compiler_params=pltpu.CompilerParams(dimension_semantics=("arbitrary", "arbitrary"),
                                             vmem_limit_bytes=VMEM_LIMIT),
        name="inproj",
    )(x2, sc, sh, g, w_bf)


VT_ROWS = ATT_DH + 16


def _kvprep_kernel(kv_ref, ik_ref, k_ref, vt_ref, kx_ref):
    kv = kv_ref[...]
    tm = kv.shape[0]
    k_ref[...] = kv[:, :256].astype(BF16)
    tail = jnp.where(lax.broadcasted_iota(I32, (16, tm), 0) == 0, 1.0, 0.0)
    for g in range(ATT_KV_HEADS):
        vt = kv[:, 256 + g * ATT_DH:256 + (g + 1) * ATT_DH].T
        vt_ref[g] = jnp.concatenate([vt, tail], axis=0).astype(BF16)
    kx_ref[...] = ik_ref[...][:, :IDX_DIM].astype(BF16)


def _kvprep(proj):
    n = proj.shape[0]
    tm = 512
    return pl.pallas_call(
        _kvprep_kernel,
        out_shape=(jax.ShapeDtypeStruct((n, 256), BF16),
                   jax.ShapeDtypeStruct((ATT_KV_HEADS, VT_ROWS, n), BF16),
                   jax.ShapeDtypeStruct((n, IDX_DIM), BF16)),
        grid=(n // tm,),
        in_specs=[pl.BlockSpec((tm, 512), lambda i: (i, OFF_AK // 512)),
                  pl.BlockSpec((tm, LANES), lambda i: (i, OFF_IK // LANES))],
        out_specs=(pl.BlockSpec((tm, 256), lambda i: (i, 0)),
                   pl.BlockSpec((ATT_KV_HEADS, VT_ROWS, tm), lambda i: (0, 0, i)),
                   pl.BlockSpec((tm, IDX_DIM), lambda i: (i, 0))),
        compiler_params=pltpu.CompilerParams(dimension_semantics=("arbitrary",),
                                             vmem_limit_bytes=VMEM_LIMIT),
        name="kvprep",
    )(proj, proj)


def _hgrn_kernel(q_ref, f_ref, i_ref, g_ref, lbl_ref, gout_ref, o_ref, st_ref, *, chunks, layer):
    @pl.when(pl.program_id(2) == 0)
    def _():
        st_ref[...] = jnp.zeros_like(st_ref)

    lbl = lbl_ref[...]
    e = jnp.exp(lbl - jnp.max(lbl, axis=0, keepdims=True))
    sm = e / jnp.sum(e, axis=0, keepdims=True)
    lb = jnp.sum(sm[: layer + 1], axis=0, keepdims=True)
    gout = gout_ref[...]

    c = REC_CHUNK
    nsub = c // REC_SUB
    rr = lax.broadcasted_iota(I32, (c, c), 0)
    cc = lax.broadcasted_iota(I32, (c, c), 1)
    row = lax.broadcasted_iota(I32, (c, REC_D), 0)
    sub = lax.shift_right_logical(row, REC_SUB.bit_length() - 1)

    f = lb + (1.0 - lb) * jax.nn.sigmoid(f_ref[...])
    logf = jnp.log2(f)
    k = 1.0 - f
    qf = _silu(q_ref[...]) * (REC_D ** -0.5)
    vb = i_ref[...].astype(BF16)

    halves = [1 << l for l in range(REC_SUB.bit_length() - 1)]
    logf_w = jnp.concatenate([logf[ci * c:(ci + 1) * c] for ci in range(chunks)], axis=1)
    b_all = jnp.dot((rr >= cc).astype(F32), logf_w, preferred_element_type=F32,
                    precision=lax.Precision.HIGHEST)
    row_w = lax.broadcasted_iota(I32, b_all.shape, 0)
    blk_end, split = b_all, {}
    for h in halves:
        first = (row_w & (2 * h - 1)) < h
        split[h] = jnp.where(first, blk_end, pltpu.roll(blk_end, h, 0))
        blk_end = jnp.where(first, pltpu.roll(blk_end, c - h, 0), blk_end)
    lvl_mask = [jnp.logical_and((rr ^ cc) < 2 * h, jnp.logical_and((rr & h) != 0, (cc & h) == 0)) for h in halves]
    eye = rr == cc
    in_second = {h: (row & h) != 0 for h in halves}
    after_sub = [sub > i for i in range(nsub - 1)]
    in_sub = [sub == i for i in range(nsub - 1)]

    qxs, kxs, qls, kls, diag, upd, q_in, decay = [], [], [], [], [], [], [], []
    for ci in range(chunks):
        sl = slice(ci * c, (ci + 1) * c)
        cols = slice(ci * REC_D, (ci + 1) * REC_D)
        b = b_all[:, cols]
        kc, qc = k[sl], qf[sl]
        qparts, kparts = [], []
        for i in range(nsub - 1):
            r = b[(i + 1) * REC_SUB - 1:(i + 1) * REC_SUB, :]
            qparts.append(jnp.where(after_sub[i], qc * jnp.exp2(b - r), 0.0))
            kparts.append(jnp.where(in_sub[i], kc * jnp.exp2(r - b), 0.0))
        qxs.append(jnp.concatenate(qparts, axis=1).astype(BF16))
        kxs.append(jnp.concatenate(kparts, axis=1).astype(BF16))
        for h in halves:
            e = jnp.exp2(-jnp.abs(b - split[h][:, cols]))
            qls.append(jnp.where(in_second[h], qc * e, 0.0).astype(BF16))
            kls.append(jnp.where(in_second[h], 0.0, kc * e).astype(BF16))
        diag.append(jnp.sum(qc * kc, axis=1, keepdims=True))
        b_end = b[c - 1:c, :]
        upd.append((kc * jnp.exp2(b_end - b)).astype(BF16))
        q_in.append((qc * jnp.exp2(b)).astype(BF16))
        decay.append(jnp.exp2(b_end))
    nl = len(halves)
    cross = [_nt_dot(qxs[ci], kxs[ci]) for ci in range(chunks)]
    within = [[_nt_dot(qls[ci * nl + l], kls[ci * nl + l]) for l in range(nl)] for ci in range(chunks)]
    scores = []
    for ci in range(chunks):
        s = cross[ci] + jnp.where(eye, diag[ci], 0.0)
        for l in range(nl):
            s = s + jnp.where(lvl_mask[l], within[ci][l], 0.0)
        scores.append(s.astype(BF16))
    upd = [_tn_dot(vb[ci * c:(ci + 1) * c], upd[ci]) for ci in range(chunks)]
    intra = [jnp.dot(scores[ci], vb[ci * c:(ci + 1) * c], preferred_element_type=F32) for ci in range(chunks)]

    st = st_ref[...]
    outs = []
    for ci in range(chunks):
        outs.append(_nt_dot(q_in[ci], st.astype(BF16)) + intra[ci])
        st = st * decay[ci] + upd[ci]
    st_ref[...] = st

    o = jnp.concatenate(outs, axis=0)
    on = o * lax.rsqrt(jnp.mean(o * o, axis=-1, keepdims=True) + EPS)
    o_ref[...] = (on * gout * _silu(g_ref[...])).astype(o_ref.dtype)


def _hgrn(proj, lb_logits, g_out, bsz, seq, layer):
    n = proj.shape[0]
    tc = min(512, seq)
    per_b = seq // tc
    nl = lb_logits.shape[0]

    def col(off):
        return lambda b, h, c: (b * per_b + c, off // REC_D + h)

    return pl.pallas_call(
        functools.partial(_hgrn_kernel, chunks=tc // REC_CHUNK, layer=layer),
        out_shape=jax.ShapeDtypeStruct((n, REC_HEADS * REC_D), BF16),
        grid=(bsz, REC_HEADS, per_b),
        in_specs=[pl.BlockSpec((tc, REC_D), col(OFF_RQ)),
                  pl.BlockSpec((tc, REC_D), col(OFF_RF)),
                  pl.BlockSpec((tc, REC_D), col(OFF_RI)),
                  pl.BlockSpec((tc, REC_D), col(OFF_RG)),
                  pl.BlockSpec((nl, REC_D), lambda b, h, c: (0, h)),
                  pl.BlockSpec((1, REC_D), lambda b, h, c: (0, h))],
        out_specs=pl.BlockSpec((tc, REC_D), lambda b, h, c: (b * per_b + c, h)),
        scratch_shapes=[pltpu.VMEM((REC_D, REC_D), F32)],
        compiler_params=pltpu.CompilerParams(
            dimension_semantics=("arbitrary", "arbitrary", "arbitrary"),
            vmem_limit_bytes=VMEM_LIMIT),
        name="hgrn2",
    )(proj, proj, proj, proj, lb_logits, g_out.reshape(1, -1))


TQ = 128
TK = 1024
TK_TAIL = TK // 2
LOG2E = 1.4426950408889634
CNT_ROWS = 64
CNT_BLOCK = 256
SETTLE_EVERY = 4
VALUE_SPLITS = 4


def _dsa_kernel(qi_ref, iw_ref, aq_ref, kx_ref, k_ref, vt_ref, gout_ref, o_ref,
                sc_ref, m_ref, acc_ref, *, ksel, seq):
    t0 = pl.program_id(1) * TQ
    need = t0 + TQ
    nfull = (need + TK_TAIL - 1) // TK
    tail0 = pl.multiple_of(nfull * TK, TK)
    has_tail = need > tail0
    nhalf = 2 * nfull + has_tail.astype(I32)
    qpos = t0 + lax.broadcasted_iota(I32, (1, TQ), 1)
    krow = lax.broadcasted_iota(I32, (TK, TQ), 0)

    def over_tiles(body):
        lax.fori_loop(0, nfull, lambda kt, c: (body(pl.multiple_of(kt * TK, TK), TK), c)[1], 0)

        @pl.when(has_tail)
        def _():
            body(tail0, TK_TAIL)

    qit = (qi_ref[...] * (IDX_DIM ** -0.5)).T
    qht = jnp.concatenate([qit[h * IDX_DIM:(h + 1) * IDX_DIM] for h in range(IDX_HEADS)],
                          axis=1).astype(BF16)
    iwt = iw_ref[...].T
    wrow = [iwt[IDX_DIM + h:IDX_DIM + h + 1] * (IDX_HEADS ** -0.5) for h in range(IDX_HEADS)]

    def score_body(k0, rows):
        rel = jnp.dot(kx_ref[pl.ds(k0, rows), :], qht, preferred_element_type=F32)
        sc = jnp.zeros((rows, TQ), F32)
        for h in range(IDX_HEADS):
            sc = sc + wrow[h] * jnp.maximum(rel[:, h * TQ:(h + 1) * TQ], 0.0)
        sc_ref[pl.ds(k0, rows), :] = jnp.where(k0 + krow[:rows] <= qpos, sc, -jnp.inf)

    over_tiles(score_body)

    def u_to_float(u):
        key = u ^ INT_MIN
        return lax.bitcast_convert_type(key ^ ((key >> 31) & 0x7FFFFFFF), F32)

    def float_to_u(f):
        bits = lax.bitcast_convert_type(f, I32)
        return bits ^ ((bits >> 31) & 0x7FFFFFFF) ^ INT_MIN

    def count(pred):
        def block(k0, rows, acc):
            for u in range(0, rows, CNT_BLOCK):
                r0 = pl.multiple_of(k0 + u, CNT_BLOCK)
                hit = pred(sc_ref[pl.ds(r0, CNT_BLOCK), :], r0 + krow[:CNT_BLOCK])
                acc = acc + jnp.sum(jnp.where(hit, 1.0, 0.0).reshape(CNT_BLOCK // CNT_ROWS, CNT_ROWS, TQ), axis=0)
            return acc
        acc = lax.fori_loop(0, nfull, lambda kt, a: block(kt * TK, TK, a), jnp.zeros((CNT_ROWS, TQ), F32))
        acc = lax.cond(has_tail, lambda a: block(tail0, TK_TAIL, a), lambda a: a, acc)
        return jnp.sum(acc, axis=0, keepdims=True)

    def group_max(j, gm):
        r0 = pl.multiple_of(j * ksel, ksel)
        return jnp.maximum(gm, sc_ref[pl.ds(r0, ksel), :])

    gm = lax.fori_loop(0, nhalf * (TK_TAIL // ksel), group_max, jnp.full((ksel, TQ), -jnp.inf, F32))
    kf = float(ksel)

    def ult(a, b):
        return (a ^ INT_MIN) < (b ^ INT_MIN)

    def is_open(lo, hi):
        return ult(jnp.int32(1), hi - lo)

    def not_settled(cnt):
        return jnp.where(cnt == kf, 0.0, 1.0)

    def search_cond(st):
        _, lo, hi, cnt = st
        return jnp.max(jnp.where(is_open(lo, hi), not_settled(cnt), 0.0)) > 0.0

    def search_step(_, st):
        it, lo, hi, cnt = st
        mid_u = lo + lax.shift_right_logical(hi - lo, 1)
        mid_v = float_to_u(0.5 * (u_to_float(lo) + u_to_float(hi)))
        use_v = jnp.logical_and(it < VALUE_SPLITS, jnp.logical_and(ult(lo, mid_v), ult(mid_v, hi)))
        cand = jnp.where(is_open(lo, hi), jnp.where(use_v, mid_v, mid_u), lo)
        cand_f = u_to_float(cand)
        c = count(lambda s, pos: s >= cand_f)
        take = c >= kf
        return it + 1, jnp.where(take, cand, lo), jnp.where(take, hi, cand), jnp.where(take, c, cnt)

    def search_body(st):
        return lax.fori_loop(0, SETTLE_EVERY, search_step, st)

    _, u_thr, _, cnt_thr = lax.while_loop(
        search_cond, search_body,
        (jnp.int32(0), float_to_u(jnp.min(gm, axis=0, keepdims=True)),
         float_to_u(jnp.max(gm, axis=0, keepdims=True)) + 1, jnp.full((1, TQ), -1.0, F32)))
    thr = u_to_float(u_thr)
    real = (u_thr ^ INT_MIN) > KEY_NEGINF

    nbits = seq.bit_length()

    def resolve_ties():
        c_gt = count(lambda s, pos: s > thr)
        c_eq = count(lambda s, pos: s == thr)
        need = kf - c_gt
        excess = jnp.logical_and(c_eq > need, real)

        def find_last():
            def jbody(i, y):
                cand = y | lax.shift_left(jnp.int32(1), (nbits - 1) - i)
                below = count(lambda s, pos: jnp.logical_and(s == thr, pos < cand))
                return jnp.where(below <= need - 1.0, cand, y)
            return lax.fori_loop(0, nbits, jbody, jnp.zeros((1, TQ), I32))

        return lax.cond(jnp.max(jnp.where(excess, 1.0, 0.0)) > 0.0, find_last,
                        lambda: jnp.full((1, TQ), seq, I32))

    last = lax.cond(jnp.max(not_settled(cnt_thr)) > 0.0, resolve_ties, lambda: jnp.full((1, TQ), seq, I32))
    last = jnp.where(real, last, -1)
    thr_m = jnp.where(real, thr, -jnp.inf)

    aq = aq_ref[...] * ((ATT_DH ** -0.5) * LOG2E)
    qgt = []
    for g in range(ATT_KV_HEADS):
        blk = [aq[:, (g * ATT_GROUP + j) * ATT_DH:(g * ATT_GROUP + j + 1) * ATT_DH].T for j in range(ATT_GROUP)]
        qgt.append(jnp.concatenate(blk, axis=1).astype(BF16))

    m_ref[...] = jnp.full(m_ref.shape, NEG_BIG, F32)
    acc_ref[...] = jnp.zeros(acc_ref.shape, F32)

    def att_body(k0, rows):
        sc = sc_ref[pl.ds(k0, rows), :]
        sel = jnp.logical_or(sc > thr_m, jnp.logical_and(sc == thr_m, k0 + krow[:rows] <= last))
        bias = jnp.where(sel, 0.0, NEG_BIG)
        bias = jnp.concatenate([bias] * ATT_GROUP, axis=1)
        heads = range(ATT_KV_HEADS)
        ss = [jnp.dot(k_ref[pl.ds(k0, rows), g * ATT_DH:(g + 1) * ATT_DH], qgt[g],
                      preferred_element_type=F32) + bias for g in heads]
        ps, alphas = [], []
        for g in heads:
            m_old = m_ref[g]
            m_new = jnp.maximum(m_old, jnp.max(ss[g], axis=0, keepdims=True))
            m_ref[g] = m_new
            alphas.append(jnp.exp2(m_old - m_new))
            ps.append(jnp.exp2(ss[g] - m_new).astype(BF16))
        pvs = [jnp.dot(vt_ref[g, :, pl.ds(k0, rows)], ps[g], preferred_element_type=F32) for g in heads]
        for g in heads:
            acc_ref[g] = alphas[g] * acc_ref[g] + pvs[g]

    over_tiles(att_body)

    gout = gout_ref[...]
    for g in range(ATT_KV_HEADS):
        a = acc_ref[g]
        o = a[:ATT_DH] / a[ATT_DH:ATT_DH + 1]
        on = o * lax.rsqrt(jnp.mean(o * o, axis=0, keepdims=True) + EPS)
        for j in range(ATT_GROUP):
            hsl = slice((g * ATT_GROUP + j) * ATT_DH, (g * ATT_GROUP + j + 1) * ATT_DH)
            o_ref[:, hsl] = (on[:, j * TQ:(j + 1) * TQ].T * gout[:, hsl]).astype(o_ref.dtype)


def _dsa(proj, kb, vt, kxb, g_out, bsz, seq):
    n = proj.shape[0]
    nqb = seq // TQ
    ksel = min(TOPK_MAX, seq // 4)
    assert TK_TAIL % ksel == 0 and TK_TAIL % CNT_BLOCK == 0 and seq % TK == 0
    kb3 = kb.reshape(bsz, seq, 256)
    kx3 = kxb.reshape(bsz, seq, IDX_DIM)
    gq = ATT_GROUP * TQ
    return pl.pallas_call(
        functools.partial(_dsa_kernel, ksel=ksel, seq=seq),
        out_shape=jax.ShapeDtypeStruct((n, ATT_HEADS * ATT_DH), BF16),
        grid=(bsz, nqb),
        in_specs=[pl.BlockSpec((TQ, 512), lambda b, q: (b * nqb + q, OFF_IQ // 512)),
                  pl.BlockSpec((TQ, LANES), lambda b, q: (b * nqb + q, OFF_IK // LANES)),
                  pl.BlockSpec((TQ, 1024), lambda b, q: (b * nqb + q, OFF_AQ // 1024)),
                  pl.BlockSpec((None, seq, IDX_DIM), lambda b, q: (b, 0, 0)),
                  pl.BlockSpec((None, seq, 256), lambda b, q: (b, 0, 0)),
                  pl.BlockSpec((ATT_KV_HEADS, VT_ROWS, seq), lambda b, q: (0, 0, b)),
                  pl.BlockSpec((1, ATT_HEADS * ATT_DH), lambda b, q: (0, 0))],
        out_specs=pl.BlockSpec((TQ, ATT_HEADS * ATT_DH), lambda b, q: (b * nqb + q, 0)),
        scratch_shapes=[pltpu.VMEM((seq, TQ), F32),
                        pltpu.VMEM((ATT_KV_HEADS, 1, gq), F32),
                        pltpu.VMEM((ATT_KV_HEADS, VT_ROWS, gq), F32)],
        compiler_params=pltpu.CompilerParams(dimension_semantics=("arbitrary", "arbitrary"),
                                             vmem_limit_bytes=VMEM_LIMIT),
        name="dsa",
    )(proj, proj, proj, kx3, kb3, vt, g_out.reshape(1, -1))


OUT_RB = 128


def _outproj_kernel(rec_ref, att_ref, x_ref, gt_ref, sc_ref, sh_ref, g_ref, wo_ref, wrh_ref, wrl_ref, br_ref,
                    x1_ref, h2_ref, oh0_ref, oh1_ref, wts_ref):
    wo = wo_ref[...]
    half = rec_ref.shape[1]
    tm = x_ref.shape[0]
    blocks = [slice(r, r + OUT_RB) for r in range(0, tm, OUT_RB)]
    mixed = [jnp.dot(rec_ref[rs, :], wo[:half], preferred_element_type=F32)
             + jnp.dot(att_ref[rs, :], wo[half:], preferred_element_type=F32) for rs in blocks]
    his, los = [], []
    for rs, mx in zip(blocks, mixed):
        x1 = x_ref[rs, :] + gt_ref[0] * mx
        x1_ref[rs, :] = x1
        h2 = _norm_mod(x1, g_ref[...], sc_ref[0], sh_ref[0])
        h2_ref[rs, :] = h2
        hi = h2.astype(BF16)
        his.append(hi)
        los.append((h2 - hi.astype(F32)).astype(BF16))
    wrh, wrl = wrh_ref[...], wrl_ref[...]
    logits = jnp.concatenate(
        [jnp.dot(hi, wrh, preferred_element_type=F32) + jnp.dot(lo, wrh, preferred_element_type=F32)
         + jnp.dot(hi, wrl, preferred_element_type=F32) for hi, lo in zip(his, los)], axis=0) + br_ref[...]
    lane = lax.broadcasted_iota(I32, (tm, LANES), 1)
    big = jnp.int32(LANES)

    def argmax_first(vals, mask):
        mv = jnp.where(mask, vals, -jnp.inf)
        top = jnp.max(mv, axis=1, keepdims=True)
        idx = jnp.min(jnp.where(jnp.logical_and(mask, mv == top), lane, big), axis=1, keepdims=True)
        return top, idx

    gmask = lane < N_GROUPS
    gtop, gidx = argmax_first(logits, gmask)
    p_g = 1.0 / jnp.sum(jnp.where(gmask, jnp.exp(logits - gtop), 0.0), axis=1, keepdims=True)
    e_lo = N_GROUPS + gidx * EXPERTS_PER_GROUP
    emask = jnp.logical_and(lane >= e_lo, lane < e_lo + EXPERTS_PER_GROUP)
    v1, i1 = argmax_first(logits, emask)
    v2, i2 = argmax_first(logits, jnp.logical_and(emask, lane != i1))
    r = jnp.exp(v2 - v1)
    w1 = p_g / (1.0 + r)
    w2 = p_g * r / (1.0 + r)
    oh0_ref[...] = jnp.where(lane + N_GROUPS == i1, 1.0, 0.0).astype(BF16)
    oh1_ref[...] = jnp.where(lane + N_GROUPS == i2, 1.0, 0.0).astype(BF16)
    wts_ref[...] = jnp.where(lane == 0, w1, jnp.where(lane == 1, w2, 0.0))


def _outproj(rec, att, x2, gt, sc, sh, g, wo_bf, wr, br, seq):
    n, d = x2.shape
    wr_hi = wr.astype(BF16)
    wr_lo = (wr - wr_hi.astype(F32)).astype(BF16)
    tm = min(512, seq)
    per_b = seq // tm
    half = rec.shape[1]
    bspec = pl.BlockSpec((1, 1, d), lambda i: (i // per_b, 0, 0))
    return pl.pallas_call(
        _outproj_kernel,
        out_shape=(jax.ShapeDtypeStruct((n, d), F32),
                   jax.ShapeDtypeStruct((n, d), F32),
                   jax.ShapeDtypeStruct((n, LANES), BF16),
                   jax.ShapeDtypeStruct((n, LANES), BF16),
                   jax.ShapeDtypeStruct((n, LANES), F32)),
        grid=(n // tm,),
        in_specs=[pl.BlockSpec((tm, half), lambda i: (i, 0)),
                  pl.BlockSpec((tm, half), lambda i: (i, 0)),
                  pl.BlockSpec((tm, d), lambda i: (i, 0)),
                  bspec, bspec, bspec,
                  pl.BlockSpec((1, d), lambda i: (0, 0)),
                  pl.BlockSpec((2 * half, d), lambda i: (0, 0)),
                  pl.BlockSpec((d, LANES), lambda i: (0, 0)),
                  pl.BlockSpec((d, LANES), lambda i: (0, 0)),
                  pl.BlockSpec((1, LANES), lambda i: (0, 0))],
        out_specs=(pl.BlockSpec((tm, d), lambda i: (i, 0)),
                   pl.BlockSpec((tm, d), lambda i: (i, 0)),
                   pl.BlockSpec((tm, LANES), lambda i: (i, 0)),
                   pl.BlockSpec((tm, LANES), lambda i: (i, 0)),
                   pl.BlockSpec((tm, LANES), lambda i: (i, 0))),
        compiler_params=pltpu.CompilerParams(dimension_semantics=("arbitrary",),
                                             vmem_limit_bytes=VMEM_LIMIT),
        name="outproj",
    )(rec, att, x2, gt, sc, sh, g, wo_bf, wr_hi, wr_lo, br)


MOE_TM = 256


def _moe_kernel(te_ref, nt_ref, hs_ref, wg_ref, wu_ref, wd_ref, o_ref):
    t = pl.program_id(0)

    @pl.when(t < nt_ref[0])
    def _():
        xs = hs_ref[...].astype(BF16)
        gte = jnp.dot(xs, wg_ref[...].astype(BF16), preferred_element_type=F32)
        up = jnp.dot(xs, wu_ref[...].astype(BF16), preferred_element_type=F32)
        act = (_silu(gte) * up).astype(BF16)
        o_ref[...] = jnp.dot(act, wd_ref[...].astype(BF16), preferred_element_type=F32)

    @pl.when(t >= nt_ref[0])
    def _():
        o_ref[...] = jnp.zeros_like(o_ref)


def _moe(tile_expert, n_tiles, hs, wg, wu, wd):
    p = hs.shape[0]
    d, de = wg.shape[1], wg.shape[2]
    tm = MOE_TM
    grid_spec = pltpu.PrefetchScalarGridSpec(
        num_scalar_prefetch=2,
        grid=(p // tm,),
        in_specs=[pl.BlockSpec((tm, d), lambda t, te, nt: (jnp.minimum(t, nt[0] - 1), 0)),
                  pl.BlockSpec((None, d, de), lambda t, te, nt: (te[t], 0, 0)),
                  pl.BlockSpec((None, d, de), lambda t, te, nt: (te[t], 0, 0)),
                  pl.BlockSpec((None, de, d), lambda t, te, nt: (te[t], 0, 0))],
        out_specs=pl.BlockSpec((tm, d), lambda t, te, nt: (t, 0)),
    )
    return pl.pallas_call(
        _moe_kernel,
        out_shape=jax.ShapeDtypeStruct((p, d), F32),
        grid_spec=grid_spec,
        compiler_params=pltpu.CompilerParams(dimension_semantics=("arbitrary",),
                                             vmem_limit_bytes=56 * 1024 * 1024),
        name="moe",
    )(tile_expert, n_tiles, hs, wg, wu, wd)


FIN_TM = 256
DMA_UNROLL = 8


def _final_kernel(dcur_ref, dnext_ref, x1_ref, wts_ref, gt_ref, g_ref, ys_ref, o_ref, buf, sem):
    i = pl.program_id(0)
    n = pl.num_programs(0)
    tm = x1_ref.shape[0]

    def row_copy(dref, slot, r, s):
        return pltpu.make_async_copy(ys_ref.at[pl.ds(dref[0, 0, 2 * r + s], 1)],
                                     buf.at[slot, s, pl.ds(r, 1)], sem.at[slot])

    def issue(dref, slot):
        def body(r, c):
            row_copy(dref, slot, r, 0).start()
            row_copy(dref, slot, r, 1).start()
            return c
        lax.fori_loop(0, tm, body, 0, unroll=DMA_UNROLL)

    @pl.when(i == 0)
    def _():
        issue(dcur_ref, 0)

    @pl.when(i + 1 < n)
    def _():
        issue(dnext_ref, (i + 1) % 2)

    slot = i % 2
    for s in range(2):
        pltpu.make_async_copy(ys_ref.at[pl.ds(0, tm)], buf.at[slot, s], sem.at[slot]).wait()
    w = wts_ref[...]
    y = w[:, 0:1] * buf[slot, 0] + w[:, 1:2] * buf[slot, 1]
    xo = x1_ref[...] + gt_ref[0] * y
    o_ref[...] = xo * lax.rsqrt(jnp.mean(xo * xo, axis=-1, keepdims=True) + EPS) * g_ref[...]


def _final(dest3, x1, wts, gt, g, ys, seq):
    n, d = x1.shape
    tm = FIN_TM
    per_b = seq // tm
    steps = n // tm
    row = pl.BlockSpec((tm, d), lambda i: (i, 0))
    smem = lambda f: pl.BlockSpec((1, 1, 2 * tm), f, memory_space=pltpu.SMEM)
    return pl.pallas_call(
        _final_kernel,
        out_shape=jax.ShapeDtypeStruct((n, d), F32),
        grid=(steps,),
        in_specs=[smem(lambda i: (i, 0, 0)),
                  smem(lambda i: (jnp.minimum(i + 1, steps - 1), 0, 0)),
                  row,
                  pl.BlockSpec((tm, LANES), lambda i: (i, 0)),
                  pl.BlockSpec((1, 1, d), lambda i: (i // per_b, 0, 0)),
                  pl.BlockSpec((1, d), lambda i: (0, 0)),
                  pl.BlockSpec(memory_space=pl.ANY)],
        out_specs=row,
        scratch_shapes=[pltpu.VMEM((2, 2, tm, d), F32), pltpu.SemaphoreType.DMA((2,))],
        compiler_params=pltpu.CompilerParams(dimension_semantics=("arbitrary",),
                                             vmem_limit_bytes=VMEM_LIMIT),
        name="final",
    )(dest3, dest3, x1, wts, gt, g, ys)


ROUTE_T = 512


def _route_kernel(oh0_ref, oh1_ref, dest_ref, cnt_ref, run_ref, tot_ref):
    ph = pl.program_id(0)
    i = pl.program_id(1)
    a0 = oh0_ref[...]
    a1 = oh1_ref[...]
    both = a0 + a1
    colsum = jnp.sum(both.astype(F32), axis=0, keepdims=True)

    @pl.when(jnp.logical_and(ph == 0, i == 0))
    def _():
        tot_ref[...] = jnp.zeros_like(tot_ref)

    @pl.when(ph == 0)
    def _():
        tot_ref[...] = tot_ref[...] + colsum

    @pl.when(ph == 1)
    def _():
        @pl.when(i == 0)
        def _():
            run_ref[...] = jnp.zeros_like(run_ref)

        tot = tot_ref[...]
        tiles = jnp.ceil(tot * (1.0 / MOE_TM))
        rr = lax.broadcasted_iota(I32, (LANES, LANES), 0)
        cc = lax.broadcasted_iota(I32, (LANES, LANES), 1)
        before = (rr < cc).astype(BF16)
        tiles8 = jnp.broadcast_to(tiles, (8, LANES)).astype(BF16)
        poff = jnp.dot(tiles8, before, preferred_element_type=F32)[0:1] * float(MOE_TM)
        t = a0.shape[0]
        r2 = lax.broadcasted_iota(I32, (t, t), 0)
        c2 = lax.broadcasted_iota(I32, (t, t), 1)
        earlier = (c2 < r2).astype(BF16)
        rank = jnp.dot(earlier, both, preferred_element_type=F32)
        tgt = poff + run_ref[...] + rank
        d0 = jnp.sum(a0.astype(F32) * tgt, axis=1, keepdims=True)
        d1 = jnp.sum(a1.astype(F32) * tgt, axis=1, keepdims=True)
        lane = lax.broadcasted_iota(I32, (t, LANES), 1)
        dest_ref[...] = jnp.where(lane == 0, d0, jnp.where(lane == 1, d1, 0.0)).astype(I32)
        run_ref[...] = run_ref[...] + colsum
        cnt_ref[...] = tot


def _route(oh0, oh1):
    n = oh0.shape[0]
    t = ROUTE_T
    blk = pl.BlockSpec((t, LANES), lambda ph, i: (i, 0))
    return pl.pallas_call(
        _route_kernel,
        out_shape=(jax.ShapeDtypeStruct((n, LANES), I32), jax.ShapeDtypeStruct((1, LANES), F32)),
        grid=(2, n // t),
        in_specs=[blk, blk],
        out_specs=(pl.BlockSpec((t, LANES), lambda ph, i: (i * ph, 0)),
                   pl.BlockSpec((1, LANES), lambda ph, i: (0, 0))),
        scratch_shapes=[pltpu.VMEM((1, LANES), F32), pltpu.VMEM((1, LANES), F32)],
        compiler_params=pltpu.CompilerParams(dimension_semantics=("arbitrary", "arbitrary"),
                                             vmem_limit_bytes=VMEM_LIMIT),
        name="route",
    )(oh0, oh1)


DISP_TM = 1024


def _dispatch_kernel(pad_ref, dest_ref, h_ref, hs_ref, zero_ref, sem, zsem):
    tm = h_ref.shape[0]

    @pl.when(pl.program_id(0) == 0)
    def _():
        zero_ref[...] = jnp.zeros_like(zero_ref)

        def zero_copy(row0):
            return pltpu.make_async_copy(zero_ref, hs_ref.at[pl.ds(pl.multiple_of(row0, MOE_TM), MOE_TM)], zsem)

        def fill(e, c):
            @pl.when(pad_ref[e] >= 0)
            def _():
                zero_copy(pad_ref[e]).start()
            return c

        def drain(e, c):
            @pl.when(pad_ref[e] >= 0)
            def _():
                zero_copy(pad_ref[e]).wait()
            return c

        lax.fori_loop(0, N_EXPERTS, fill, 0)
        lax.fori_loop(0, N_EXPERTS, drain, 0)
        used = pad_ref[N_EXPERTS]
        total = hs_ref.shape[0] // MOE_TM
        lax.fori_loop(used, total, lambda t, c: (zero_copy(t * MOE_TM).start(), c)[1], 0)
        lax.fori_loop(used, total, lambda t, c: (zero_copy(t * MOE_TM).wait(), c)[1], 0)

    def body(r, c):
        for s in range(2):
            pltpu.make_async_copy(h_ref.at[pl.ds(r, 1)], hs_ref.at[pl.ds(dest_ref[0, 0, 2 * r + s], 1)],
                                  sem).start()
        return c

    lax.fori_loop(0, tm, body, 0, unroll=DMA_UNROLL)
    for _ in range(2):
        pltpu.make_async_copy(h_ref, hs_ref.at[pl.ds(0, tm)], sem).wait()


def _dispatch(pad_start, dest3, h2, p_rows):
    n, w = h2.shape
    tm = DISP_TM
    grid_spec = pltpu.PrefetchScalarGridSpec(
        num_scalar_prefetch=1,
        grid=(n // tm,),
        in_specs=[pl.BlockSpec((1, 1, 2 * tm), lambda i, pad: (i, 0, 0), memory_space=pltpu.SMEM),
                  pl.BlockSpec((tm, w), lambda i, pad: (i, 0))],
        out_specs=pl.BlockSpec(memory_space=pl.ANY),
        scratch_shapes=[pltpu.VMEM((MOE_TM, w), h2.dtype), pltpu.SemaphoreType.DMA(()),
                        pltpu.SemaphoreType.DMA(())],
    )
    return pl.pallas_call(
        _dispatch_kernel,
        out_shape=jax.ShapeDtypeStruct((p_rows, w), h2.dtype),
        grid_spec=grid_spec,
        compiler_params=pltpu.CompilerParams(dimension_semantics=("arbitrary",),
                                             vmem_limit_bytes=VMEM_LIMIT),
        name="dispatch",
    )(pad_start, dest3, h2)


def _tile_tables(counts, n_tiles_max):
    cnt = counts[0, :N_EXPERTS].astype(I32)
    tiles = (cnt + MOE_TM - 1) // MOE_TM
    tiles_end = jnp.cumsum(tiles)
    tile = jnp.arange(n_tiles_max, dtype=I32)
    tile_expert = jnp.sum((tile[:, None] >= tiles_end[None, :]).astype(I32), axis=1)
    n_used = tiles_end[-1:].astype(I32)
    pad_start = jnp.where(tiles > 0, (tiles_end - 1) * MOE_TM, -1).astype(I32)
    return jnp.minimum(tile_expert, N_EXPERTS - 1), n_used, jnp.concatenate([pad_start, n_used])


def kernel(x, c, w_ada, b_ada, g_norm_mix, w_in, lb_logits, g_rec_out, g_att_out, w_out, g_norm_ffn,
           w_router_group, b_router_group, w_router_expert, b_router_expert,
           w_expert_gate, w_expert_up, w_expert_down, g_final):
    bsz, seq, d = x.shape
    n = bsz * seq
    assert w_ada.shape[0] == 1, "single trunk layer"
    layer = 0
    x2 = x.reshape(n, d)

    mod = _adaln(c, w_ada[layer], b_ada[layer])
    sh1, sc1, gt1, sh2, sc2, gt2 = [m.reshape(bsz, 1, d) for m in jnp.split(mod, 6, axis=-1)]

    w_in_bf = jnp.pad(w_in[layer], ((0, 0), (0, IN_PAD - IN_COLS))).astype(BF16)
    proj = _inproj(x2, sc1, sh1, g_norm_mix[layer].reshape(1, d), w_in_bf, seq)
    kb, vb, kxb = _kvprep(proj)
    rec = _hgrn(proj, lb_logits, g_rec_out[layer], bsz, seq, layer)
    att = _dsa(proj, kb, vb, kxb, g_att_out[layer], bsz, seq)

    wr = jnp.concatenate([w_router_group[layer], w_router_expert[layer]], axis=1)
    wr = jnp.pad(wr, ((0, 0), (0, LANES - wr.shape[1])))
    br = jnp.concatenate([b_router_group[layer], b_router_expert[layer]])
    br = jnp.pad(br, (0, LANES - br.shape[0])).reshape(1, LANES)
    x1, h2, oh0, oh1, wts = _outproj(rec, att, x2, gt1, sc2, sh2, g_norm_ffn[layer].reshape(1, d),
                                      w_out[layer].astype(BF16), wr, br, seq)

    dest, counts = _route(oh0, oh1)
    dest2 = dest[:, :2]
    p_rows = 2 * n + N_EXPERTS * MOE_TM
    tile_expert, n_tiles, pad_start = _tile_tables(counts, p_rows // MOE_TM)
    hs = _dispatch(pad_start, dest2.reshape(n // DISP_TM, 1, 2 * DISP_TM), h2, p_rows)
    ys = _moe(tile_expert, n_tiles, hs, w_expert_gate[layer], w_expert_up[layer], w_expert_down[layer])
    out = _final(dest2.reshape(n // FIN_TM, 1, 2 * FIN_TM), x1, wts, gt2, g_final.reshape(1, d), ys, seq)
    return out.reshape(bsz, seq, d)
```

```python
import functools

import jax
import jax.numpy as jnp
import numpy as np
from jax import lax
from jax.experimental import pallas as pl
from jax.experimental.pallas import tpu as pltpu

F32 = jnp.float32
BF16 = jnp.bfloat16
I32 = jnp.int32

EPS = 1e-6
LANES = 128

REC_HEADS = 8
REC_D = 128
REC_CHUNK = 64
REC_SUB = 16
ATT_HEADS = 8
ATT_DH = 128
ATT_KV_HEADS = 2
ATT_GROUP = ATT_HEADS // ATT_KV_HEADS
IDX_HEADS = 8
IDX_DIM = 64
TOPK_MAX = 256
N_GROUPS = 4
EXPERTS_PER_GROUP = 8
N_EXPERTS = N_GROUPS * EXPERTS_PER_GROUP
D_EXPERT = 512

OFF_RQ = 0
OFF_RF = 1024
OFF_RI = 2048
OFF_RG = 3072
OFF_AQ = 4096
OFF_AK = 5120
OFF_AV = 5376
OFF_IQ = 5632
OFF_IK = 6144
OFF_IW = 6208
IN_COLS = 6216
IN_PAD = 6272

VMEM_LIMIT = 48 * 1024 * 1024

INT_MIN = -(2 ** 31)
KEY_NEGINF = int(np.array(-np.inf, np.float32).view(np.int32)) ^ 0x7FFFFFFF
NEG_BIG = -1e30


def _silu(v):
    return v * jax.nn.sigmoid(v)


def _nt_dot(a, b):
    return lax.dot_general(a, b, (((1,), (1,)), ((), ())), preferred_element_type=F32)


def _tn_dot(a, b):
    return lax.dot_general(a, b, (((0,), (0,)), ((), ())), preferred_element_type=F32)


def _adaln_kernel(c_ref, w_ref, b_ref, o_ref):
    ca = _silu(c_ref[...])
    o_ref[...] = jnp.dot(ca, w_ref[...], preferred_element_type=F32,
                         precision=lax.Precision.HIGHEST) + b_ref[...]


def _adaln(c, w, b):
    bsz, d = c.shape
    n = w.shape[1]
    tn = 512
    return pl.pallas_call(
        _adaln_kernel,
        out_shape=jax.ShapeDtypeStruct((bsz, n), F32),
        grid=(n // tn,),
        in_specs=[pl.BlockSpec((bsz, d), lambda j: (0, 0)),
                  pl.BlockSpec((d, tn), lambda j: (0, j)),
                  pl.BlockSpec((1, tn), lambda j: (0, j))],
        out_specs=pl.BlockSpec((bsz, tn), lambda j: (0, j)),
        compiler_params=pltpu.CompilerParams(dimension_semantics=("arbitrary",),
                                             vmem_limit_bytes=VMEM_LIMIT),
        name="adaln",
    )(c, w, b.reshape(1, n))


def _norm_mod(x, g, sc, sh):
    xn = x * lax.rsqrt(jnp.mean(x * x, axis=-1, keepdims=True) + EPS)
    return xn * g * (1.0 + sc) + sh


IN_RB = 256


def _inproj_kernel(x_ref, sc_ref, sh_ref, g_ref, w_ref, o_ref, h_ref):
    @pl.when(pl.program_id(1) == 0)
    def _():
        tm = x_ref.shape[0]
        rb = min(IN_RB, tm)
        hs = []
        for r in range(0, tm, rb):
            h = _norm_mod(x_ref[r:r + rb, :], g_ref[...], sc_ref[0], sh_ref[0]).astype(BF16)
            h_ref[r:r + rb, :] = h
            hs.append(h)
        w = w_ref[...]
        for i, r in enumerate(range(0, tm, rb)):
            o_ref[r:r + rb, :] = jnp.dot(hs[i], w, preferred_element_type=F32)

    @pl.when(pl.program_id(1) != 0)
    def _():
        o_ref[...] = jnp.dot(h_ref[...], w_ref[...], preferred_element_type=F32)


def _inproj(x2, sc, sh, g, w_bf, seq):
    n, d = x2.shape
    ncol = w_bf.shape[1]
    tm = min(1024, seq)
    tn = 896
    per_b = seq // tm
    return pl.pallas_call(
        _inproj_kernel,
        out_shape=jax.ShapeDtypeStruct((n, ncol), F32),
        grid=(n // tm, ncol // tn),
        in_specs=[pl.BlockSpec((tm, d), lambda i, j: (i, 0)),
                  pl.BlockSpec((1, 1, d), lambda i, j: (i // per_b, 0, 0)),
                  pl.BlockSpec((1, 1, d), lambda i, j: (i // per_b, 0, 0)),
                  pl.BlockSpec((1, d), lambda i, j: (0, 0)),
                  pl.BlockSpec((d, tn), lambda i, j: (0, j))],
        out_specs=pl.BlockSpec((tm, tn), lambda i, j: (i, j)),
        scratch_shapes=[pltpu.VMEM((tm, d), BF16)],
        compiler_params=pltpu.CompilerParams(dimension_semantics=("arbitrary", "arbitrary"),
                                             vmem_limit_bytes=VMEM_LIMIT),
        name="inproj",
    )(x2, sc, sh, g, w_bf)


VT_ROWS = ATT_DH + 16


def _kvprep_kernel(kv_ref, ik_ref, k_ref, vt_ref, kx_ref):
    kv = kv_ref[...]
    tm = kv.shape[0]
    k_ref[...] = kv[:, :256].astype(BF16)
    tail = jnp.where(lax.broadcasted_iota(I32, (16, tm), 0) == 0, 1.0, 0.0)
    for g in range(ATT_KV_HEADS):
        vt = kv[:, 256 + g * ATT_DH:256 + (g + 1) * ATT_DH].T
        vt_ref[g] = jnp.concatenate([vt, tail], axis=0).astype(BF16)
    kx_ref[...] = ik_ref[...][:, :IDX_DIM].astype(BF16)


def _kvprep(proj):
    n = proj.shape[0]
    tm = 512
    return pl.pallas_call(
        _kvprep_kernel,
        out_shape=(jax.ShapeDtypeStruct((n, 256), BF16),
                   jax.ShapeDtypeStruct((ATT_KV_HEADS, VT_ROWS, n), BF16),
                   jax.ShapeDtypeStruct((n, IDX_DIM), BF16)),
        grid=(n // tm,),
        in_specs=[pl.BlockSpec((tm, 512), lambda i: (i, OFF_AK // 512)),
                  pl.BlockSpec((tm, LANES), lambda i: (i, OFF_IK // LANES))],
        out_specs=(pl.BlockSpec((tm, 256), lambda i: (i, 0)),
                   pl.BlockSpec((ATT_KV_HEADS, VT_ROWS, tm), lambda i: (0, 0, i)),
                   pl.BlockSpec((tm, IDX_DIM), lambda i: (i, 0))),
        compiler_params=pltpu.CompilerParams(dimension_semantics=("arbitrary",),
                                             vmem_limit_bytes=VMEM_LIMIT),
        name="kvprep",
    )(proj, proj)


def _hgrn_kernel(q_ref, f_ref, i_ref, g_ref, lbl_ref, gout_ref, o_ref, st_ref, *, chunks, layer):
    @pl.when(pl.program_id(2) == 0)
    def _():
        st_ref[...] = jnp.zeros_like(st_ref)

    lbl = lbl_ref[...]
    e = jnp.exp(lbl - jnp.max(lbl, axis=0, keepdims=True))
    sm = e / jnp.sum(e, axis=0, keepdims=True)
    lb = jnp.sum(sm[: layer + 1], axis=0, keepdims=True)
    gout = gout_ref[...]

    c = REC_CHUNK
    nsub = c // REC_SUB
    rr = lax.broadcasted_iota(I32, (c, c), 0)
    cc = lax.broadcasted_iota(I32, (c, c), 1)
    row = lax.broadcasted_iota(I32, (c, REC_D), 0)
    sub = lax.shift_right_logical(row, REC_SUB.bit_length() - 1)

    f = lb + (1.0 - lb) * jax.nn.sigmoid(f_ref[...])
    logf = jnp.log2(f)
    k = 1.0 - f
    qf = _silu(q_ref[...]) * (REC_D ** -0.5)
    vb = i_ref[...].astype(BF16)

    halves = [1 << l for l in range(REC_SUB.bit_length() - 1)]
    logf_w = jnp.concatenate([logf[ci * c:(ci + 1) * c] for ci in range(chunks)], axis=1)
    b_all = jnp.dot((rr >= cc).astype(F32), logf_w, preferred_element_type=F32,
                    precision=lax.Precision.HIGHEST)
    row_w = lax.broadcasted_iota(I32, b_all.shape, 0)
    blk_end, split = b_all, {}
    for h in halves:
        first = (row_w & (2 * h - 1)) < h
        split[h] = jnp.where(first, blk_end, pltpu.roll(blk_end, h, 0))
        blk_end = jnp.where(first, pltpu.roll(blk_end, c - h, 0), blk_end)
    lvl_mask = [jnp.logical_and((rr ^ cc) < 2 * h, jnp.logical_and((rr & h) != 0, (cc & h) == 0)) for h in halves]
    eye = rr == cc
    in_second = {h: (row & h) != 0 for h in halves}
    after_sub = [sub > i for i in range(nsub - 1)]
    in_sub = [sub == i for i in range(nsub - 1)]

    qxs, kxs, qls, kls, diag, upd, q_in, decay = [], [], [], [], [], [], [], []
    for ci in range(chunks):
        sl = slice(ci * c, (ci + 1) * c)
        cols = slice(ci * REC_D, (ci + 1) * REC_D)
        b = b_all[:, cols]
        kc, qc = k[sl], qf[sl]
        qparts, kparts = [], []
        for i in range(nsub - 1):
            r = b[(i + 1) * REC_SUB - 1:(i + 1) * REC_SUB, :]
            qparts.append(jnp.where(after_sub[i], qc * jnp.exp2(b - r), 0.0))
            kparts.append(jnp.where(in_sub[i], kc * jnp.exp2(r - b), 0.0))
        qxs.append(jnp.concatenate(qparts, axis=1).astype(BF16))
        kxs.append(jnp.concatenate(kparts, axis=1).astype(BF16))
        for h in halves:
            e = jnp.exp2(-jnp.abs(b - split[h][:, cols]))
            qls.append(jnp.where(in_second[h], qc * e, 0.0).astype(BF16))
            kls.append(jnp.where(in_second[h], 0.0, kc * e).astype(BF16))
        diag.append(jnp.sum(qc * kc, axis=1, keepdims=True))
        b_end = b[c - 1:c, :]
        upd.append((kc * jnp.exp2(b_end - b)).astype(BF16))
        q_in.append((qc * jnp.exp2(b)).astype(BF16))
        decay.append(jnp.exp2(b_end))
    nl = len(halves)
    cross = [_nt_dot(qxs[ci], kxs[ci]) for ci in range(chunks)]
    within = [[_nt_dot(qls[ci * nl + l], kls[ci * nl + l]) for l in range(nl)] for ci in range(chunks)]
    scores = []
    for ci in range(chunks):
        s = cross[ci] + jnp.where(eye, diag[ci], 0.0)
        for l in range(nl):
            s = s + jnp.where(lvl_mask[l], within[ci][l], 0.0)
        scores.append(s.astype(BF16))
    upd = [_tn_dot(vb[ci * c:(ci + 1) * c], upd[ci]) for ci in range(chunks)]
    intra = [jnp.dot(scores[ci], vb[ci * c:(ci + 1) * c], preferred_element_type=F32) for ci in range(chunks)]

    st = st_ref[...]
    outs = []
    for ci in range(chunks):
        outs.append(_nt_dot(q_in[ci], st.astype(BF16)) + intra[ci])
        st = st * decay[ci] + upd[ci]
    st_ref[...] = st

    o = jnp.concatenate(outs, axis=0)
    on = o * lax.rsqrt(jnp.mean(o * o, axis=-1, keepdims=True) + EPS)
    o_ref[...] = (on * gout * _silu(g_ref[...])).astype(o_ref.dtype)


def _hgrn(proj, lb_logits, g_out, bsz, seq, layer):
    n = proj.shape[0]
    tc = min(2048, seq)
    per_b = seq // tc
    nl = lb_logits.shape[0]

    def col(off):
        return lambda b, h, c: (b * per_b + c, off // REC_D + h)

    return pl.pallas_call(
        functools.partial(_hgrn_kernel, chunks=tc // REC_CHUNK, layer=layer),
        out_shape=jax.ShapeDtypeStruct((n, REC_HEADS * REC_D), BF16),
        grid=(bsz, REC_HEADS, per_b),
        in_specs=[pl.BlockSpec((tc, REC_D), col(OFF_RQ)),
                  pl.BlockSpec((tc, REC_D), col(OFF_RF)),
                  pl.BlockSpec((tc, REC_D), col(OFF_RI)),
                  pl.BlockSpec((tc, REC_D), col(OFF_RG)),
                  pl.BlockSpec((nl, REC_D), lambda b, h, c: (0, h)),
                  pl.BlockSpec((1, REC_D), lambda b, h, c: (0, h))],
        out_specs=pl.BlockSpec((tc, REC_D), lambda b, h, c: (b * per_b + c, h)),
        scratch_shapes=[pltpu.VMEM((REC_D, REC_D), F32)],
        compiler_params=pltpu.CompilerParams(
            dimension_semantics=("arbitrary", "arbitrary", "arbitrary"),
            vmem_limit_bytes=VMEM_LIMIT),
        name="hgrn2",
    )(proj, proj, proj, proj, lb_logits, g_out.reshape(1, -1))


TQ = 128
TK = 1024
TK_TAIL = TK // 2
LOG2E = 1.4426950408889634
CNT_ROWS = 64
CNT_BLOCK = 256
SETTLE_EVERY = 4
VALUE_SPLITS = 4


def _dsa_kernel(qi_ref, iw_ref, aq_ref, kx_ref, k_ref, vt_ref, gout_ref, o_ref,
                sc_ref, m_ref, acc_ref, *, ksel, seq):
    t0 = pl.program_id(1) * TQ
    need = t0 + TQ
    nfull = (need + TK_TAIL - 1) // TK
    tail0 = pl.multiple_of(nfull * TK, TK)
    has_tail = need > tail0
    nhalf = 2 * nfull + has_tail.astype(I32)
    qpos = t0 + lax.broadcasted_iota(I32, (1, TQ), 1)
    krow = lax.broadcasted_iota(I32, (TK, TQ), 0)

    def over_tiles(body):
        lax.fori_loop(0, nfull, lambda kt, c: (body(pl.multiple_of(kt * TK, TK), TK), c)[1], 0)

        @pl.when(has_tail)
        def _():
            body(tail0, TK_TAIL)

    qit = (qi_ref[...] * (IDX_DIM ** -0.5)).T
    qht = jnp.concatenate([qit[h * IDX_DIM:(h + 1) * IDX_DIM] for h in range(IDX_HEADS)],
                          axis=1).astype(BF16)
    iwt = iw_ref[...].T
    wrow = [iwt[IDX_DIM + h:IDX_DIM + h + 1] * (IDX_HEADS ** -0.5) for h in range(IDX_HEADS)]

    def score_body(k0, rows):
        rel = jnp.dot(kx_ref[pl.ds(k0, rows), :], qht, preferred_element_type=F32)
        sc = jnp.zeros((rows, TQ), F32)
        for h in range(IDX_HEADS):
            sc = sc + wrow[h] * jnp.maximum(rel[:, h * TQ:(h + 1) * TQ], 0.0)
        sc_ref[pl.ds(k0, rows), :] = jnp.where(k0 + krow[:rows] <= qpos, sc, -jnp.inf)

    over_tiles(score_body)

    def u_to_float(u):
        key = u ^ INT_MIN
        return lax.bitcast_convert_type(key ^ ((key >> 31) & 0x7FFFFFFF), F32)

    def float_to_u(f):
        bits = lax.bitcast_convert_type(f, I32)
        return bits ^ ((bits >> 31) & 0x7FFFFFFF) ^ INT_MIN

    def count(pred):
        def block(k0, rows, acc):
            for u in range(0, rows, CNT_BLOCK):
                r0 = pl.multiple_of(k0 + u, CNT_BLOCK)
                hit = pred(sc_ref[pl.ds(r0, CNT_BLOCK), :], r0 + krow[:CNT_BLOCK])
                acc = acc + jnp.sum(jnp.where(hit, 1.0, 0.0).reshape(CNT_BLOCK // CNT_ROWS, CNT_ROWS, TQ), axis=0)
            return acc
        acc = lax.fori_loop(0, nfull, lambda kt, a: block(kt * TK, TK, a), jnp.zeros((CNT_ROWS, TQ), F32))
        acc = lax.cond(has_tail, lambda a: block(tail0, TK_TAIL, a), lambda a: a, acc)
        return jnp.sum(acc, axis=0, keepdims=True)

    def group_max(j, gm):
        r0 = pl.multiple_of(j * ksel, ksel)
        return jnp.maximum(gm, sc_ref[pl.ds(r0, ksel), :])

    gm = lax.fori_loop(0, nhalf * (TK_TAIL // ksel), group_max, jnp.full((ksel, TQ), -jnp.inf, F32))
    kf = float(ksel)

    def ult(a, b):
        return (a ^ INT_MIN) < (b ^ INT_MIN)

    def is_open(lo, hi):
        return ult(jnp.int32(1), hi - lo)

    def not_settled(cnt):
        return jnp.where(cnt == kf, 0.0, 1.0)

    def search_cond(st):
        _, lo, hi, cnt = st
        return jnp.max(jnp.where(is_open(lo, hi), not_settled(cnt), 0.0)) > 0.0

    def search_step(_, st):
        it, lo, hi, cnt = st
        mid_u = lo + lax.shift_right_logical(hi - lo, 1)
        mid_v = float_to_u(0.5 * (u_to_float(lo) + u_to_float(hi)))
        use_v = jnp.logical_and(it < VALUE_SPLITS, jnp.logical_and(ult(lo, mid_v), ult(mid_v, hi)))
        cand = jnp.where(is_open(lo, hi), jnp.where(use_v, mid_v, mid_u), lo)
        cand_f = u_to_float(cand)
        c = count(lambda s, pos: s >= cand_f)
        take = c >= kf
        return it + 1, jnp.where(take, cand, lo), jnp.where(take, hi, cand), jnp.where(take, c, cnt)

    def search_body(st):
        return lax.fori_loop(0, SETTLE_EVERY, search_step, st)

    _, u_thr, _, cnt_thr = lax.while_loop(
        search_cond, search_body,
        (jnp.int32(0), float_to_u(jnp.min(gm, axis=0, keepdims=True)),
         float_to_u(jnp.max(gm, axis=0, keepdims=True)) + 1, jnp.full((1, TQ), -1.0, F32)))
    thr = u_to_float(u_thr)
    real = (u_thr ^ INT_MIN) > KEY_NEGINF

    nbits = seq.bit_length()

    def resolve_ties():
        c_gt = count(lambda s, pos: s > thr)
        c_eq = count(lambda s, pos: s == thr)
        need = kf - c_gt
        excess = jnp.logical_and(c_eq > need, real)

        def find_last():
            def jbody(i, y):
                cand = y | lax.shift_left(jnp.int32(1), (nbits - 1) - i)
                below = count(lambda s, pos: jnp.logical_and(s == thr, pos < cand))
                return jnp.where(below <= need - 1.0, cand, y)
            return lax.fori_loop(0, nbits, jbody, jnp.zeros((1, TQ), I32))

        return lax.cond(jnp.max(jnp.where(excess, 1.0, 0.0)) > 0.0, find_last,
                        lambda: jnp.full((1, TQ), seq, I32))

    last = lax.cond(jnp.max(not_settled(cnt_thr)) > 0.0, resolve_ties, lambda: jnp.full((1, TQ), seq, I32))
    last = jnp.where(real, last, -1)
    thr_m = jnp.where(real, thr, -jnp.inf)

    aq = aq_ref[...] * ((ATT_DH ** -0.5) * LOG2E)
    qgt = []
    for g in range(ATT_KV_HEADS):
        blk = [aq[:, (g * ATT_GROUP + j) * ATT_DH:(g * ATT_GROUP + j + 1) * ATT_DH].T for j in range(ATT_GROUP)]
        qgt.append(jnp.concatenate(blk, axis=1).astype(BF16))

    m_ref[...] = jnp.full(m_ref.shape, NEG_BIG, F32)
    acc_ref[...] = jnp.zeros(acc_ref.shape, F32)

    def att_body(k0, rows):
        sc = sc_ref[pl.ds(k0, rows), :]
        sel = jnp.logical_or(sc > thr_m, jnp.logical_and(sc == thr_m, k0 + krow[:rows] <= last))
        bias = jnp.where(sel, 0.0, NEG_BIG)
        bias = jnp.concatenate([bias] * ATT_GROUP, axis=1)
        heads = range(ATT_KV_HEADS)
        ss = [jnp.dot(k_ref[pl.ds(k0, rows), g * ATT_DH:(g + 1) * ATT_DH], qgt[g],
                      preferred_element_type=F32) + bias for g in heads]
        ps, alphas = [], []
        for g in heads:
            m_old = m_ref[g]
            m_new = jnp.maximum(m_old, jnp.max(ss[g], axis=0, keepdims=True))
            m_ref[g] = m_new
            alphas.append(jnp.exp2(m_old - m_new))
            ps.append(jnp.exp2(ss[g] - m_new).astype(BF16))
        pvs = [jnp.dot(vt_ref[g, :, pl.ds(k0, rows)], ps[g], preferred_element_type=F32) for g in heads]
        for g in heads:
            acc_ref[g] = alphas[g] * acc_ref[g] + pvs[g]

    over_tiles(att_body)

    gout = gout_ref[...]
    for g in range(ATT_KV_HEADS):
        a = acc_ref[g]
        o = a[:ATT_DH] / a[ATT_DH:ATT_DH + 1]
        on = o * lax.rsqrt(jnp.mean(o * o, axis=0, keepdims=True) + EPS)
        for j in range(ATT_GROUP):
            hsl = slice((g * ATT_GROUP + j) * ATT_DH, (g * ATT_GROUP + j + 1) * ATT_DH)
            o_ref[:, hsl] = (on[:, j * TQ:(j + 1) * TQ].T * gout[:, hsl]).astype(o_ref.dtype)


def _dsa(proj, kb, vt, kxb, g_out, bsz, seq):
    n = proj.shape[0]
    nqb = seq // TQ
    ksel = min(TOPK_MAX, seq // 4)
    assert TK_TAIL % ksel == 0 and TK_TAIL % CNT_BLOCK == 0 and seq % TK == 0
    kb3 = kb.reshape(bsz, seq, 256)
    kx3 = kxb.reshape(bsz, seq, IDX_DIM)
    gq = ATT_GROUP * TQ
    return pl.pallas_call(
        functools.partial(_dsa_kernel, ksel=ksel, seq=seq),
        out_shape=jax.ShapeDtypeStruct((n, ATT_HEADS * ATT_DH), BF16),
        grid=(bsz, nqb),
        in_specs=[pl.BlockSpec((TQ, 512), lambda b, q: (b * nqb + q, OFF_IQ // 512)),
                  pl.BlockSpec((TQ, LANES), lambda b, q: (b * nqb + q, OFF_IK // LANES)),
                  pl.BlockSpec((TQ, 1024), lambda b, q: (b * nqb + q, OFF_AQ // 1024)),
                  pl.BlockSpec((None, seq, IDX_DIM), lambda b, q: (b, 0, 0)),
                  pl.BlockSpec((None, seq, 256), lambda b, q: (b, 0, 0)),
                  pl.BlockSpec((ATT_KV_HEADS, VT_ROWS, seq), lambda b, q: (0, 0, b)),
                  pl.BlockSpec((1, ATT_HEADS * ATT_DH), lambda b, q: (0, 0))],
        out_specs=pl.BlockSpec((TQ, ATT_HEADS * ATT_DH), lambda b, q: (b * nqb + q, 0)),
        scratch_shapes=[pltpu.VMEM((seq, TQ), F32),
                        pltpu.VMEM((ATT_KV_HEADS, 1, gq), F32),
                        pltpu.VMEM((ATT_KV_HEADS, VT_ROWS, gq), F32)],
        compiler_params=pltpu.CompilerParams(dimension_semantics=("arbitrary", "arbitrary"),
                                             vmem_limit_bytes=VMEM_LIMIT),
        name="dsa",
    )(proj, proj, proj, kx3, kb3, vt, g_out.reshape(1, -1))


OUT_RB = 128


def _outproj_kernel(rec_ref, att_ref, x_ref, gt_ref, sc_ref, sh_ref, g_ref, wo_ref, wrh_ref, wrl_ref, br_ref,
                    x1_ref, h2_ref, oh0_ref, oh1_ref, wts_ref):
    wo = wo_ref[...]
    half = rec_ref.shape[1]
    tm = x_ref.shape[0]
    blocks = [slice(r, r + OUT_RB) for r in range(0, tm, OUT_RB)]
    mixed = [jnp.dot(rec_ref[rs, :], wo[:half], preferred_element_type=F32)
             + jnp.dot(att_ref[rs, :], wo[half:], preferred_element_type=F32) for rs in blocks]
    his, los = [], []
    for rs, mx in zip(blocks, mixed):
        x1 = x_ref[rs, :] + gt_ref[0] * mx
        x1_ref[rs, :] = x1
        h2 = _norm_mod(x1, g_ref[...], sc_ref[0], sh_ref[0])
        h2_ref[rs, :] = h2
        hi = h2.astype(BF16)
        his.append(hi)
        los.append((h2 - hi.astype(F32)).astype(BF16))
    wrh, wrl = wrh_ref[...], wrl_ref[...]
    logits = jnp.concatenate(
        [jnp.dot(hi, wrh, preferred_element_type=F32) + jnp.dot(lo, wrh, preferred_element_type=F32)
         + jnp.dot(hi, wrl, preferred_element_type=F32) for hi, lo in zip(his, los)], axis=0) + br_ref[...]
    lane = lax.broadcasted_iota(I32, (tm, LANES), 1)
    big = jnp.int32(LANES)

    def argmax_first(vals, mask):
        mv = jnp.where(mask, vals, -jnp.inf)
        top = jnp.max(mv, axis=1, keepdims=True)
        idx = jnp.min(jnp.where(jnp.logical_and(mask, mv == top), lane, big), axis=1, keepdims=True)
        return top, idx

    gmask = lane < N_GROUPS
    gtop, gidx = argmax_first(logits, gmask)
    p_g = 1.0 / jnp.sum(jnp.where(gmask, jnp.exp(logits - gtop), 0.0), axis=1, keepdims=True)
    e_lo = N_GROUPS + gidx * EXPERTS_PER_GROUP
    emask = jnp.logical_and(lane >= e_lo, lane < e_lo + EXPERTS_PER_GROUP)
    v1, i1 = argmax_first(logits, emask)
    v2, i2 = argmax_first(logits, jnp.logical_and(emask, lane != i1))
    r = jnp.exp(v2 - v1)
    w1 = p_g / (1.0 + r)
    w2 = p_g * r / (1.0 + r)
    oh0_ref[...] = jnp.where(lane + N_GROUPS == i1, 1.0, 0.0).astype(BF16)
    oh1_ref[...] = jnp.where(lane + N_GROUPS == i2, 1.0, 0.0).astype(BF16)
    wts_ref[...] = jnp.where(lane == 0, w1, jnp.where(lane == 1, w2, 0.0))


def _outproj(rec, att, x2, gt, sc, sh, g, wo_bf, wr, br, seq):
    n, d = x2.shape
    wr_hi = wr.astype(BF16)
    wr_lo = (wr - wr_hi.astype(F32)).astype(BF16)
    tm = min(512, seq)
    per_b = seq // tm
    half = rec.shape[1]
    bspec = pl.BlockSpec((1, 1, d), lambda i: (i // per_b, 0, 0))
    return pl.pallas_call(
        _outproj_kernel,
        out_shape=(jax.ShapeDtypeStruct((n, d), F32),
                   jax.ShapeDtypeStruct((n, d), F32),
                   jax.ShapeDtypeStruct((n, LANES), BF16),
                   jax.ShapeDtypeStruct((n, LANES), BF16),
                   jax.ShapeDtypeStruct((n, LANES), F32)),
        grid=(n // tm,),
        in_specs=[pl.BlockSpec((tm, half), lambda i: (i, 0)),
                  pl.BlockSpec((tm, half), lambda i: (i, 0)),
                  pl.BlockSpec((tm, d), lambda i: (i, 0)),
                  bspec, bspec, bspec,
                  pl.BlockSpec((1, d), lambda i: (0, 0)),
                  pl.BlockSpec((2 * half, d), lambda i: (0, 0)),
                  pl.BlockSpec((d, LANES), lambda i: (0, 0)),
                  pl.BlockSpec((d, LANES), lambda i: (0, 0)),
                  pl.BlockSpec((1, LANES), lambda i: (0, 0))],
        out_specs=(pl.BlockSpec((tm, d), lambda i: (i, 0)),
                   pl.BlockSpec((tm, d), lambda i: (i, 0)),
                   pl.BlockSpec((tm, LANES), lambda i: (i, 0)),
                   pl.BlockSpec((tm, LANES), lambda i: (i, 0)),
                   pl.BlockSpec((tm, LANES), lambda i: (i, 0))),
        compiler_params=pltpu.CompilerParams(dimension_semantics=("arbitrary",),
                                             vmem_limit_bytes=VMEM_LIMIT),
        name="outproj",
    )(rec, att, x2, gt, sc, sh, g, wo_bf, wr_hi, wr_lo, br)


MOE_TM = 512


def _moe_kernel(te_ref, nt_ref, hs_ref, wg_ref, wu_ref, wd_ref, o_ref):
    t = pl.program_id(0)

    @pl.when(t < nt_ref[0])
    def _():
        xs = hs_ref[...].astype(BF16)
        gte = jnp.dot(xs, wg_ref[...].astype(BF16), preferred_element_type=F32)
        up = jnp.dot(xs, wu_ref[...].astype(BF16), preferred_element_type=F32)
        act = (_silu(gte) * up).astype(BF16)
        o_ref[...] = jnp.dot(act, wd_ref[...].astype(BF16), preferred_element_type=F32)

    @pl.when(t >= nt_ref[0])
    def _():
        o_ref[...] = jnp.zeros_like(o_ref)


def _moe(tile_expert, n_tiles, hs, wg, wu, wd):
    p = hs.shape[0]
    d, de = wg.shape[1], wg.shape[2]
    tm = MOE_TM
    grid_spec = pltpu.PrefetchScalarGridSpec(
        num_scalar_prefetch=2,
        grid=(p // tm,),
        in_specs=[pl.BlockSpec((tm, d), lambda t, te, nt: (jnp.minimum(t, nt[0] - 1), 0)),
                  pl.BlockSpec((None, d, de), lambda t, te, nt: (te[t], 0, 0)),
                  pl.BlockSpec((None, d, de), lambda t, te, nt: (te[t], 0, 0)),
                  pl.BlockSpec((None, de, d), lambda t, te, nt: (te[t], 0, 0))],
        out_specs=pl.BlockSpec((tm, d), lambda t, te, nt: (t, 0)),
    )
    return pl.pallas_call(
        _moe_kernel,
        out_shape=jax.ShapeDtypeStruct((p, d), F32),
        grid_spec=grid_spec,
        compiler_params=pltpu.CompilerParams(dimension_semantics=("arbitrary",),
                                             vmem_limit_bytes=56 * 1024 * 1024),
        name="moe",
    )(tile_expert, n_tiles, hs, wg, wu, wd)


FIN_TM = 256
DMA_UNROLL = 8


def _final_kernel(dcur_ref, dnext_ref, x1_ref, wts_ref, gt_ref, g_ref, ys_ref, o_ref, buf, sem):
    i = pl.program_id(0)
    n = pl.num_programs(0)
    tm = x1_ref.shape[0]

    def row_copy(dref, slot, r, s):
        return pltpu.make_async_copy(ys_ref.at[pl.ds(dref[0, 0, 2 * r + s], 1)],
                                     buf.at[slot, s, pl.ds(r, 1)], sem.at[slot])

    def issue(dref, slot):
        def body(r, c):
            row_copy(dref, slot, r, 0).start()
            row_copy(dref, slot, r, 1).start()
            return c
        lax.fori_loop(0, tm, body, 0, unroll=DMA_UNROLL)

    @pl.when(i == 0)
    def _():
        issue(dcur_ref, 0)

    @pl.when(i + 1 < n)
    def _():
        issue(dnext_ref, (i + 1) % 2)

    slot = i % 2
    for s in range(2):
        pltpu.make_async_copy(ys_ref.at[pl.ds(0, tm)], buf.at[slot, s], sem.at[slot]).wait()
    w = wts_ref[...]
    y = w[:, 0:1] * buf[slot, 0] + w[:, 1:2] * buf[slot, 1]
    xo = x1_ref[...] + gt_ref[0] * y
    o_ref[...] = xo * lax.rsqrt(jnp.mean(xo * xo, axis=-1, keepdims=True) + EPS) * g_ref[...]


def _final(dest3, x1, wts, gt, g, ys, seq):
    n, d = x1.shape
    tm = FIN_TM
    per_b = seq // tm
    steps = n // tm
    row = pl.BlockSpec((tm, d), lambda i: (i, 0))
    smem = lambda f: pl.BlockSpec((1, 1, 2 * tm), f, memory_space=pltpu.SMEM)
    return pl.pallas_call(
        _final_kernel,
        out_shape=jax.ShapeDtypeStruct((n, d), F32),
        grid=(steps,),
        in_specs=[smem(lambda i: (i, 0, 0)),
                  smem(lambda i: (jnp.minimum(i + 1, steps - 1), 0, 0)),
                  row,
                  pl.BlockSpec((tm, LANES), lambda i: (i, 0)),
                  pl.BlockSpec((1, 1, d), lambda i: (i // per_b, 0, 0)),
                  pl.BlockSpec((1, d), lambda i: (0, 0)),
                  pl.BlockSpec(memory_space=pl.ANY)],
        out_specs=row,
        scratch_shapes=[pltpu.VMEM((2, 2, tm, d), F32), pltpu.SemaphoreType.DMA((2,))],
        compiler_params=pltpu.CompilerParams(dimension_semantics=("arbitrary",),
                                             vmem_limit_bytes=VMEM_LIMIT),
        name="final",
    )(dest3, dest3, x1, wts, gt, g, ys)


ROUTE_T = 512


def _route_kernel(oh0_ref, oh1_ref, dest_ref, cnt_ref, run_ref, tot_ref):
    ph = pl.program_id(0)
    i = pl.program_id(1)
    a0 = oh0_ref[...]
    a1 = oh1_ref[...]
    both = a0 + a1
    colsum = jnp.sum(both.astype(F32), axis=0, keepdims=True)

    @pl.when(jnp.logical_and(ph == 0, i == 0))
    def _():
        tot_ref[...] = jnp.zeros_like(tot_ref)

    @pl.when(ph == 0)
    def _():
        tot_ref[...] = tot_ref[...] + colsum

    @pl.when(ph == 1)
    def _():
        @pl.when(i == 0)
        def _():
            run_ref[...] = jnp.zeros_like(run_ref)

        tot = tot_ref[...]
        tiles = jnp.ceil(tot * (1.0 / MOE_TM))
        rr = lax.broadcasted_iota(I32, (LANES, LANES), 0)
        cc = lax.broadcasted_iota(I32, (LANES, LANES), 1)
        before = (rr < cc).astype(BF16)
        tiles8 = jnp.broadcast_to(tiles, (8, LANES)).astype(BF16)
        poff = jnp.dot(tiles8, before, preferred_element_type=F32)[0:1] * float(MOE_TM)
        t = a0.shape[0]
        r2 = lax.broadcasted_iota(I32, (t, t), 0)
        c2 = lax.broadcasted_iota(I32, (t, t), 1)
        earlier = (c2 < r2).astype(BF16)
        rank = jnp.dot(earlier, both, preferred_element_type=F32)
        tgt = poff + run_ref[...] + rank
        d0 = jnp.sum(a0.astype(F32) * tgt, axis=1, keepdims=True)
        d1 = jnp.sum(a1.astype(F32) * tgt, axis=1, keepdims=True)
        lane = lax.broadcasted_iota(I32, (t, LANES), 1)
        dest_ref[...] = jnp.where(lane == 0, d0, jnp.where(lane == 1, d1, 0.0)).astype(I32)
        run_ref[...] = run_ref[...] + colsum
        cnt_ref[...] = tot


def _route(oh0, oh1):
    n = oh0.shape[0]
    t = ROUTE_T
    blk = pl.BlockSpec((t, LANES), lambda ph, i: (i, 0))
    return pl.pallas_call(
        _route_kernel,
        out_shape=(jax.ShapeDtypeStruct((n, LANES), I32), jax.ShapeDtypeStruct((1, LANES), F32)),
        grid=(2, n // t),
        in_specs=[blk, blk],
        out_specs=(pl.BlockSpec((t, LANES), lambda ph, i: (i * ph, 0)),
                   pl.BlockSpec((1, LANES), lambda ph, i: (0, 0))),
        scratch_shapes=[pltpu.VMEM((1, LANES), F32), pltpu.VMEM((1, LANES), F32)],
        compiler_params=pltpu.CompilerParams(dimension_semantics=("arbitrary", "arbitrary"),
                                             vmem_limit_bytes=VMEM_LIMIT),
        name="route",
    )(oh0, oh1)


DISP_TM = 1024


def _dispatch_kernel(pad_ref, dest_ref, h_ref, hs_ref, zero_ref, sem, zsem):
    tm = h_ref.shape[0]

    @pl.when(pl.program_id(0) == 0)
    def _():
        zero_ref[...] = jnp.zeros_like(zero_ref)

        def zero_copy(row0):
            return pltpu.make_async_copy(zero_ref, hs_ref.at[pl.ds(pl.multiple_of(row0, MOE_TM), MOE_TM)], zsem)

        def fill(e, c):
            @pl.when(pad_ref[e] >= 0)
            def _():
                zero_copy(pad_ref[e]).start()
            return c

        def drain(e, c):
            @pl.when(pad_ref[e] >= 0)
            def _():
                zero_copy(pad_ref[e]).wait()
            return c

        lax.fori_loop(0, N_EXPERTS, fill, 0)
        lax.fori_loop(0, N_EXPERTS, drain, 0)
        used = pad_ref[N_EXPERTS]
        total = hs_ref.shape[0] // MOE_TM
        lax.fori_loop(used, total, lambda t, c: (zero_copy(t * MOE_TM).start(), c)[1], 0)
        lax.fori_loop(used, total, lambda t, c: (zero_copy(t * MOE_TM).wait(), c)[1], 0)

    def body(r, c):
        for s in range(2):
            pltpu.make_async_copy(h_ref.at[pl.ds(r, 1)], hs_ref.at[pl.ds(dest_ref[0, 0, 2 * r + s], 1)],
                                  sem).start()
        return c

    lax.fori_loop(0, tm, body, 0, unroll=DMA_UNROLL)
    for _ in range(2):
        pltpu.make_async_copy(h_ref, hs_ref.at[pl.ds(0, tm)], sem).wait()


def _dispatch(pad_start, dest3, h2, p_rows):
    n, w = h2.shape
    tm = DISP_TM
    grid_spec = pltpu.PrefetchScalarGridSpec(
        num_scalar_prefetch=1,
        grid=(n // tm,),
        in_specs=[pl.BlockSpec((1, 1, 2 * tm), lambda i, pad: (i, 0, 0), memory_space=pltpu.SMEM),
                  pl.BlockSpec((tm, w), lambda i, pad: (i, 0))],
        out_specs=pl.BlockSpec(memory_space=pl.ANY),
        scratch_shapes=[pltpu.VMEM((MOE_TM, w), h2.dtype), pltpu.SemaphoreType.DMA(()),
                        pltpu.SemaphoreType.DMA(())],
    )
    return pl.pallas_call(
        _dispatch_kernel,
        out_shape=jax.ShapeDtypeStruct((p_rows, w), h2.dtype),
        grid_spec=grid_spec,
        compiler_params=pltpu.CompilerParams(dimension_semantics=("arbitrary",),
                                             vmem_limit_bytes=VMEM_LIMIT),
        name="dispatch",
    )(pad_start, dest3, h2)


def _tile_tables(counts, n_tiles_max):
    cnt = counts[0, :N_EXPERTS].astype(I32)
    tiles = (cnt + MOE_TM - 1) // MOE_TM
    tiles_end = jnp.cumsum(tiles)
    tile = jnp.arange(n_tiles_max, dtype=I32)
    tile_expert = jnp.sum((tile[:, None] >= tiles_end[None, :]).astype(I32), axis=1)
    n_used = tiles_end[-1:].astype(I32)
    pad_start = jnp.where(tiles > 0, (tiles_end - 1) * MOE_TM, -1).astype(I32)
    return jnp.minimum(tile_expert, N_EXPERTS - 1), n_used, jnp.concatenate([pad_start, n_used])


def kernel(x, c, w_ada, b_ada, g_norm_mix, w_in, lb_logits, g_rec_out, g_att_out, w_out, g_norm_ffn,
           w_router_group, b_router_group, w_router_expert, b_router_expert,
           w_expert_gate, w_expert_up, w_expert_down, g_final):
    bsz, seq, d = x.shape
    n = bsz * seq
    assert w_ada.shape[0] == 1, "single trunk layer"
    layer = 0
    x2 = x.reshape(n, d)

    mod = _adaln(c, w_ada[layer], b_ada[layer])
    sh1, sc1, gt1, sh2, sc2, gt2 = [m.reshape(bsz, 1, d) for m in jnp.split(mod, 6, axis=-1)]

    w_in_bf = jnp.pad(w_in[layer], ((0, 0), (0, IN_PAD - IN_COLS))).astype(BF16)
    proj = _inproj(x2, sc1, sh1, g_norm_mix[layer].reshape(1, d), w_in_bf, seq)
    kb, vb, kxb = _kvprep(proj)
    rec = _hgrn(proj, lb_logits, g_rec_out[layer], bsz, seq, layer)
    att = _dsa(proj, kb, vb, kxb, g_att_out[layer], bsz, seq)

    wr = jnp.concatenate([w_router_group[layer], w_router_expert[layer]], axis=1)
    wr = jnp.pad(wr, ((0, 0), (0, LANES - wr.shape[1])))
    br = jnp.concatenate([b_router_group[layer], b_router_expert[layer]])
    br = jnp.pad(br, (0, LANES - br.shape[0])).reshape(1, LANES)
    x1, h2, oh0, oh1, wts = _outproj(rec, att, x2, gt1, sc2, sh2, g_norm_ffn[layer].reshape(1, d),
                                      w_out[layer].astype(BF16), wr, br, seq)

    dest, counts = _route(oh0, oh1)
    dest2 = dest[:, :2]
    p_rows = 2 * n + N_EXPERTS * MOE_TM
    tile_expert, n_tiles, pad_start = _tile_tables(counts, p_rows // MOE_TM)
    hs = _dispatch(pad_start, dest2.reshape(n // DISP_TM, 1, 2 * DISP_TM), h2, p_rows)
    ys = _moe(tile_expert, n_tiles, hs, w_expert_gate[layer], w_expert_up[layer], w_expert_down[layer])
    out = _final(dest2.reshape(n // FIN_TM, 1, 2 * FIN_TM), x1, wts, gt2, g_final.reshape(1, d), ys, seq)
    return out.reshape(bsz, seq, d)
```

```python
import functools

import jax
import jax.numpy as jnp
import numpy as np
from jax import lax
from jax.experimental import pallas as pl
from jax.experimental.pallas import tpu as pltpu

F32 = jnp.float32
BF16 = jnp.bfloat16
I32 = jnp.int32

EPS = 1e-6
LANES = 128

REC_HEADS = 8
REC_D = 128
REC_CHUNK = 64
REC_SUB = 16
ATT_HEADS = 8
ATT_DH = 128
ATT_KV_HEADS = 2
ATT_GROUP = ATT_HEADS // ATT_KV_HEADS
IDX_HEADS = 8
IDX_DIM = 64
TOPK_MAX = 256
N_GROUPS = 4
EXPERTS_PER_GROUP = 8
N_EXPERTS = N_GROUPS * EXPERTS_PER_GROUP

OFF_RQ = 0
OFF_RF = 1024
OFF_RI = 2048
OFF_RG = 3072
OFF_AQ = 4096
OFF_AK = 5120
OFF_IQ = 5632
OFF_IK = 6144
IN_COLS = 6216
IN_PAD = 6272

VMEM_LIMIT = 48 * 1024 * 1024
VMEM_LIMIT_MOE = 56 * 1024 * 1024

INT_MIN = -(2 ** 31)
KEY_NEGINF = int(np.array(-np.inf, np.float32).view(np.int32)) ^ 0x7FFFFFFF
NEG_BIG = -1e30


def _silu(v):
    return v * jax.nn.sigmoid(v)


def _nt_dot(a, b):
    return lax.dot_general(a, b, (((1,), (1,)), ((), ())), preferred_element_type=F32)


def _tn_dot(a, b):
    return lax.dot_general(a, b, (((0,), (0,)), ((), ())), preferred_element_type=F32)


def _adaln_kernel(c_ref, w_ref, b_ref, o_ref):
    ca = _silu(c_ref[...])
    o_ref[...] = jnp.dot(ca, w_ref[...], preferred_element_type=F32,
                         precision=lax.Precision.HIGHEST) + b_ref[...]


def _adaln(c, w, b):
    bsz, d = c.shape
    n = w.shape[1]
    tn = 512
    return pl.pallas_call(
        _adaln_kernel,
        out_shape=jax.ShapeDtypeStruct((bsz, n), F32),
        grid=(n // tn,),
        in_specs=[pl.BlockSpec((bsz, d), lambda j: (0, 0)),
                  pl.BlockSpec((d, tn), lambda j: (0, j)),
                  pl.BlockSpec((1, tn), lambda j: (0, j))],
        out_specs=pl.BlockSpec((bsz, tn), lambda j: (0, j)),
        compiler_params=pltpu.CompilerParams(dimension_semantics=("arbitrary",),
                                             vmem_limit_bytes=VMEM_LIMIT),
        name="adaln",
    )(c, w, b.reshape(1, n))


def _norm_mod(x, g, sc, sh):
    xn = x * lax.rsqrt(jnp.mean(x * x, axis=-1, keepdims=True) + EPS)
    return xn * g * (1.0 + sc) + sh


IN_RB = 256


def _inproj_kernel(x_ref, sc_ref, sh_ref, g_ref, w_ref, o_ref, h_ref):
    @pl.when(pl.program_id(1) == 0)
    def _():
        tm = x_ref.shape[0]
        rb = min(IN_RB, tm)
        hs = []
        for r in range(0, tm, rb):
            h = _norm_mod(x_ref[r:r + rb, :], g_ref[...], sc_ref[0], sh_ref[0]).astype(BF16)
            h_ref[r:r + rb, :] = h
            hs.append(h)
        w = w_ref[...]
        for i, r in enumerate(range(0, tm, rb)):
            o_ref[r:r + rb, :] = jnp.dot(hs[i], w, preferred_element_type=F32)

    @pl.when(pl.program_id(1) != 0)
    def _():
        o_ref[...] = jnp.dot(h_ref[...], w_ref[...], preferred_element_type=F32)


def _inproj(x2, sc, sh, g, w_bf, seq):
    n, d = x2.shape
    ncol = w_bf.shape[1]
    tm = min(1024, seq)
    tn = 896
    per_b = seq // tm
    return pl.pallas_call(
        _inproj_kernel,
        out_shape=jax.ShapeDtypeStruct((n, ncol), F32),
        grid=(n // tm, ncol // tn),
        in_specs=[pl.BlockSpec((tm, d), lambda i, j: (i, 0)),
                  pl.BlockSpec((1, 1, d), lambda i, j: (i // per_b, 0, 0)),
                  pl.BlockSpec((1, 1, d), lambda i, j: (i // per_b, 0, 0)),
                  pl.BlockSpec((1, d), lambda i, j: (0, 0)),
                  pl.BlockSpec((d, tn), lambda i, j: (0, j))],
        out_specs=pl.BlockSpec((tm, tn), lambda i, j: (i, j)),
        scratch_shapes=[pltpu.VMEM((tm, d), BF16)],
        compiler_params=pltpu.CompilerParams(dimension_semantics=("arbitrary", "arbitrary"),
                                             vmem_limit_bytes=VMEM_LIMIT),
        name="inproj",
    )(x2, sc, sh, g, w_bf)


VT_ROWS = ATT_DH + 16


def _kvprep_kernel(kv_ref, ik_ref, k_ref, vt_ref, kx_ref):
    kv = kv_ref[...]
    tm = kv.shape[0]
    k_ref[...] = kv[:, :256].astype(BF16)
    tail = jnp.where(lax.broadcasted_iota(I32, (16, tm), 0) == 0, 1.0, 0.0)
    for g in range(ATT_KV_HEADS):
        vt = kv[:, 256 + g * ATT_DH:256 + (g + 1) * ATT_DH].T
        vt_ref[g] = jnp.concatenate([vt, tail], axis=0).astype(BF16)
    kx_ref[...] = ik_ref[...][:, :IDX_DIM].astype(BF16)


def _kvprep(proj):
    n = proj.shape[0]
    tm = 512
    return pl.pallas_call(
        _kvprep_kernel,
        out_shape=(jax.ShapeDtypeStruct((n, 256), BF16),
                   jax.ShapeDtypeStruct((ATT_KV_HEADS, VT_ROWS, n), BF16),
                   jax.ShapeDtypeStruct((n, IDX_DIM), BF16)),
        grid=(n // tm,),
        in_specs=[pl.BlockSpec((tm, 512), lambda i: (i, OFF_AK // 512)),
                  pl.BlockSpec((tm, LANES), lambda i: (i, OFF_IK // LANES))],
        out_specs=(pl.BlockSpec((tm, 256), lambda i: (i, 0)),
                   pl.BlockSpec((ATT_KV_HEADS, VT_ROWS, tm), lambda i: (0, 0, i)),
                   pl.BlockSpec((tm, IDX_DIM), lambda i: (i, 0))),
        compiler_params=pltpu.CompilerParams(dimension_semantics=("arbitrary",),
                                             vmem_limit_bytes=VMEM_LIMIT),
        name="kvprep",
    )(proj, proj)


def _hgrn_kernel(q_ref, f_ref, i_ref, g_ref, lbl_ref, gout_ref, o_ref, st_ref, *, chunks, layer):
    @pl.when(pl.program_id(2) == 0)
    def _():
        st_ref[...] = jnp.zeros_like(st_ref)

    lbl = lbl_ref[...]
    e = jnp.exp(lbl - jnp.max(lbl, axis=0, keepdims=True))
    sm = e / jnp.sum(e, axis=0, keepdims=True)
    lb = jnp.sum(sm[: layer + 1], axis=0, keepdims=True)
    gout = gout_ref[...]

    c = REC_CHUNK
    nsub = c // REC_SUB
    rr = lax.broadcasted_iota(I32, (c, c), 0)
    cc = lax.broadcasted_iota(I32, (c, c), 1)
    row = lax.broadcasted_iota(I32, (c, REC_D), 0)
    sub = lax.shift_right_logical(row, REC_SUB.bit_length() - 1)

    f = lb + (1.0 - lb) * jax.nn.sigmoid(f_ref[...])
    logf = jnp.log2(f)
    k = 1.0 - f
    qf = _silu(q_ref[...]) * (REC_D ** -0.5)
    vb = i_ref[...].astype(BF16)

    halves = [1 << l for l in range(REC_SUB.bit_length() - 1)]
    logf_w = jnp.concatenate([logf[ci * c:(ci + 1) * c] for ci in range(chunks)], axis=1)
    b_all = jnp.dot((rr >= cc).astype(F32), logf_w, preferred_element_type=F32,
                    precision=lax.Precision.HIGHEST)
    row_w = lax.broadcasted_iota(I32, b_all.shape, 0)
    blk_end, split = b_all, {}
    for h in halves:
        first = (row_w & (2 * h - 1)) < h
        split[h] = jnp.where(first, blk_end, pltpu.roll(blk_end, h, 0))
        blk_end = jnp.where(first, pltpu.roll(blk_end, c - h, 0), blk_end)
    lvl_mask = [jnp.logical_and((rr ^ cc) < 2 * h, jnp.logical_and((rr & h) != 0, (cc & h) == 0)) for h in halves]
    eye = rr == cc
    in_second = {h: (row & h) != 0 for h in halves}
    after_sub = [sub > i for i in range(nsub - 1)]
    in_sub = [sub == i for i in range(nsub - 1)]

    qxs, kxs, qls, kls, diag, upd, q_in, decay = [], [], [], [], [], [], [], []
    for ci in range(chunks):
        sl = slice(ci * c, (ci + 1) * c)
        cols = slice(ci * REC_D, (ci + 1) * REC_D)
        b = b_all[:, cols]
        kc, qc = k[sl], qf[sl]
        qparts, kparts = [], []
        for i in range(nsub - 1):
            r = b[(i + 1) * REC_SUB - 1:(i + 1) * REC_SUB, :]
            qparts.append(jnp.where(after_sub[i], qc * jnp.exp2(b - r), 0.0))
            kparts.append(jnp.where(in_sub[i], kc * jnp.exp2(r - b), 0.0))
        qxs.append(jnp.concatenate(qparts, axis=1).astype(BF16))
        kxs.append(jnp.concatenate(kparts, axis=1).astype(BF16))
        for h in halves:
            e = jnp.exp2(-jnp.abs(b - split[h][:, cols]))
            qls.append(jnp.where(in_second[h], qc * e, 0.0).astype(BF16))
            kls.append(jnp.where(in_second[h], 0.0, kc * e).astype(BF16))
        diag.append(jnp.sum(qc * kc, axis=1, keepdims=True))
        b_end = b[c - 1:c, :]
        upd.append((kc * jnp.exp2(b_end - b)).astype(BF16))
        q_in.append((qc * jnp.exp2(b)).astype(BF16))
        decay.append(jnp.exp2(b_end))
    nl = len(halves)
    cross = [_nt_dot(qxs[ci], kxs[ci]) for ci in range(chunks)]
    within = [[_nt_dot(qls[ci * nl + l], kls[ci * nl + l]) for l in range(nl)] for ci in range(chunks)]
    scores = []
    for ci in range(chunks):
        s = cross[ci] + jnp.where(eye, diag[ci], 0.0)
        for l in range(nl):
            s = s + jnp.where(lvl_mask[l], within[ci][l], 0.0)
        scores.append(s.astype(BF16))
    upd = [_tn_dot(vb[ci * c:(ci + 1) * c], upd[ci]) for ci in range(chunks)]
    intra = [jnp.dot(scores[ci], vb[ci * c:(ci + 1) * c], preferred_element_type=F32) for ci in range(chunks)]

    st = st_ref[...]
    outs = []
    for ci in range(chunks):
        outs.append(_nt_dot(q_in[ci], st.astype(BF16)) + intra[ci])
        st = st * decay[ci] + upd[ci]
    st_ref[...] = st

    o = jnp.concatenate(outs, axis=0)
    on = o * lax.rsqrt(jnp.mean(o * o, axis=-1, keepdims=True) + EPS)
    o_ref[...] = (on * gout * _silu(g_ref[...])).astype(o_ref.dtype)


def _hgrn(proj, lb_logits, g_out, bsz, seq, layer):
    n = proj.shape[0]
    tc = min(2048, seq)
    per_b = seq // tc
    nl = lb_logits.shape[0]

    def col(off):
        return lambda b, h, c: (b * per_b + c, off // REC_D + h)

    return pl.pallas_call(
        functools.partial(_hgrn_kernel, chunks=tc // REC_CHUNK, layer=layer),
        out_shape=jax.ShapeDtypeStruct((n, REC_HEADS * REC_D), BF16),
        grid=(bsz, REC_HEADS, per_b),
        in_specs=[pl.BlockSpec((tc, REC_D), col(OFF_RQ)),
                  pl.BlockSpec((tc, REC_D), col(OFF_RF)),
                  pl.BlockSpec((tc, REC_D), col(OFF_RI)),
                  pl.BlockSpec((tc, REC_D), col(OFF_RG)),
                  pl.BlockSpec((nl, REC_D), lambda b, h, c: (0, h)),
                  pl.BlockSpec((1, REC_D), lambda b, h, c: (0, h))],
        out_specs=pl.BlockSpec((tc, REC_D), lambda b, h, c: (b * per_b + c, h)),
        scratch_shapes=[pltpu.VMEM((REC_D, REC_D), F32)],
        compiler_params=pltpu.CompilerParams(
            dimension_semantics=("arbitrary", "arbitrary", "arbitrary"),
            vmem_limit_bytes=VMEM_LIMIT),
        name="hgrn2",
    )(proj, proj, proj, proj, lb_logits, g_out.reshape(1, -1))


TQ = 128
TK = 1024
TK_TAIL = TK // 2
LOG2E = 1.4426950408889634
CNT_ROWS = 64
CNT_BLOCK = 256
SETTLE_EVERY = 4
VALUE_SPLITS = 4


def _dsa_kernel(qi_ref, iw_ref, aq_ref, kx_ref, k_ref, vt_ref, gout_ref, o_ref,
                sc_ref, m_ref, acc_ref, *, ksel, seq):
    t0 = pl.program_id(1) * TQ
    need = t0 + TQ
    nfull = (need + TK_TAIL - 1) // TK
    tail0 = pl.multiple_of(nfull * TK, TK)
    has_tail = need > tail0
    nhalf = 2 * nfull + has_tail.astype(I32)
    qpos = t0 + lax.broadcasted_iota(I32, (1, TQ), 1)
    krow = lax.broadcasted_iota(I32, (TK, TQ), 0)

    def over_tiles(body):
        lax.fori_loop(0, nfull, lambda kt, c: (body(pl.multiple_of(kt * TK, TK), TK), c)[1], 0)

        @pl.when(has_tail)
        def _():
            body(tail0, TK_TAIL)

    qit = (qi_ref[...] * (IDX_DIM ** -0.5)).T
    qht = jnp.concatenate([qit[h * IDX_DIM:(h + 1) * IDX_DIM] for h in range(IDX_HEADS)],
                          axis=1).astype(BF16)
    iwt = iw_ref[...].T
    wrow = [iwt[IDX_DIM + h:IDX_DIM + h + 1] * (IDX_HEADS ** -0.5) for h in range(IDX_HEADS)]

    def score_body(k0, rows):
        rel = jnp.dot(kx_ref[pl.ds(k0, rows), :], qht, preferred_element_type=F32)
        sc = jnp.zeros((rows, TQ), F32)
        for h in range(IDX_HEADS):
            sc = sc + wrow[h] * jnp.maximum(rel[:, h * TQ:(h + 1) * TQ], 0.0)
        sc_ref[pl.ds(k0, rows), :] = jnp.where(k0 + krow[:rows] <= qpos, sc, -jnp.inf)

    over_tiles(score_body)

    def u_to_float(u):
        key = u ^ INT_MIN
        return lax.bitcast_convert_type(key ^ ((key >> 31) & 0x7FFFFFFF), F32)

    def float_to_u(f):
        bits = lax.bitcast_convert_type(f, I32)
        return bits ^ ((bits >> 31) & 0x7FFFFFFF) ^ INT_MIN

    def count(pred):
        def block(k0, rows, acc):
            for u in range(0, rows, CNT_BLOCK):
                r0 = pl.multiple_of(k0 + u, CNT_BLOCK)
                hit = pred(sc_ref[pl.ds(r0, CNT_BLOCK), :], r0 + krow[:CNT_BLOCK])
                acc = acc + jnp.sum(jnp.where(hit, 1.0, 0.0).reshape(CNT_BLOCK // CNT_ROWS, CNT_ROWS, TQ), axis=0)
            return acc
        acc = lax.fori_loop(0, nfull, lambda kt, a: block(kt * TK, TK, a), jnp.zeros((CNT_ROWS, TQ), F32))
        acc = lax.cond(has_tail, lambda a: block(tail0, TK_TAIL, a), lambda a: a, acc)
        return jnp.sum(acc, axis=0, keepdims=True)

    def group_max(j, gm):
        r0 = pl.multiple_of(j * ksel, ksel)
        return jnp.maximum(gm, sc_ref[pl.ds(r0, ksel), :])

    gm = lax.fori_loop(0, nhalf * (TK_TAIL // ksel), group_max, jnp.full((ksel, TQ), -jnp.inf, F32))
    kf = float(ksel)

    def ult(a, b):
        return (a ^ INT_MIN) < (b ^ INT_MIN)

    def is_open(lo, hi):
        return ult(jnp.int32(1), hi - lo)

    def not_settled(cnt):
        return jnp.where(cnt == kf, 0.0, 1.0)

    def search_cond(st):
        _, lo, hi, cnt = st
        return jnp.max(jnp.where(is_open(lo, hi), not_settled(cnt), 0.0)) > 0.0

    def search_step(_, st):
        it, lo, hi, cnt = st
        mid_u = lo + lax.shift_right_logical(hi - lo, 1)
        mid_v = float_to_u(0.5 * (u_to_float(lo) + u_to_float(hi)))
        use_v = jnp.logical_and(it < VALUE_SPLITS, jnp.logical_and(ult(lo, mid_v), ult(mid_v, hi)))
        cand = jnp.where(is_open(lo, hi), jnp.where(use_v, mid_v, mid_u), lo)
        cand_f = u_to_float(cand)
        c = count(lambda s, pos: s >= cand_f)
        take = c >= kf
        return it + 1, jnp.where(take, cand, lo), jnp.where(take, hi, cand), jnp.where(take, c, cnt)

    def search_body(st):
        return lax.fori_loop(0, SETTLE_EVERY, search_step, st)

    _, u_thr, _, cnt_thr = lax.while_loop(
        search_cond, search_body,
        (jnp.int32(0), float_to_u(jnp.min(gm, axis=0, keepdims=True)),
         float_to_u(jnp.max(gm, axis=0, keepdims=True)) + 1, jnp.full((1, TQ), -1.0, F32)))
    thr = u_to_float(u_thr)
    real = (u_thr ^ INT_MIN) > KEY_NEGINF

    nbits = seq.bit_length()

    def resolve_ties():
        c_gt = count(lambda s, pos: s > thr)
        c_eq = count(lambda s, pos: s == thr)
        need = kf - c_gt
        excess = jnp.logical_and(c_eq > need, real)

        def find_last():
            def jbody(i, y):
                cand = y | lax.shift_left(jnp.int32(1), (nbits - 1) - i)
                below = count(lambda s, pos: jnp.logical_and(s == thr, pos < cand))
                return jnp.where(below <= need - 1.0, cand, y)
            return lax.fori_loop(0, nbits, jbody, jnp.zeros((1, TQ), I32))

        return lax.cond(jnp.max(jnp.where(excess, 1.0, 0.0)) > 0.0, find_last,
                        lambda: jnp.full((1, TQ), seq, I32))

    last = lax.cond(jnp.max(not_settled(cnt_thr)) > 0.0, resolve_ties, lambda: jnp.full((1, TQ), seq, I32))
    last = jnp.where(real, last, -1)
    thr_m = jnp.where(real, thr, -jnp.inf)

    aq = aq_ref[...] * ((ATT_DH ** -0.5) * LOG2E)
    qgt = []
    for g in range(ATT_KV_HEADS):
        blk = [aq[:, (g * ATT_GROUP + j) * ATT_DH:(g * ATT_GROUP + j + 1) * ATT_DH].T for j in range(ATT_GROUP)]
        qgt.append(jnp.concatenate(blk, axis=1).astype(BF16))

    m_ref[...] = jnp.full(m_ref.shape, NEG_BIG, F32)
    acc_ref[...] = jnp.zeros(acc_ref.shape, F32)

    def att_body(k0, rows):
        sc = sc_ref[pl.ds(k0, rows), :]
        sel = jnp.logical_or(sc > thr_m, jnp.logical_and(sc == thr_m, k0 + krow[:rows] <= last))
        bias = jnp.where(sel, 0.0, NEG_BIG)
        bias = jnp.concatenate([bias] * ATT_GROUP, axis=1)
        heads = range(ATT_KV_HEADS)
        ss = [jnp.dot(k_ref[pl.ds(k0, rows), g * ATT_DH:(g + 1) * ATT_DH], qgt[g],
                      preferred_element_type=F32) + bias for g in heads]
        ps, alphas = [], []
        for g in heads:
            m_old = m_ref[g]
            m_new = jnp.maximum(m_old, jnp.max(ss[g], axis=0, keepdims=True))
            m_ref[g] = m_new
            alphas.append(jnp.exp2(m_old - m_new))
            ps.append(jnp.exp2(ss[g] - m_new).astype(BF16))
        pvs = [jnp.dot(vt_ref[g, :, pl.ds(k0, rows)], ps[g], preferred_element_type=F32) for g in heads]
        for g in heads:
            acc_ref[g] = alphas[g] * acc_ref[g] + pvs[g]

    over_tiles(att_body)

    gout = gout_ref[...]
    for g in range(ATT_KV_HEADS):
        a = acc_ref[g]
        o = a[:ATT_DH] / a[ATT_DH:ATT_DH + 1]
        on = o * lax.rsqrt(jnp.mean(o * o, axis=0, keepdims=True) + EPS)
        for j in range(ATT_GROUP):
            hsl = slice((g * ATT_GROUP + j) * ATT_DH, (g * ATT_GROUP + j + 1) * ATT_DH)
            o_ref[:, hsl] = (on[:, j * TQ:(j + 1) * TQ].T * gout[:, hsl]).astype(o_ref.dtype)


def _dsa(proj, kb, vt, kxb, g_out, bsz, seq):
    n = proj.shape[0]
    nqb = seq // TQ
    ksel = min(TOPK_MAX, seq // 4)
    assert TK_TAIL % ksel == 0 and TK_TAIL % CNT_BLOCK == 0 and seq % TK == 0
    kb3 = kb.reshape(bsz, seq, 256)
    kx3 = kxb.reshape(bsz, seq, IDX_DIM)
    gq = ATT_GROUP * TQ
    return pl.pallas_call(
        functools.partial(_dsa_kernel, ksel=ksel, seq=seq),
        out_shape=jax.ShapeDtypeStruct((n, ATT_HEADS * ATT_DH), BF16),
        grid=(bsz, nqb),
        in_specs=[pl.BlockSpec((TQ, 512), lambda b, q: (b * nqb + q, OFF_IQ // 512)),
                  pl.BlockSpec((TQ, LANES), lambda b, q: (b * nqb + q, OFF_IK // LANES)),
                  pl.BlockSpec((TQ, 1024), lambda b, q: (b * nqb + q, OFF_AQ // 1024)),
                  pl.BlockSpec((None, seq, IDX_DIM), lambda b, q: (b, 0, 0)),
                  pl.BlockSpec((None, seq, 256), lambda b, q: (b, 0, 0)),
                  pl.BlockSpec((ATT_KV_HEADS, VT_ROWS, seq), lambda b, q: (0, 0, b)),
                  pl.BlockSpec((1, ATT_HEADS * ATT_DH), lambda b, q: (0, 0))],
        out_specs=pl.BlockSpec((TQ, ATT_HEADS * ATT_DH), lambda b, q: (b * nqb + q, 0)),
        scratch_shapes=[pltpu.VMEM((seq, TQ), F32),
                        pltpu.VMEM((ATT_KV_HEADS, 1, gq), F32),
                        pltpu.VMEM((ATT_KV_HEADS, VT_ROWS, gq), F32)],
        compiler_params=pltpu.CompilerParams(dimension_semantics=("arbitrary", "arbitrary"),
                                             vmem_limit_bytes=VMEM_LIMIT),
        name="dsa",
    )(proj, proj, proj, kx3, kb3, vt, g_out.reshape(1, -1))


OUT_RB = 128


def _outproj_kernel(rec_ref, att_ref, x_ref, gt_ref, sc_ref, sh_ref, g_ref, wo_ref, wrh_ref, wrl_ref, br_ref,
                    x1_ref, h2_ref, oh0_ref, oh1_ref, wts_ref):
    wo = wo_ref[...]
    half = rec_ref.shape[1]
    tm = x_ref.shape[0]
    blocks = [slice(r, r + OUT_RB) for r in range(0, tm, OUT_RB)]
    mixed = [jnp.dot(rec_ref[rs, :], wo[:half], preferred_element_type=F32)
             + jnp.dot(att_ref[rs, :], wo[half:], preferred_element_type=F32) for rs in blocks]
    his, los = [], []
    for rs, mx in zip(blocks, mixed):
        x1 = x_ref[rs, :] + gt_ref[0] * mx
        x1_ref[rs, :] = x1
        h2 = _norm_mod(x1, g_ref[...], sc_ref[0], sh_ref[0])
        h2_ref[rs, :] = h2
        hi = h2.astype(BF16)
        his.append(hi)
        los.append((h2 - hi.astype(F32)).astype(BF16))
    wrh, wrl = wrh_ref[...], wrl_ref[...]
    logits = jnp.concatenate(
        [jnp.dot(hi, wrh, preferred_element_type=F32) + jnp.dot(lo, wrh, preferred_element_type=F32)
         + jnp.dot(hi, wrl, preferred_element_type=F32) for hi, lo in zip(his, los)], axis=0) + br_ref[...]
    lane = lax.broadcasted_iota(I32, (tm, LANES), 1)
    big = jnp.int32(LANES)

    def argmax_first(vals, mask):
        mv = jnp.where(mask, vals, -jnp.inf)
        top = jnp.max(mv, axis=1, keepdims=True)
        idx = jnp.min(jnp.where(jnp.logical_and(mask, mv == top), lane, big), axis=1, keepdims=True)
        return top, idx

    gmask = lane < N_GROUPS
    gtop, gidx = argmax_first(logits, gmask)
    p_g = 1.0 / jnp.sum(jnp.where(gmask, jnp.exp(logits - gtop), 0.0), axis=1, keepdims=True)
    e_lo = N_GROUPS + gidx * EXPERTS_PER_GROUP
    emask = jnp.logical_and(lane >= e_lo, lane < e_lo + EXPERTS_PER_GROUP)
    v1, i1 = argmax_first(logits, emask)
    v2, i2 = argmax_first(logits, jnp.logical_and(emask, lane != i1))
    r = jnp.exp(v2 - v1)
    w1 = p_g / (1.0 + r)
    w2 = p_g * r / (1.0 + r)
    oh0_ref[...] = jnp.where(lane + N_GROUPS == i1, 1.0, 0.0).astype(BF16)
    oh1_ref[...] = jnp.where(lane + N_GROUPS == i2, 1.0, 0.0).astype(BF16)
    wts_ref[...] = jnp.where(lane == 0, w1, jnp.where(lane == 1, w2, 0.0))


def _outproj(rec, att, x2, gt, sc, sh, g, wo_bf, wr, br, seq):
    n, d = x2.shape
    wr_hi = wr.astype(BF16)
    wr_lo = (wr - wr_hi.astype(F32)).astype(BF16)
    tm = min(512, seq)
    per_b = seq // tm
    half = rec.shape[1]
    bspec = pl.BlockSpec((1, 1, d), lambda i: (i // per_b, 0, 0))
    return pl.pallas_call(
        _outproj_kernel,
        out_shape=(jax.ShapeDtypeStruct((n, d), F32),
                   jax.ShapeDtypeStruct((n, d), F32),
                   jax.ShapeDtypeStruct((n, LANES), BF16),
                   jax.ShapeDtypeStruct((n, LANES), BF16),
                   jax.ShapeDtypeStruct((n, LANES), F32)),
        grid=(n // tm,),
        in_specs=[pl.BlockSpec((tm, half), lambda i: (i, 0)),
                  pl.BlockSpec((tm, half), lambda i: (i, 0)),
                  pl.BlockSpec((tm, d), lambda i: (i, 0)),
                  bspec, bspec, bspec,
                  pl.BlockSpec((1, d), lambda i: (0, 0)),
                  pl.BlockSpec((2 * half, d), lambda i: (0, 0)),
                  pl.BlockSpec((d, LANES), lambda i: (0, 0)),
                  pl.BlockSpec((d, LANES), lambda i: (0, 0)),
                  pl.BlockSpec((1, LANES), lambda i: (0, 0))],
        out_specs=(pl.BlockSpec((tm, d), lambda i: (i, 0)),
                   pl.BlockSpec((tm, d), lambda i: (i, 0)),
                   pl.BlockSpec((tm, LANES), lambda i: (i, 0)),
                   pl.BlockSpec((tm, LANES), lambda i: (i, 0)),
                   pl.BlockSpec((tm, LANES), lambda i: (i, 0))),
        compiler_params=pltpu.CompilerParams(dimension_semantics=("arbitrary",),
                                             vmem_limit_bytes=VMEM_LIMIT),
        name="outproj",
    )(rec, att, x2, gt, sc, sh, g, wo_bf, wr_hi, wr_lo, br)


MOE_TM = 256


def _moe_kernel(te_ref, nt_ref, hs_ref, wg_ref, wu_ref, wd_ref, o_ref):
    t = pl.program_id(0)

    @pl.when(t < nt_ref[0])
    def _():
        xs = hs_ref[...].astype(BF16)
        gte = jnp.dot(xs, wg_ref[...].astype(BF16), preferred_element_type=F32)
        up = jnp.dot(xs, wu_ref[...].astype(BF16), preferred_element_type=F32)
        act = (_silu(gte) * up).astype(BF16)
        o_ref[...] = jnp.dot(act, wd_ref[...].astype(BF16), preferred_element_type=F32)

    @pl.when(t >= nt_ref[0])
    def _():
        o_ref[...] = jnp.zeros_like(o_ref)


def _moe(tile_expert, n_tiles, hs, wg, wu, wd):
    p = hs.shape[0]
    d, de = wg.shape[1], wg.shape[2]
    tm = MOE_TM
    grid_spec = pltpu.PrefetchScalarGridSpec(
        num_scalar_prefetch=2,
        grid=(p // tm,),
        in_specs=[pl.BlockSpec((tm, d), lambda t, te, nt: (jnp.minimum(t, nt[0] - 1), 0)),
                  pl.BlockSpec((None, d, de), lambda t, te, nt: (te[t], 0, 0)),
                  pl.BlockSpec((None, d, de), lambda t, te, nt: (te[t], 0, 0)),
                  pl.BlockSpec((None, de, d), lambda t, te, nt: (te[t], 0, 0))],
        out_specs=pl.BlockSpec((tm, d), lambda t, te, nt: (t, 0)),
    )
    return pl.pallas_call(
        _moe_kernel,
        out_shape=jax.ShapeDtypeStruct((p, d), F32),
        grid_spec=grid_spec,
        compiler_params=pltpu.CompilerParams(dimension_semantics=("arbitrary",),
                                             vmem_limit_bytes=VMEM_LIMIT_MOE),
        name="moe",
    )(tile_expert, n_tiles, hs, wg, wu, wd)


FIN_TM = 256
DMA_UNROLL = 8


def _final_kernel(dcur_ref, dnext_ref, x1_ref, wts_ref, gt_ref, g_ref, ys_ref, o_ref, buf, sem):
    i = pl.program_id(0)
    n = pl.num_programs(0)
    tm = x1_ref.shape[0]

    def row_copy(dref, slot, r, s):
        return pltpu.make_async_copy(ys_ref.at[pl.ds(dref[0, 0, 2 * r + s], 1)],
                                     buf.at[slot, s, pl.ds(r, 1)], sem.at[slot])

    def issue(dref, slot):
        def body(r, c):
            row_copy(dref, slot, r, 0).start()
            row_copy(dref, slot, r, 1).start()
            return c
        lax.fori_loop(0, tm, body, 0, unroll=DMA_UNROLL)

    @pl.when(i == 0)
    def _():
        issue(dcur_ref, 0)

    @pl.when(i + 1 < n)
    def _():
        issue(dnext_ref, (i + 1) % 2)

    slot = i % 2
    for s in range(2):
        pltpu.make_async_copy(ys_ref.at[pl.ds(0, tm)], buf.at[slot, s], sem.at[slot]).wait()
    w = wts_ref[...]
    y = w[:, 0:1] * buf[slot, 0] + w[:, 1:2] * buf[slot, 1]
    xo = x1_ref[...] + gt_ref[0] * y
    o_ref[...] = xo * lax.rsqrt(jnp.mean(xo * xo, axis=-1, keepdims=True) + EPS) * g_ref[...]


def _final(dest3, x1, wts, gt, g, ys, seq):
    n, d = x1.shape
    tm = FIN_TM
    per_b = seq // tm
    steps = n // tm
    row = pl.BlockSpec((tm, d), lambda i: (i, 0))
    smem = lambda f: pl.BlockSpec((1, 1, 2 * tm), f, memory_space=pltpu.SMEM)
    return pl.pallas_call(
        _final_kernel,
        out_shape=jax.ShapeDtypeStruct((n, d), F32),
        grid=(steps,),
        in_specs=[smem(lambda i: (i, 0, 0)),
                  smem(lambda i: (jnp.minimum(i + 1, steps - 1), 0, 0)),
                  row,
                  pl.BlockSpec((tm, LANES), lambda i: (i, 0)),
                  pl.BlockSpec((1, 1, d), lambda i: (i // per_b, 0, 0)),
                  pl.BlockSpec((1, d), lambda i: (0, 0)),
                  pl.BlockSpec(memory_space=pl.ANY)],
        out_specs=row,
        scratch_shapes=[pltpu.VMEM((2, 2, tm, d), F32), pltpu.SemaphoreType.DMA((2,))],
        compiler_params=pltpu.CompilerParams(dimension_semantics=("arbitrary",),
                                             vmem_limit_bytes=VMEM_LIMIT),
        name="final",
    )(dest3, dest3, x1, wts, gt, g, ys)


ROUTE_T = 512


def _route_kernel(oh0_ref, oh1_ref, dest_ref, cnt_ref, run_ref, tot_ref):
    ph = pl.program_id(0)
    i = pl.program_id(1)
    a0 = oh0_ref[...]
    a1 = oh1_ref[...]
    both = a0 + a1
    colsum = jnp.sum(both.astype(F32), axis=0, keepdims=True)

    @pl.when(jnp.logical_and(ph == 0, i == 0))
    def _():
        tot_ref[...] = jnp.zeros_like(tot_ref)

    @pl.when(ph == 0)
    def _():
        tot_ref[...] = tot_ref[...] + colsum

    @pl.when(ph == 1)
    def _():
        @pl.when(i == 0)
        def _():
            run_ref[...] = jnp.zeros_like(run_ref)

        tot = tot_ref[...]
        tiles = jnp.ceil(tot * (1.0 / MOE_TM))
        rr = lax.broadcasted_iota(I32, (LANES, LANES), 0)
        cc = lax.broadcasted_iota(I32, (LANES, LANES), 1)
        before = (rr < cc).astype(BF16)
        tiles8 = jnp.broadcast_to(tiles, (8, LANES)).astype(BF16)
        poff = jnp.dot(tiles8, before, preferred_element_type=F32)[0:1] * float(MOE_TM)
        t = a0.shape[0]
        r2 = lax.broadcasted_iota(I32, (t, t), 0)
        c2 = lax.broadcasted_iota(I32, (t, t), 1)
        earlier = (c2 < r2).astype(BF16)
        rank = jnp.dot(earlier, both, preferred_element_type=F32)
        tgt = poff + run_ref[...] + rank
        d0 = jnp.sum(a0.astype(F32) * tgt, axis=1, keepdims=True)
        d1 = jnp.sum(a1.astype(F32) * tgt, axis=1, keepdims=True)
        lane = lax.broadcasted_iota(I32, (t, LANES), 1)
        dest_ref[...] = jnp.where(lane == 0, d0, jnp.where(lane == 1, d1, 0.0)).astype(I32)
        run_ref[...] = run_ref[...] + colsum
        cnt_ref[...] = tot


def _route(oh0, oh1):
    n = oh0.shape[0]
    t = ROUTE_T
    blk = pl.BlockSpec((t, LANES), lambda ph, i: (i, 0))
    return pl.pallas_call(
        _route_kernel,
        out_shape=(jax.ShapeDtypeStruct((n, LANES), I32), jax.ShapeDtypeStruct((1, LANES), F32)),
        grid=(2, n // t),
        in_specs=[blk, blk],
        out_specs=(pl.BlockSpec((t, LANES), lambda ph, i: (i * ph, 0)),
                   pl.BlockSpec((1, LANES), lambda ph, i: (0, 0))),
        scratch_shapes=[pltpu.VMEM((1, LANES), F32), pltpu.VMEM((1, LANES), F32)],
        compiler_params=pltpu.CompilerParams(dimension_semantics=("arbitrary", "arbitrary"),
                                             vmem_limit_bytes=VMEM_LIMIT),
        name="route",
    )(oh0, oh1)


DISP_TM = 1024


def _dispatch_kernel(pad_ref, dest_ref, h_ref, hs_ref, zero_ref, sem, zsem):
    tm = h_ref.shape[0]

    @pl.when(pl.program_id(0) == 0)
    def _():
        zero_ref[...] = jnp.zeros_like(zero_ref)

        def zero_copy(row0):
            return pltpu.make_async_copy(zero_ref, hs_ref.at[pl.ds(pl.multiple_of(row0, MOE_TM), MOE_TM)], zsem)

        def fill(e, c):
            @pl.when(pad_ref[e] >= 0)
            def _():
                zero_copy(pad_ref[e]).start()
            return c

        def drain(e, c):
            @pl.when(pad_ref[e] >= 0)
            def _():
                zero_copy(pad_ref[e]).wait()
            return c

        lax.fori_loop(0, N_EXPERTS, fill, 0)
        lax.fori_loop(0, N_EXPERTS, drain, 0)
        used = pad_ref[N_EXPERTS]
        total = hs_ref.shape[0] // MOE_TM
        lax.fori_loop(used, total, lambda t, c: (zero_copy(t * MOE_TM).start(), c)[1], 0)
        lax.fori_loop(used, total, lambda t, c: (zero_copy(t * MOE_TM).wait(), c)[1], 0)

    def body(r, c):
        for s in range(2):
            pltpu.make_async_copy(h_ref.at[pl.ds(r, 1)], hs_ref.at[pl.ds(dest_ref[0, 0, 2 * r + s], 1)],
                                  sem).start()
        return c

    lax.fori_loop(0, tm, body, 0, unroll=DMA_UNROLL)
    for _ in range(2):
        pltpu.make_async_copy(h_ref, hs_ref.at[pl.ds(0, tm)], sem).wait()


def _dispatch(pad_start, dest3, h2, p_rows):
    n, w = h2.shape
    tm = DISP_TM
    grid_spec = pltpu.PrefetchScalarGridSpec(
        num_scalar_prefetch=1,
        grid=(n // tm,),
        in_specs=[pl.BlockSpec((1, 1, 2 * tm), lambda i, pad: (i, 0, 0), memory_space=pltpu.SMEM),
                  pl.BlockSpec((tm, w), lambda i, pad: (i, 0))],
        out_specs=pl.BlockSpec(memory_space=pl.ANY),
        scratch_shapes=[pltpu.VMEM((MOE_TM, w), h2.dtype), pltpu.SemaphoreType.DMA(()),
                        pltpu.SemaphoreType.DMA(())],
    )
    return pl.pallas_call(
        _dispatch_kernel,
        out_shape=jax.ShapeDtypeStruct((p_rows, w), h2.dtype),
        grid_spec=grid_spec,
        compiler_params=pltpu.CompilerParams(dimension_semantics=("arbitrary",),
                                             vmem_limit_bytes=VMEM_LIMIT),
        name="dispatch",
    )(pad_start, dest3, h2)


def _tile_tables(counts, n_tiles_max):
    cnt = counts[0, :N_EXPERTS].astype(I32)
    tiles = (cnt + MOE_TM - 1) // MOE_TM
    tiles_end = jnp.cumsum(tiles)
    tile = jnp.arange(n_tiles_max, dtype=I32)
    tile_expert = jnp.sum((tile[:, None] >= tiles_end[None, :]).astype(I32), axis=1)
    n_used = tiles_end[-1:].astype(I32)
    pad_start = jnp.where(tiles > 0, (tiles_end - 1) * MOE_TM, -1).astype(I32)
    return jnp.minimum(tile_expert, N_EXPERTS - 1), n_used, jnp.concatenate([pad_start, n_used])


def kernel(x, c, w_ada, b_ada, g_norm_mix, w_in, lb_logits, g_rec_out, g_att_out, w_out, g_norm_ffn,
           w_router_group, b_router_group, w_router_expert, b_router_expert,
           w_expert_gate, w_expert_up, w_expert_down, g_final):
    bsz, seq, d = x.shape
    n = bsz * seq
    assert w_ada.shape[0] == 1, "single trunk layer"
    layer = 0
    x2 = x.reshape(n, d)

    mod = _adaln(c, w_ada[layer], b_ada[layer])
    sh1, sc1, gt1, sh2, sc2, gt2 = [m.reshape(bsz, 1, d) for m in jnp.split(mod, 6, axis=-1)]

    w_in_bf = jnp.pad(w_in[layer], ((0, 0), (0, IN_PAD - IN_COLS))).astype(BF16)
    proj = _inproj(x2, sc1, sh1, g_norm_mix[layer].reshape(1, d), w_in_bf, seq)
    kb, vb, kxb = _kvprep(proj)
    rec = _hgrn(proj, lb_logits, g_rec_out[layer], bsz, seq, layer)
    att = _dsa(proj, kb, vb, kxb, g_att_out[layer], bsz, seq)

    wr = jnp.concatenate([w_router_group[layer], w_router_expert[layer]], axis=1)
    wr = jnp.pad(wr, ((0, 0), (0, LANES - wr.shape[1])))
    br = jnp.concatenate([b_router_group[layer], b_router_expert[layer]])
    br = jnp.pad(br, (0, LANES - br.shape[0])).reshape(1, LANES)
    x1, h2, oh0, oh1, wts = _outproj(rec, att, x2, gt1, sc2, sh2, g_norm_ffn[layer].reshape(1, d),
                                      w_out[layer].astype(BF16), wr, br, seq)

    dest, counts = _route(oh0, oh1)
    dest2 = dest[:, :2]
    p_rows = 2 * n + N_EXPERTS * MOE_TM
    tile_expert, n_tiles, pad_start = _tile_tables(counts, p_rows // MOE_TM)
    hs = _dispatch(pad_start, dest2.reshape(n // DISP_TM, 1, 2 * DISP_TM), h2, p_rows)
    ys = _moe(tile_expert, n_tiles, hs, w_expert_gate[layer], w_expert_up[layer], w_expert_down[layer])
    out = _final(dest2.reshape(n // FIN_TM, 1, 2 * FIN_TM), x1, wts, gt2, g_final.reshape(1, d), ys, seq)
    return out.reshape(bsz, seq, d)
```

```python
import functools

import jax
import jax.numpy as jnp
import numpy as np
from jax import lax
from jax.experimental import pallas as pl
from jax.experimental.pallas import tpu as pltpu

F32 = jnp.float32
BF16 = jnp.bfloat16
I32 = jnp.int32

EPS = 1e-6
LANES = 128

REC_HEADS = 8
REC_D = 128
REC_CHUNK = 64
REC_SUB = 16
ATT_HEADS = 8
ATT_DH = 128
ATT_KV_HEADS = 2
ATT_GROUP = ATT_HEADS // ATT_KV_HEADS
IDX_HEADS = 8
IDX_DIM = 64
TOPK_MAX = 256
N_GROUPS = 4
EXPERTS_PER_GROUP = 8
N_EXPERTS = N_GROUPS * EXPERTS_PER_GROUP

OFF_RQ = 0
OFF_RF = 1024
OFF_RI = 2048
OFF_RG = 3072
OFF_AQ = 4096
OFF_AK = 5120
OFF_IQ = 5632
OFF_IK = 6144
IN_COLS = 6216

VMEM_LIMIT = 48 * 1024 * 1024
VMEM_LIMIT_MOE = 56 * 1024 * 1024

INT_MIN = -(2 ** 31)
KEY_NEGINF = int(np.array(-np.inf, np.float32).view(np.int32)) ^ 0x7FFFFFFF
NEG_BIG = -1e30


def _silu(v):
    return v * jax.nn.sigmoid(v)


def _nt_dot(a, b):
    return lax.dot_general(a, b, (((1,), (1,)), ((), ())), preferred_element_type=F32)


def _tn_dot(a, b):
    return lax.dot_general(a, b, (((0,), (0,)), ((), ())), preferred_element_type=F32)


def _adaln_kernel(c_ref, w_ref, b_ref, o_ref):
    ca = _silu(c_ref[...])
    o_ref[...] = jnp.dot(ca, w_ref[...], preferred_element_type=F32,
                         precision=lax.Precision.HIGHEST) + b_ref[...]


def _adaln(c, w, b):
    bsz, d = c.shape
    n = w.shape[1]
    tn = 512
    return pl.pallas_call(
        _adaln_kernel,
        out_shape=jax.ShapeDtypeStruct((bsz, n), F32),
        grid=(n // tn,),
        in_specs=[pl.BlockSpec((bsz, d), lambda j: (0, 0)),
                  pl.BlockSpec((d, tn), lambda j: (0, j)),
                  pl.BlockSpec((1, tn), lambda j: (0, j))],
        out_specs=pl.BlockSpec((bsz, tn), lambda j: (0, j)),
        compiler_params=pltpu.CompilerParams(dimension_semantics=("arbitrary",),
                                             vmem_limit_bytes=VMEM_LIMIT),
        name="adaln",
    )(c, w, b.reshape(1, n))


def _norm_mod(x, g, sc, sh):
    xn = x * lax.rsqrt(jnp.mean(x * x, axis=-1, keepdims=True) + EPS)
    return xn * g * (1.0 + sc) + sh


IN_RB = 256


def _inproj_kernel(x_ref, sc_ref, sh_ref, g_ref, w_ref, wt_ref, o_ref, ot_ref, h_ref):
    w = w_ref[...]

    @pl.when(pl.program_id(1) == 0)
    def _():
        tm = x_ref.shape[0]
        rb = min(IN_RB, tm)
        hs = []
        for r in range(0, tm, rb):
            h = _norm_mod(x_ref[r:r + rb, :], g_ref[...], sc_ref[0], sh_ref[0]).astype(BF16)
            h_ref[r:r + rb, :] = h
            hs.append(h)
        wt = wt_ref[...]
        for i, r in enumerate(range(0, tm, rb)):
            o_ref[r:r + rb, :] = jnp.dot(hs[i], w, preferred_element_type=F32)
            ot_ref[r:r + rb, :] = jnp.dot(hs[i], wt, preferred_element_type=F32)

    @pl.when(pl.program_id(1) != 0)
    def _():
        o_ref[...] = jnp.dot(h_ref[...], w, preferred_element_type=F32)


def _inproj(x2, sc, sh, g, w_in, seq):
    n, d = x2.shape
    tm = min(1024, seq)
    tn = 1024
    per_b = seq // tm
    w_tail = jnp.pad(w_in[:, OFF_IK:], ((0, 0), (0, LANES - (IN_COLS - OFF_IK)))).astype(BF16)
    w_bf = w_in.astype(BF16)
    return pl.pallas_call(
        _inproj_kernel,
        out_shape=(jax.ShapeDtypeStruct((n, OFF_IK), F32), jax.ShapeDtypeStruct((n, LANES), F32)),
        grid=(n // tm, OFF_IK // tn),
        in_specs=[pl.BlockSpec((tm, d), lambda i, j: (i, 0)),
                  pl.BlockSpec((1, 1, d), lambda i, j: (i // per_b, 0, 0)),
                  pl.BlockSpec((1, 1, d), lambda i, j: (i // per_b, 0, 0)),
                  pl.BlockSpec((1, d), lambda i, j: (0, 0)),
                  pl.BlockSpec((d, tn), lambda i, j: (0, j)),
                  pl.BlockSpec((d, LANES), lambda i, j: (0, 0))],
        out_specs=(pl.BlockSpec((tm, tn), lambda i, j: (i, j)),
                   pl.BlockSpec((tm, LANES), lambda i, j: (i, 0))),
        scratch_shapes=[pltpu.VMEM((tm, d), BF16)],
        compiler_params=pltpu.CompilerParams(dimension_semantics=("arbitrary", "arbitrary"),
                                             vmem_limit_bytes=VMEM_LIMIT),
        name="inproj",
    )(x2, sc, sh, g, w_bf, w_tail)


VT_ROWS = ATT_DH + 16


def _kvprep_kernel(kv_ref, ik_ref, k_ref, vt_ref, kx_ref):
    kv = kv_ref[...]
    tm = kv.shape[0]
    k_ref[...] = kv[:, :256].astype(BF16)
    tail = jnp.where(lax.broadcasted_iota(I32, (16, tm), 0) == 0, 1.0, 0.0)
    for g in range(ATT_KV_HEADS):
        vt = kv[:, 256 + g * ATT_DH:256 + (g + 1) * ATT_DH].T
        vt_ref[g] = jnp.concatenate([vt, tail], axis=0).astype(BF16)
    kx_ref[...] = ik_ref[...][:, :IDX_DIM].astype(BF16)


def _kvprep(proj, ptail):
    n = proj.shape[0]
    tm = 512
    return pl.pallas_call(
        _kvprep_kernel,
        out_shape=(jax.ShapeDtypeStruct((n, 256), BF16),
                   jax.ShapeDtypeStruct((ATT_KV_HEADS, VT_ROWS, n), BF16),
                   jax.ShapeDtypeStruct((n, IDX_DIM), BF16)),
        grid=(n // tm,),
        in_specs=[pl.BlockSpec((tm, 512), lambda i: (i, OFF_AK // 512)),
                  pl.BlockSpec((tm, LANES), lambda i: (i, 0))],
        out_specs=(pl.BlockSpec((tm, 256), lambda i: (i, 0)),
                   pl.BlockSpec((ATT_KV_HEADS, VT_ROWS, tm), lambda i: (0, 0, i)),
                   pl.BlockSpec((tm, IDX_DIM), lambda i: (i, 0))),
        compiler_params=pltpu.CompilerParams(dimension_semantics=("arbitrary",),
                                             vmem_limit_bytes=VMEM_LIMIT),
        name="kvprep",
    )(proj, ptail)


def _hgrn_kernel(q_ref, f_ref, i_ref, g_ref, lbl_ref, gout_ref, o_ref, st_ref, *, chunks, layer):
    @pl.when(pl.program_id(2) == 0)
    def _():
        st_ref[...] = jnp.zeros_like(st_ref)

    lbl = lbl_ref[...]
    e = jnp.exp(lbl - jnp.max(lbl, axis=0, keepdims=True))
    sm = e / jnp.sum(e, axis=0, keepdims=True)
    lb = jnp.sum(sm[: layer + 1], axis=0, keepdims=True)
    gout = gout_ref[...]

    c = REC_CHUNK
    nsub = c // REC_SUB
    rr = lax.broadcasted_iota(I32, (c, c), 0)
    cc = lax.broadcasted_iota(I32, (c, c), 1)
    row = lax.broadcasted_iota(I32, (c, REC_D), 0)
    sub = lax.shift_right_logical(row, REC_SUB.bit_length() - 1)

    f = lb + (1.0 - lb) * jax.nn.sigmoid(f_ref[...])
    logf = jnp.log2(f)
    k = 1.0 - f
    qf = _silu(q_ref[...]) * (REC_D ** -0.5)
    vb = i_ref[...].astype(BF16)

    halves = [1 << l for l in range(REC_SUB.bit_length() - 1)]
    logf_w = jnp.concatenate([logf[ci * c:(ci + 1) * c] for ci in range(chunks)], axis=1)
    b_all = jnp.dot((rr >= cc).astype(F32), logf_w, preferred_element_type=F32,
                    precision=lax.Precision.HIGHEST)
    row_w = lax.broadcasted_iota(I32, b_all.shape, 0)
    blk_end, split = b_all, {}
    for h in halves:
        first = (row_w & (2 * h - 1)) < h
        split[h] = jnp.where(first, blk_end, pltpu.roll(blk_end, h, 0))
        blk_end = jnp.where(first, pltpu.roll(blk_end, c - h, 0), blk_end)
    lvl_mask = [jnp.logical_and((rr ^ cc) < 2 * h, jnp.logical_and((rr & h) != 0, (cc & h) == 0)) for h in halves]
    eye = rr == cc
    in_second = {h: (row & h) != 0 for h in halves}
    after_sub = [sub > i for i in range(nsub - 1)]
    in_sub = [sub == i for i in range(nsub - 1)]

    qxs, kxs, qls, kls, diag, upd, q_in, decay = [], [], [], [], [], [], [], []
    for ci in range(chunks):
        sl = slice(ci * c, (ci + 1) * c)
        cols = slice(ci * REC_D, (ci + 1) * REC_D)
        b = b_all[:, cols]
        kc, qc = k[sl], qf[sl]
        qparts, kparts = [], []
        for i in range(nsub - 1):
            r = b[(i + 1) * REC_SUB - 1:(i + 1) * REC_SUB, :]
            qparts.append(jnp.where(after_sub[i], qc * jnp.exp2(b - r), 0.0))
            kparts.append(jnp.where(in_sub[i], kc * jnp.exp2(r - b), 0.0))
        qxs.append(jnp.concatenate(qparts, axis=1).astype(BF16))
        kxs.append(jnp.concatenate(kparts, axis=1).astype(BF16))
        for h in halves:
            e = jnp.exp2(-jnp.abs(b - split[h][:, cols]))
            qls.append(jnp.where(in_second[h], qc * e, 0.0).astype(BF16))
            kls.append(jnp.where(in_second[h], 0.0, kc * e).astype(BF16))
        diag.append(jnp.sum(qc * kc, axis=1, keepdims=True))
        b_end = b[c - 1:c, :]
        upd.append((kc * jnp.exp2(b_end - b)).astype(BF16))
        q_in.append((qc * jnp.exp2(b)).astype(BF16))
        decay.append(jnp.exp2(b_end))
    nl = len(halves)
    cross = [_nt_dot(qxs[ci], kxs[ci]) for ci in range(chunks)]
    within = [[_nt_dot(qls[ci * nl + l], kls[ci * nl + l]) for l in range(nl)] for ci in range(chunks)]
    scores = []
    for ci in range(chunks):
        s = cross[ci] + jnp.where(eye, diag[ci], 0.0)
        for l in range(nl):
            s = s + jnp.where(lvl_mask[l], within[ci][l], 0.0)
        scores.append(s.astype(BF16))
    upd = [_tn_dot(vb[ci * c:(ci + 1) * c], upd[ci]) for ci in range(chunks)]
    intra = [jnp.dot(scores[ci], vb[ci * c:(ci + 1) * c], preferred_element_type=F32) for ci in range(chunks)]

    st = st_ref[...]
    outs = []
    for ci in range(chunks):
        outs.append(_nt_dot(q_in[ci], st.astype(BF16)) + intra[ci])
        st = st * decay[ci] + upd[ci]
    st_ref[...] = st

    o = jnp.concatenate(outs, axis=0)
    on = o * lax.rsqrt(jnp.mean(o * o, axis=-1, keepdims=True) + EPS)
    o_ref[...] = (on * gout * _silu(g_ref[...])).astype(o_ref.dtype)


def _hgrn(proj, lb_logits, g_out, bsz, seq, layer):
    n = proj.shape[0]
    tc = min(2048, seq)
    per_b = seq // tc
    nl = lb_logits.shape[0]

    def col(off):
        return lambda b, h, c: (b * per_b + c, off // REC_D + h)

    return pl.pallas_call(
        functools.partial(_hgrn_kernel, chunks=tc // REC_CHUNK, layer=layer),
        out_shape=jax.ShapeDtypeStruct((n, REC_HEADS * REC_D), BF16),
        grid=(bsz, REC_HEADS, per_b),
        in_specs=[pl.BlockSpec((tc, REC_D), col(OFF_RQ)),
                  pl.BlockSpec((tc, REC_D), col(OFF_RF)),
                  pl.BlockSpec((tc, REC_D), col(OFF_RI)),
                  pl.BlockSpec((tc, REC_D), col(OFF_RG)),
                  pl.BlockSpec((nl, REC_D), lambda b, h, c: (0, h)),
                  pl.BlockSpec((1, REC_D), lambda b, h, c: (0, h))],
        out_specs=pl.BlockSpec((tc, REC_D), lambda b, h, c: (b * per_b + c, h)),
        scratch_shapes=[pltpu.VMEM((REC_D, REC_D), F32)],
        compiler_params=pltpu.CompilerParams(
            dimension_semantics=("arbitrary", "arbitrary", "arbitrary"),
            vmem_limit_bytes=VMEM_LIMIT),
        name="hgrn2",
    )(proj, proj, proj, proj, lb_logits, g_out.reshape(1, -1))


TQ = 128
TK = 1024
TK_TAIL = TK // 2
LOG2E = 1.4426950408889634
CNT_ROWS = 64
CNT_BLOCK = 256
SETTLE_EVERY = 4
VALUE_SPLITS = 4


def _dsa_kernel(qi_ref, iw_ref, aq_ref, kx_ref, k_ref, vt_ref, gout_ref, o_ref,
                sc_ref, m_ref, acc_ref, *, ksel, seq):
    t0 = pl.program_id(1) * TQ
    need = t0 + TQ
    nfull = (need + TK_TAIL - 1) // TK
    tail0 = pl.multiple_of(nfull * TK, TK)
    has_tail = need > tail0
    nhalf = 2 * nfull + has_tail.astype(I32)
    qpos = t0 + lax.broadcasted_iota(I32, (1, TQ), 1)
    krow = lax.broadcasted_iota(I32, (TK, TQ), 0)

    def over_tiles(body):
        lax.fori_loop(0, nfull, lambda kt, c: (body(pl.multiple_of(kt * TK, TK), TK), c)[1], 0)

        @pl.when(has_tail)
        def _():
            body(tail0, TK_TAIL)

    qit = (qi_ref[...] * (IDX_DIM ** -0.5)).T
    qht = jnp.concatenate([qit[h * IDX_DIM:(h + 1) * IDX_DIM] for h in range(IDX_HEADS)],
                          axis=1).astype(BF16)
    iwt = iw_ref[...].T
    wrow = [iwt[IDX_DIM + h:IDX_DIM + h + 1] * (IDX_HEADS ** -0.5) for h in range(IDX_HEADS)]

    def score_body(k0, rows):
        rel = jnp.dot(kx_ref[pl.ds(k0, rows), :], qht, preferred_element_type=F32)
        sc = jnp.zeros((rows, TQ), F32)
        for h in range(IDX_HEADS):
            sc = sc + wrow[h] * jnp.maximum(rel[:, h * TQ:(h + 1) * TQ], 0.0)
        sc_ref[pl.ds(k0, rows), :] = jnp.where(k0 + krow[:rows] <= qpos, sc, -jnp.inf)

    over_tiles(score_body)

    def u_to_float(u):
        key = u ^ INT_MIN
        return lax.bitcast_convert_type(key ^ ((key >> 31) & 0x7FFFFFFF), F32)

    def float_to_u(f):
        bits = lax.bitcast_convert_type(f, I32)
        return bits ^ ((bits >> 31) & 0x7FFFFFFF) ^ INT_MIN

    def count(pred):
        def block(k0, rows, acc):
            for u in range(0, rows, CNT_BLOCK):
                r0 = pl.multiple_of(k0 + u, CNT_BLOCK)
                hit = pred(sc_ref[pl.ds(r0, CNT_BLOCK), :], r0 + krow[:CNT_BLOCK])
                acc = acc + jnp.sum(jnp.where(hit, 1.0, 0.0).reshape(CNT_BLOCK // CNT_ROWS, CNT_ROWS, TQ), axis=0)
            return acc
        acc = lax.fori_loop(0, nfull, lambda kt, a: block(kt * TK, TK, a), jnp.zeros((CNT_ROWS, TQ), F32))
        acc = lax.cond(has_tail, lambda a: block(tail0, TK_TAIL, a), lambda a: a, acc)
        return jnp.sum(acc, axis=0, keepdims=True)

    def group_max(j, gm):
        r0 = pl.multiple_of(j * ksel, ksel)
        return jnp.maximum(gm, sc_ref[pl.ds(r0, ksel), :])

    gm = lax.fori_loop(0, nhalf * (TK_TAIL // ksel), group_max, jnp.full((ksel, TQ), -jnp.inf, F32))
    kf = float(ksel)

    def ult(a, b):
        return (a ^ INT_MIN) < (b ^ INT_MIN)

    def is_open(lo, hi):
        return ult(jnp.int32(1), hi - lo)

    def not_settled(cnt):
        return jnp.where(cnt == kf, 0.0, 1.0)

    def search_cond(st):
        _, lo, hi, cnt = st
        return jnp.max(jnp.where(is_open(lo, hi), not_settled(cnt), 0.0)) > 0.0

    def search_step(_, st):
        it, lo, hi, cnt = st
        mid_u = lo + lax.shift_right_logical(hi - lo, 1)
        mid_v = float_to_u(0.5 * (u_to_float(lo) + u_to_float(hi)))
        use_v = jnp.logical_and(it < VALUE_SPLITS, jnp.logical_and(ult(lo, mid_v), ult(mid_v, hi)))
        cand = jnp.where(is_open(lo, hi), jnp.where(use_v, mid_v, mid_u), lo)
        cand_f = u_to_float(cand)
        c = count(lambda s, pos: s >= cand_f)
        take = c >= kf
        return it + 1, jnp.where(take, cand, lo), jnp.where(take, hi, cand), jnp.where(take, c, cnt)

    def search_body(st):
        return lax.fori_loop(0, SETTLE_EVERY, search_step, st)

    _, u_thr, _, cnt_thr = lax.while_loop(
        search_cond, search_body,
        (jnp.int32(0), float_to_u(jnp.min(gm, axis=0, keepdims=True)),
         float_to_u(jnp.max(gm, axis=0, keepdims=True)) + 1, jnp.full((1, TQ), -1.0, F32)))
    thr = u_to_float(u_thr)
    real = (u_thr ^ INT_MIN) > KEY_NEGINF

    nbits = seq.bit_length()

    def resolve_ties():
        c_gt = count(lambda s, pos: s > thr)
        c_eq = count(lambda s, pos: s == thr)
        need = kf - c_gt
        excess = jnp.logical_and(c_eq > need, real)

        def find_last():
            def jbody(i, y):
                cand = y | lax.shift_left(jnp.int32(1), (nbits - 1) - i)
                below = count(lambda s, pos: jnp.logical_and(s == thr, pos < cand))
                return jnp.where(below <= need - 1.0, cand, y)
            return lax.fori_loop(0, nbits, jbody, jnp.zeros((1, TQ), I32))

        return lax.cond(jnp.max(jnp.where(excess, 1.0, 0.0)) > 0.0, find_last,
                        lambda: jnp.full((1, TQ), seq, I32))

    last = lax.cond(jnp.max(not_settled(cnt_thr)) > 0.0, resolve_ties, lambda: jnp.full((1, TQ), seq, I32))
    last = jnp.where(real, last, -1)
    thr_m = jnp.where(real, thr, -jnp.inf)

    aq = aq_ref[...] * ((ATT_DH ** -0.5) * LOG2E)
    qgt = []
    for g in range(ATT_KV_HEADS):
        blk = [aq[:, (g * ATT_GROUP + j) * ATT_DH:(g * ATT_GROUP + j + 1) * ATT_DH].T for j in range(ATT_GROUP)]
        qgt.append(jnp.concatenate(blk, axis=1).astype(BF16))

    m_ref[...] = jnp.full(m_ref.shape, NEG_BIG, F32)
    acc_ref[...] = jnp.zeros(acc_ref.shape, F32)

    def att_body(k0, rows):
        sc = sc_ref[pl.ds(k0, rows), :]
        sel = jnp.logical_or(sc > thr_m, jnp.logical_and(sc == thr_m, k0 + krow[:rows] <= last))
        bias = jnp.where(sel, 0.0, NEG_BIG)
        bias = jnp.concatenate([bias] * ATT_GROUP, axis=1)
        heads = range(ATT_KV_HEADS)
        ss = [jnp.dot(k_ref[pl.ds(k0, rows), g * ATT_DH:(g + 1) * ATT_DH], qgt[g],
                      preferred_element_type=F32) + bias for g in heads]
        ps, alphas = [], []
        for g in heads:
            m_old = m_ref[g]
            m_new = jnp.maximum(m_old, jnp.max(ss[g], axis=0, keepdims=True))
            m_ref[g] = m_new
            alphas.append(jnp.exp2(m_old - m_new))
            ps.append(jnp.exp2(ss[g] - m_new).astype(BF16))
        pvs = [jnp.dot(vt_ref[g, :, pl.ds(k0, rows)], ps[g], preferred_element_type=F32) for g in heads]
        for g in heads:
            acc_ref[g] = alphas[g] * acc_ref[g] + pvs[g]

    over_tiles(att_body)

    gout = gout_ref[...]
    for g in range(ATT_KV_HEADS):
        a = acc_ref[g]
        o = a[:ATT_DH] / a[ATT_DH:ATT_DH + 1]
        on = o * lax.rsqrt(jnp.mean(o * o, axis=0, keepdims=True) + EPS)
        for j in range(ATT_GROUP):
            hsl = slice((g * ATT_GROUP + j) * ATT_DH, (g * ATT_GROUP + j + 1) * ATT_DH)
            o_ref[:, hsl] = (on[:, j * TQ:(j + 1) * TQ].T * gout[:, hsl]).astype(o_ref.dtype)


def _dsa(proj, ptail, kb, vt, kxb, g_out, bsz, seq):
    n = proj.shape[0]
    nqb = seq // TQ
    ksel = min(TOPK_MAX, seq // 4)
    assert TK_TAIL % ksel == 0 and TK_TAIL % CNT_BLOCK == 0 and seq % TK == 0
    kb3 = kb.reshape(bsz, seq, 256)
    kx3 = kxb.reshape(bsz, seq, IDX_DIM)
    gq = ATT_GROUP * TQ
    return pl.pallas_call(
        functools.partial(_dsa_kernel, ksel=ksel, seq=seq),
        out_shape=jax.ShapeDtypeStruct((n, ATT_HEADS * ATT_DH), BF16),
        grid=(bsz, nqb),
        in_specs=[pl.BlockSpec((TQ, 512), lambda b, q: (b * nqb + q, OFF_IQ // 512)),
                  pl.BlockSpec((TQ, LANES), lambda b, q: (b * nqb + q, 0)),
                  pl.BlockSpec((TQ, 1024), lambda b, q: (b * nqb + q, OFF_AQ // 1024)),
                  pl.BlockSpec((None, seq, IDX_DIM), lambda b, q: (b, 0, 0)),
                  pl.BlockSpec((None, seq, 256), lambda b, q: (b, 0, 0)),
                  pl.BlockSpec((ATT_KV_HEADS, VT_ROWS, seq), lambda b, q: (0, 0, b)),
                  pl.BlockSpec((1, ATT_HEADS * ATT_DH), lambda b, q: (0, 0))],
        out_specs=pl.BlockSpec((TQ, ATT_HEADS * ATT_DH), lambda b, q: (b * nqb + q, 0)),
        scratch_shapes=[pltpu.VMEM((seq, TQ), F32),
                        pltpu.VMEM((ATT_KV_HEADS, 1, gq), F32),
                        pltpu.VMEM((ATT_KV_HEADS, VT_ROWS, gq), F32)],
        compiler_params=pltpu.CompilerParams(dimension_semantics=("arbitrary", "arbitrary"),
                                             vmem_limit_bytes=VMEM_LIMIT),
        name="dsa",
    )(proj, ptail, proj, kx3, kb3, vt, g_out.reshape(1, -1))


OUT_RB = 128


def _outproj_kernel(rec_ref, att_ref, x_ref, gt_ref, sc_ref, sh_ref, g_ref, wo_ref, wrh_ref, wrl_ref, br_ref,
                    x1_ref, h2_ref, oh0_ref, oh1_ref, wts_ref):
    wo = wo_ref[...]
    half = rec_ref.shape[1]
    tm = x_ref.shape[0]
    blocks = [slice(r, r + OUT_RB) for r in range(0, tm, OUT_RB)]
    mixed = [jnp.dot(rec_ref[rs, :], wo[:half], preferred_element_type=F32)
             + jnp.dot(att_ref[rs, :], wo[half:], preferred_element_type=F32) for rs in blocks]
    his, los = [], []
    for rs, mx in zip(blocks, mixed):
        x1 = x_ref[rs, :] + gt_ref[0] * mx
        x1_ref[rs, :] = x1
        h2 = _norm_mod(x1, g_ref[...], sc_ref[0], sh_ref[0])
        h2_ref[rs, :] = h2
        hi = h2.astype(BF16)
        his.append(hi)
        los.append((h2 - hi.astype(F32)).astype(BF16))
    wrh, wrl = wrh_ref[...], wrl_ref[...]
    logits = jnp.concatenate(
        [jnp.dot(hi, wrh, preferred_element_type=F32) + jnp.dot(lo, wrh, preferred_element_type=F32)
         + jnp.dot(hi, wrl, preferred_element_type=F32) for hi, lo in zip(his, los)], axis=0) + br_ref[...]
    lane = lax.broadcasted_iota(I32, (tm, LANES), 1)
    big = jnp.int32(LANES)

    def argmax_first(vals, mask):
        mv = jnp.where(mask, vals, -jnp.inf)
        top = jnp.max(mv, axis=1, keepdims=True)
        idx = jnp.min(jnp.where(jnp.logical_and(mask, mv == top), lane, big), axis=1, keepdims=True)
        return top, idx

    gmask = lane < N_GROUPS
    gtop, gidx = argmax_first(logits, gmask)
    p_g = 1.0 / jnp.sum(jnp.where(gmask, jnp.exp(logits - gtop), 0.0), axis=1, keepdims=True)
    e_lo = N_GROUPS + gidx * EXPERTS_PER_GROUP
    emask = jnp.logical_and(lane >= e_lo, lane < e_lo + EXPERTS_PER_GROUP)
    v1, i1 = argmax_first(logits, emask)
    v2, i2 = argmax_first(logits, jnp.logical_and(emask, lane != i1))
    r = jnp.exp(v2 - v1)
    w1 = p_g / (1.0 + r)
    w2 = p_g * r / (1.0 + r)
    oh0_ref[...] = jnp.where(lane + N_GROUPS == i1, 1.0, 0.0).astype(BF16)
    oh1_ref[...] = jnp.where(lane + N_GROUPS == i2, 1.0, 0.0).astype(BF16)
    wts_ref[...] = jnp.where(lane == 0, w1, jnp.where(lane == 1, w2, 0.0))


def _outproj(rec, att, x2, gt, sc, sh, g, wo_bf, wr, br, seq):
    n, d = x2.shape
    wr_hi = wr.astype(BF16)
    wr_lo = (wr - wr_hi.astype(F32)).astype(BF16)
    tm = min(512, seq)
    per_b = seq // tm
    half = rec.shape[1]
    bspec = pl.BlockSpec((1, 1, d), lambda i: (i // per_b, 0, 0))
    return pl.pallas_call(
        _outproj_kernel,
        out_shape=(jax.ShapeDtypeStruct((n, d), F32),
                   jax.ShapeDtypeStruct((n, d), F32),
                   jax.ShapeDtypeStruct((n, LANES), BF16),
                   jax.ShapeDtypeStruct((n, LANES), BF16),
                   jax.ShapeDtypeStruct((n, LANES), F32)),
        grid=(n // tm,),
        in_specs=[pl.BlockSpec((tm, half), lambda i: (i, 0)),
                  pl.BlockSpec((tm, half), lambda i: (i, 0)),
                  pl.BlockSpec((tm, d), lambda i: (i, 0)),
                  bspec, bspec, bspec,
                  pl.BlockSpec((1, d), lambda i: (0, 0)),
                  pl.BlockSpec((2 * half, d), lambda i: (0, 0)),
                  pl.BlockSpec((d, LANES), lambda i: (0, 0)),
                  pl.BlockSpec((d, LANES), lambda i: (0, 0)),
                  pl.BlockSpec((1, LANES), lambda i: (0, 0))],
        out_specs=(pl.BlockSpec((tm, d), lambda i: (i, 0)),
                   pl.BlockSpec((tm, d), lambda i: (i, 0)),
                   pl.BlockSpec((tm, LANES), lambda i: (i, 0)),
                   pl.BlockSpec((tm, LANES), lambda i: (i, 0)),
                   pl.BlockSpec((tm, LANES), lambda i: (i, 0))),
        compiler_params=pltpu.CompilerParams(dimension_semantics=("arbitrary",),
                                             vmem_limit_bytes=VMEM_LIMIT),
        name="outproj",
    )(rec, att, x2, gt, sc, sh, g, wo_bf, wr_hi, wr_lo, br)


MOE_TM = 256


def _moe_kernel(te_ref, nt_ref, hs_ref, wg_ref, wu_ref, wd_ref, o_ref):
    t = pl.program_id(0)

    @pl.when(t < nt_ref[0])
    def _():
        xs = hs_ref[...].astype(BF16)
        gte = jnp.dot(xs, wg_ref[...].astype(BF16), preferred_element_type=F32)
        up = jnp.dot(xs, wu_ref[...].astype(BF16), preferred_element_type=F32)
        act = (_silu(gte) * up).astype(BF16)
        o_ref[...] = jnp.dot(act, wd_ref[...].astype(BF16), preferred_element_type=F32)

    @pl.when(t >= nt_ref[0])
    def _():
        o_ref[...] = jnp.zeros_like(o_ref)


def _moe(tile_expert, n_tiles, hs, wg, wu, wd):
    p = hs.shape[0]
    d, de = wg.shape[1], wg.shape[2]
    tm = MOE_TM
    grid_spec = pltpu.PrefetchScalarGridSpec(
        num_scalar_prefetch=2,
        grid=(p // tm,),
        in_specs=[pl.BlockSpec((tm, d), lambda t, te, nt: (jnp.minimum(t, nt[0] - 1), 0)),
                  pl.BlockSpec((None, d, de), lambda t, te, nt: (te[t], 0, 0)),
                  pl.BlockSpec((None, d, de), lambda t, te, nt: (te[t], 0, 0)),
                  pl.BlockSpec((None, de, d), lambda t, te, nt: (te[t], 0, 0))],
        out_specs=pl.BlockSpec((tm, d), lambda t, te, nt: (t, 0)),
    )
    return pl.pallas_call(
        _moe_kernel,
        out_shape=jax.ShapeDtypeStruct((p, d), F32),
        grid_spec=grid_spec,
        compiler_params=pltpu.CompilerParams(dimension_semantics=("arbitrary",),
                                             vmem_limit_bytes=VMEM_LIMIT_MOE),
        name="moe",
    )(tile_expert, n_tiles, hs, wg, wu, wd)


FIN_TM = 256
DMA_UNROLL = 8


def _final_kernel(dcur_ref, dnext_ref, x1_ref, wts_ref, gt_ref, g_ref, ys_ref, o_ref, buf, sem):
    i = pl.program_id(0)
    n = pl.num_programs(0)
    tm = x1_ref.shape[0]

    def row_copy(dref, slot, r, s):
        return pltpu.make_async_copy(ys_ref.at[pl.ds(dref[0, 0, 2 * r + s], 1)],
                                     buf.at[slot, s, pl.ds(r, 1)], sem.at[slot])

    def issue(dref, slot):
        def body(r, c):
            row_copy(dref, slot, r, 0).start()
            row_copy(dref, slot, r, 1).start()
            return c
        lax.fori_loop(0, tm, body, 0, unroll=DMA_UNROLL)

    @pl.when(i == 0)
    def _():
        issue(dcur_ref, 0)

    @pl.when(i + 1 < n)
    def _():
        issue(dnext_ref, (i + 1) % 2)

    slot = i % 2
    for s in range(2):
        pltpu.make_async_copy(ys_ref.at[pl.ds(0, tm)], buf.at[slot, s], sem.at[slot]).wait()
    w = wts_ref[...]
    y = w[:, 0:1] * buf[slot, 0] + w[:, 1:2] * buf[slot, 1]
    xo = x1_ref[...] + gt_ref[0] * y
    o_ref[...] = xo * lax.rsqrt(jnp.mean(xo * xo, axis=-1, keepdims=True) + EPS) * g_ref[...]


def _final(dest3, x1, wts, gt, g, ys, seq):
    n, d = x1.shape
    tm = FIN_TM
    per_b = seq // tm
    steps = n // tm
    row = pl.BlockSpec((tm, d), lambda i: (i, 0))
    smem = lambda f: pl.BlockSpec((1, 1, 2 * tm), f, memory_space=pltpu.SMEM)
    return pl.pallas_call(
        _final_kernel,
        out_shape=jax.ShapeDtypeStruct((n, d), F32),
        grid=(steps,),
        in_specs=[smem(lambda i: (i, 0, 0)),
                  smem(lambda i: (jnp.minimum(i + 1, steps - 1), 0, 0)),
                  row,
                  pl.BlockSpec((tm, LANES), lambda i: (i, 0)),
                  pl.BlockSpec((1, 1, d), lambda i: (i // per_b, 0, 0)),
                  pl.BlockSpec((1, d), lambda i: (0, 0)),
                  pl.BlockSpec(memory_space=pl.ANY)],
        out_specs=row,
        scratch_shapes=[pltpu.VMEM((2, 2, tm, d), F32), pltpu.SemaphoreType.DMA((2,))],
        compiler_params=pltpu.CompilerParams(dimension_semantics=("arbitrary",),
                                             vmem_limit_bytes=VMEM_LIMIT),
        name="final",
    )(dest3, dest3, x1, wts, gt, g, ys)


ROUTE_T = 512


def _route_kernel(oh0_ref, oh1_ref, dest_ref, cnt_ref, run_ref, tot_ref):
    ph = pl.program_id(0)
    i = pl.program_id(1)
    a0 = oh0_ref[...]
    a1 = oh1_ref[...]
    both = a0 + a1
    colsum = jnp.sum(both.astype(F32), axis=0, keepdims=True)

    @pl.when(jnp.logical_and(ph == 0, i == 0))
    def _():
        tot_ref[...] = jnp.zeros_like(tot_ref)

    @pl.when(ph == 0)
    def _():
        tot_ref[...] = tot_ref[...] + colsum

    @pl.when(ph == 1)
    def _():
        @pl.when(i == 0)
        def _():
            run_ref[...] = jnp.zeros_like(run_ref)

        tot = tot_ref[...]
        tiles = jnp.ceil(tot * (1.0 / MOE_TM))
        rr = lax.broadcasted_iota(I32, (LANES, LANES), 0)
        cc = lax.broadcasted_iota(I32, (LANES, LANES), 1)
        before = (rr < cc).astype(BF16)
        tiles8 = jnp.broadcast_to(tiles, (8, LANES)).astype(BF16)
        poff = jnp.dot(tiles8, before, preferred_element_type=F32)[0:1] * float(MOE_TM)
        t = a0.shape[0]
        r2 = lax.broadcasted_iota(I32, (t, t), 0)
        c2 = lax.broadcasted_iota(I32, (t, t), 1)
        earlier = (c2 < r2).astype(BF16)
        rank = jnp.dot(earlier, both, preferred_element_type=F32)
        tgt = poff + run_ref[...] + rank
        d0 = jnp.sum(a0.astype(F32) * tgt, axis=1, keepdims=True)
        d1 = jnp.sum(a1.astype(F32) * tgt, axis=1, keepdims=True)
        lane = lax.broadcasted_iota(I32, (t, LANES), 1)
        dest_ref[...] = jnp.where(lane == 0, d0, jnp.where(lane == 1, d1, 0.0)).astype(I32)
        run_ref[...] = run_ref[...] + colsum
        cnt_ref[...] = tot


def _route(oh0, oh1):
    n = oh0.shape[0]
    t = ROUTE_T
    blk = pl.BlockSpec((t, LANES), lambda ph, i: (i, 0))
    return pl.pallas_call(
        _route_kernel,
        out_shape=(jax.ShapeDtypeStruct((n, LANES), I32), jax.ShapeDtypeStruct((1, LANES), F32)),
        grid=(2, n // t),
        in_specs=[blk, blk],
        out_specs=(pl.BlockSpec((t, LANES), lambda ph, i: (i * ph, 0)),
                   pl.BlockSpec((1, LANES), lambda ph, i: (0, 0))),
        scratch_shapes=[pltpu.VMEM((1, LANES), F32), pltpu.VMEM((1, LANES), F32)],
        compiler_params=pltpu.CompilerParams(dimension_semantics=("arbitrary", "arbitrary"),
                                             vmem_limit_bytes=VMEM_LIMIT),
        name="route",
    )(oh0, oh1)


DISP_TM = 1024


def _dispatch_kernel(pad_ref, dest_ref, h_ref, hs_ref, zero_ref, sem, zsem):
    tm = h_ref.shape[0]

    @pl.when(pl.program_id(0) == 0)
    def _():
        zero_ref[...] = jnp.zeros_like(zero_ref)

        def zero_copy(row0):
            return pltpu.make_async_copy(zero_ref, hs_ref.at[pl.ds(pl.multiple_of(row0, MOE_TM), MOE_TM)], zsem)

        def fill(e, c):
            @pl.when(pad_ref[e] >= 0)
            def _():
                zero_copy(pad_ref[e]).start()
            return c

        def drain(e, c):
            @pl.when(pad_ref[e] >= 0)
            def _():
                zero_copy(pad_ref[e]).wait()
            return c

        lax.fori_loop(0, N_EXPERTS, fill, 0)
        lax.fori_loop(0, N_EXPERTS, drain, 0)
        used = pad_ref[N_EXPERTS]
        total = hs_ref.shape[0] // MOE_TM
        lax.fori_loop(used, total, lambda t, c: (zero_copy(t * MOE_TM).start(), c)[1], 0)
        lax.fori_loop(used, total, lambda t, c: (zero_copy(t * MOE_TM).wait(), c)[1], 0)

    def body(r, c):
        for s in range(2):
            pltpu.make_async_copy(h_ref.at[pl.ds(r, 1)], hs_ref.at[pl.ds(dest_ref[0, 0, 2 * r + s], 1)],
                                  sem).start()
        return c

    lax.fori_loop(0, tm, body, 0, unroll=DMA_UNROLL)
    for _ in range(2):
        pltpu.make_async_copy(h_ref, hs_ref.at[pl.ds(0, tm)], sem).wait()


def _dispatch(pad_start, dest3, h2, p_rows):
    n, w = h2.shape
    tm = DISP_TM
    grid_spec = pltpu.PrefetchScalarGridSpec(
        num_scalar_prefetch=1,
        grid=(n // tm,),
        in_specs=[pl.BlockSpec((1, 1, 2 * tm), lambda i, pad: (i, 0, 0), memory_space=pltpu.SMEM),
                  pl.BlockSpec((tm, w), lambda i, pad: (i, 0))],
        out_specs=pl.BlockSpec(memory_space=pl.ANY),
        scratch_shapes=[pltpu.VMEM((MOE_TM, w), h2.dtype), pltpu.SemaphoreType.DMA(()),
                        pltpu.SemaphoreType.DMA(())],
    )
    return pl.pallas_call(
        _dispatch_kernel,
        out_shape=jax.ShapeDtypeStruct((p_rows, w), h2.dtype),
        grid_spec=grid_spec,
        compiler_params=pltpu.CompilerParams(dimension_semantics=("arbitrary",),
                                             vmem_limit_bytes=VMEM_LIMIT),
        name="dispatch",
    )(pad_start, dest3, h2)


def _tile_tables(counts, n_tiles_max):
    cnt = counts[0, :N_EXPERTS].astype(I32)
    tiles = (cnt + MOE_TM - 1) // MOE_TM
    tiles_end = jnp.cumsum(tiles)
    tile = jnp.arange(n_tiles_max, dtype=I32)
    tile_expert = jnp.sum((tile[:, None] >= tiles_end[None, :]).astype(I32), axis=1)
    n_used = tiles_end[-1:].astype(I32)
    pad_start = jnp.where(tiles > 0, (tiles_end - 1) * MOE_TM, -1).astype(I32)
    return jnp.minimum(tile_expert, N_EXPERTS - 1), n_used, jnp.concatenate([pad_start, n_used])


def kernel(x, c, w_ada, b_ada, g_norm_mix, w_in, lb_logits, g_rec_out, g_att_out, w_out, g_norm_ffn,
           w_router_group, b_router_group, w_router_expert, b_router_expert,
           w_expert_gate, w_expert_up, w_expert_down, g_final):
    bsz, seq, d = x.shape
    n = bsz * seq
    assert w_ada.shape[0] == 1, "single trunk layer"
    layer = 0
    x2 = x.reshape(n, d)

    mod = _adaln(c, w_ada[layer], b_ada[layer])
    sh1, sc1, gt1, sh2, sc2, gt2 = [m.reshape(bsz, 1, d) for m in jnp.split(mod, 6, axis=-1)]

    proj, ptail = _inproj(x2, sc1, sh1, g_norm_mix[layer].reshape(1, d), w_in[layer], seq)
    kb, vb, kxb = _kvprep(proj, ptail)
    rec = _hgrn(proj, lb_logits, g_rec_out[layer], bsz, seq, layer)
    att = _dsa(proj, ptail, kb, vb, kxb, g_att_out[layer], bsz, seq)

    wr = jnp.concatenate([w_router_group[layer], w_router_expert[layer]], axis=1)
    wr = jnp.pad(wr, ((0, 0), (0, LANES - wr.shape[1])))
    br = jnp.concatenate([b_router_group[layer], b_router_expert[layer]])
    br = jnp.pad(br, (0, LANES - br.shape[0])).reshape(1, LANES)
    x1, h2, oh0, oh1, wts = _outproj(rec, att, x2, gt1, sc2, sh2, g_norm_ffn[layer].reshape(1, d),
                                      w_out[layer].astype(BF16), wr, br, seq)

    dest, counts = _route(oh0, oh1)
    dest2 = dest[:, :2]
    p_rows = 2 * n + N_EXPERTS * MOE_TM
    tile_expert, n_tiles, pad_start = _tile_tables(counts, p_rows // MOE_TM)
    hs = _dispatch(pad_start, dest2.reshape(n // DISP_TM, 1, 2 * DISP_TM), h2, p_rows)
    ys = _moe(tile_expert, n_tiles, hs, w_expert_gate[layer], w_expert_up[layer], w_expert_down[layer])
    out = _final(dest2.reshape(n // FIN_TM, 1, 2 * FIN_TM), x1, wts, gt2, g_final.reshape(1, d), ys, seq)
    return out.reshape(bsz, seq, d)
```

```python
import functools

import jax
import jax.numpy as jnp
import numpy as np
from jax import lax
from jax.experimental import pallas as pl
from jax.experimental.pallas import tpu as pltpu

F32 = jnp.float32
BF16 = jnp.bfloat16
I32 = jnp.int32

EPS = 1e-6
LANES = 128

REC_HEADS = 8
REC_D = 128
REC_CHUNK = 64
REC_SUB = 16
ATT_HEADS = 8
ATT_DH = 128
ATT_KV_HEADS = 2
ATT_GROUP = ATT_HEADS // ATT_KV_HEADS
IDX_HEADS = 8
IDX_DIM = 64
TOPK_MAX = 256
N_GROUPS = 4
EXPERTS_PER_GROUP = 8
N_EXPERTS = N_GROUPS * EXPERTS_PER_GROUP

OFF_RQ = 0
OFF_RF = 1024
OFF_RI = 2048
OFF_RG = 3072
OFF_AQ = 4096
OFF_AK = 5120
OFF_IQ = 5632
OFF_IK = 6144
IN_COLS = 6216

VMEM_LIMIT = 48 * 1024 * 1024
VMEM_LIMIT_MOE = 56 * 1024 * 1024

INT_MIN = -(2 ** 31)
KEY_NEGINF = int(np.array(-np.inf, np.float32).view(np.int32)) ^ 0x7FFFFFFF
NEG_BIG = -1e30


def _silu(v):
    return v * jax.nn.sigmoid(v)


def _nt_dot(a, b):
    return lax.dot_general(a, b, (((1,), (1,)), ((), ())), preferred_element_type=F32)


def _tn_dot(a, b):
    return lax.dot_general(a, b, (((0,), (0,)), ((), ())), preferred_element_type=F32)


def _adaln_kernel(c_ref, w_ref, b_ref, o_ref):
    ca = _silu(c_ref[...])
    o_ref[...] = jnp.dot(ca, w_ref[...], preferred_element_type=F32,
                         precision=lax.Precision.HIGHEST) + b_ref[...]


def _adaln(c, w, b):
    bsz, d = c.shape
    n = w.shape[1]
    tn = 512
    return pl.pallas_call(
        _adaln_kernel,
        out_shape=jax.ShapeDtypeStruct((bsz, n), F32),
        grid=(n // tn,),
        in_specs=[pl.BlockSpec((bsz, d), lambda j: (0, 0)),
                  pl.BlockSpec((d, tn), lambda j: (0, j)),
                  pl.BlockSpec((1, tn), lambda j: (0, j))],
        out_specs=pl.BlockSpec((bsz, tn), lambda j: (0, j)),
        compiler_params=pltpu.CompilerParams(dimension_semantics=("arbitrary",),
                                             vmem_limit_bytes=VMEM_LIMIT),
        name="adaln",
    )(c, w, b.reshape(1, n))


def _norm_mod(x, g, sc, sh):
    xn = x * lax.rsqrt(jnp.mean(x * x, axis=-1, keepdims=True) + EPS)
    return xn * g * (1.0 + sc) + sh


IN_RB = 256


def _inproj_kernel(x_ref, sc_ref, sh_ref, g_ref, w_ref, wt_ref, o_ref, ot_ref, h_ref):
    w = w_ref[...]

    @pl.when(pl.program_id(1) == 0)
    def _():
        tm = x_ref.shape[0]
        rb = min(IN_RB, tm)
        hs = []
        for r in range(0, tm, rb):
            h = _norm_mod(x_ref[r:r + rb, :], g_ref[...], sc_ref[0], sh_ref[0]).astype(BF16)
            h_ref[r:r + rb, :] = h
            hs.append(h)
        wt = wt_ref[...]
        for i, r in enumerate(range(0, tm, rb)):
            o_ref[r:r + rb, :] = jnp.dot(hs[i], w, preferred_element_type=F32)
            ot_ref[r:r + rb, :] = jnp.dot(hs[i], wt, preferred_element_type=F32)

    @pl.when(pl.program_id(1) != 0)
    def _():
        o_ref[...] = jnp.dot(h_ref[...], w, preferred_element_type=F32)


def _inproj(x2, sc, sh, g, w_in, seq):
    n, d = x2.shape
    tm = min(1024, seq)
    tn = 1024
    per_b = seq // tm
    w_tail = jnp.pad(w_in[:, OFF_IK:], ((0, 0), (0, LANES - (IN_COLS - OFF_IK)))).astype(BF16)
    w_bf = w_in.astype(BF16)
    return pl.pallas_call(
        _inproj_kernel,
        out_shape=(jax.ShapeDtypeStruct((n, OFF_IK), F32), jax.ShapeDtypeStruct((n, LANES), F32)),
        grid=(n // tm, OFF_IK // tn),
        in_specs=[pl.BlockSpec((tm, d), lambda i, j: (i, 0)),
                  pl.BlockSpec((1, 1, d), lambda i, j: (i // per_b, 0, 0)),
                  pl.BlockSpec((1, 1, d), lambda i, j: (i // per_b, 0, 0)),
                  pl.BlockSpec((1, d), lambda i, j: (0, 0)),
                  pl.BlockSpec((d, tn), lambda i, j: (0, j)),
                  pl.BlockSpec((d, LANES), lambda i, j: (0, 0))],
        out_specs=(pl.BlockSpec((tm, tn), lambda i, j: (i, j)),
                   pl.BlockSpec((tm, LANES), lambda i, j: (i, 0))),
        scratch_shapes=[pltpu.VMEM((tm, d), BF16)],
        compiler_params=pltpu.CompilerParams(dimension_semantics=("arbitrary", "arbitrary"),
                                             vmem_limit_bytes=VMEM_LIMIT),
        name="inproj",
    )(x2, sc, sh, g, w_bf, w_tail)


VT_ROWS = ATT_DH + 16


def _kvprep_kernel(kv_ref, ik_ref, k_ref, vt_ref, kx_ref):
    kv = kv_ref[...]
    tm = kv.shape[0]
    k_ref[...] = kv[:, :256].astype(BF16)
    tail = jnp.where(lax.broadcasted_iota(I32, (16, tm), 0) == 0, 1.0, 0.0)
    for g in range(ATT_KV_HEADS):
        vt = kv[:, 256 + g * ATT_DH:256 + (g + 1) * ATT_DH].T
        vt_ref[g] = jnp.concatenate([vt, tail], axis=0).astype(BF16)
    kx_ref[...] = ik_ref[...][:, :IDX_DIM].astype(BF16)


def _kvprep(proj, ptail):
    n = proj.shape[0]
    tm = 512
    return pl.pallas_call(
        _kvprep_kernel,
        out_shape=(jax.ShapeDtypeStruct((n, 256), BF16),
                   jax.ShapeDtypeStruct((ATT_KV_HEADS, VT_ROWS, n), BF16),
                   jax.ShapeDtypeStruct((n, IDX_DIM), BF16)),
        grid=(n // tm,),
        in_specs=[pl.BlockSpec((tm, 512), lambda i: (i, OFF_AK // 512)),
                  pl.BlockSpec((tm, LANES), lambda i: (i, 0))],
        out_specs=(pl.BlockSpec((tm, 256), lambda i: (i, 0)),
                   pl.BlockSpec((ATT_KV_HEADS, VT_ROWS, tm), lambda i: (0, 0, i)),
                   pl.BlockSpec((tm, IDX_DIM), lambda i: (i, 0))),
        compiler_params=pltpu.CompilerParams(dimension_semantics=("arbitrary",),
                                             vmem_limit_bytes=VMEM_LIMIT),
        name="kvprep",
    )(proj, ptail)


def _hgrn_kernel(q_ref, f_ref, i_ref, g_ref, lbl_ref, gout_ref, o_ref, st_ref, *, chunks, layer):
    @pl.when(pl.program_id(2) == 0)
    def _():
        st_ref[...] = jnp.zeros_like(st_ref)

    lbl = lbl_ref[...]
    e = jnp.exp(lbl - jnp.max(lbl, axis=0, keepdims=True))
    sm = e / jnp.sum(e, axis=0, keepdims=True)
    lb = jnp.sum(sm[: layer + 1], axis=0, keepdims=True)
    gout = gout_ref[...]

    c = REC_CHUNK
    nsub = c // REC_SUB
    rr = lax.broadcasted_iota(I32, (c, c), 0)
    cc = lax.broadcasted_iota(I32, (c, c), 1)
    row = lax.broadcasted_iota(I32, (c, REC_D), 0)
    sub = lax.shift_right_logical(row, REC_SUB.bit_length() - 1)

    f = lb + (1.0 - lb) * jax.nn.sigmoid(f_ref[...])
    logf = jnp.log2(f)
    k = 1.0 - f
    qf = _silu(q_ref[...]) * (REC_D ** -0.5)
    vb = i_ref[...].astype(BF16)

    halves = [1 << l for l in range(REC_SUB.bit_length() - 1)]
    logf_w = jnp.concatenate([logf[ci * c:(ci + 1) * c] for ci in range(chunks)], axis=1)
    b_all = jnp.dot((rr >= cc).astype(F32), logf_w, preferred_element_type=F32,
                    precision=lax.Precision.HIGHEST)
    row_w = lax.broadcasted_iota(I32, b_all.shape, 0)
    blk_end, split = b_all, {}
    for h in halves:
        first = (row_w & (2 * h - 1)) < h
        split[h] = jnp.where(first, blk_end, pltpu.roll(blk_end, h, 0))
        blk_end = jnp.where(first, pltpu.roll(blk_end, c - h, 0), blk_end)
    lvl_mask = [jnp.logical_and((rr ^ cc) < 2 * h, jnp.logical_and((rr & h) != 0, (cc & h) == 0)) for h in halves]
    eye = rr == cc
    in_second = {h: (row & h) != 0 for h in halves}
    after_sub = [sub > i for i in range(nsub - 1)]
    in_sub = [sub == i for i in range(nsub - 1)]

    qxs, kxs, qls, kls, diag, upd, q_in, decay = [], [], [], [], [], [], [], []
    for ci in range(chunks):
        sl = slice(ci * c, (ci + 1) * c)
        cols = slice(ci * REC_D, (ci + 1) * REC_D)
        b = b_all[:, cols]
        kc, qc = k[sl], qf[sl]
        qparts, kparts = [], []
        for i in range(nsub - 1):
            r = b[(i + 1) * REC_SUB - 1:(i + 1) * REC_SUB, :]
            qparts.append(jnp.where(after_sub[i], qc * jnp.exp2(b - r), 0.0))
            kparts.append(jnp.where(in_sub[i], kc * jnp.exp2(r - b), 0.0))
        qxs.append(jnp.concatenate(qparts, axis=1).astype(BF16))
        kxs.append(jnp.concatenate(kparts, axis=1).astype(BF16))
        for h in halves:
            e = jnp.exp2(-jnp.abs(b - split[h][:, cols]))
            qls.append(jnp.where(in_second[h], qc * e, 0.0).astype(BF16))
            kls.append(jnp.where(in_second[h], 0.0, kc * e).astype(BF16))
        diag.append(jnp.sum(qc * kc, axis=1, keepdims=True))
        b_end = b[c - 1:c, :]
        upd.append((kc * jnp.exp2(b_end - b)).astype(BF16))
        q_in.append((qc * jnp.exp2(b)).astype(BF16))
        decay.append(jnp.exp2(b_end))
    nl = len(halves)
    cross = [_nt_dot(qxs[ci], kxs[ci]) for ci in range(chunks)]
    within = [[_nt_dot(qls[ci * nl + l], kls[ci * nl + l]) for l in range(nl)] for ci in range(chunks)]
    scores = []
    for ci in range(chunks):
        s = cross[ci] + jnp.where(eye, diag[ci], 0.0)
        for l in range(nl):
            s = s + jnp.where(lvl_mask[l], within[ci][l], 0.0)
        scores.append(s.astype(BF16))
    upd = [_tn_dot(vb[ci * c:(ci + 1) * c], upd[ci]) for ci in range(chunks)]
    intra = [jnp.dot(scores[ci], vb[ci * c:(ci + 1) * c], preferred_element_type=F32) for ci in range(chunks)]

    st = st_ref[...]
    outs = []
    for ci in range(chunks):
        outs.append(_nt_dot(q_in[ci], st.astype(BF16)) + intra[ci])
        st = st * decay[ci] + upd[ci]
    st_ref[...] = st

    o = jnp.concatenate(outs, axis=0)
    on = o * lax.rsqrt(jnp.mean(o * o, axis=-1, keepdims=True) + EPS)
    o_ref[...] = (on * gout * _silu(g_ref[...])).astype(o_ref.dtype)


def _hgrn(proj, lb_logits, g_out, bsz, seq, layer):
    n = proj.shape[0]
    tc = min(2048, seq)
    per_b = seq // tc
    nl = lb_logits.shape[0]

    def col(off):
        return lambda b, h, c: (b * per_b + c, off // REC_D + h)

    return pl.pallas_call(
        functools.partial(_hgrn_kernel, chunks=tc // REC_CHUNK, layer=layer),
        out_shape=jax.ShapeDtypeStruct((n, REC_HEADS * REC_D), BF16),
        grid=(bsz, REC_HEADS, per_b),
        in_specs=[pl.BlockSpec((tc, REC_D), col(OFF_RQ)),
                  pl.BlockSpec((tc, REC_D), col(OFF_RF)),
                  pl.BlockSpec((tc, REC_D), col(OFF_RI)),
                  pl.BlockSpec((tc, REC_D), col(OFF_RG)),
                  pl.BlockSpec((nl, REC_D), lambda b, h, c: (0, h)),
                  pl.BlockSpec((1, REC_D), lambda b, h, c: (0, h))],
        out_specs=pl.BlockSpec((tc, REC_D), lambda b, h, c: (b * per_b + c, h)),
        scratch_shapes=[pltpu.VMEM((REC_D, REC_D), F32)],
        compiler_params=pltpu.CompilerParams(
            dimension_semantics=("arbitrary", "arbitrary", "arbitrary"),
            vmem_limit_bytes=VMEM_LIMIT),
        name="hgrn2",
    )(proj, proj, proj, proj, lb_logits, g_out.reshape(1, -1))


TQ = 128
TK = 1024
TK_TAIL = TK // 2
LOG2E = 1.4426950408889634
CNT_ROWS = 64
CNT_BLOCK = 256
SETTLE_EVERY = 4
VALUE_SPLITS = 4


def _dsa_kernel(qi_ref, iw_ref, aq_ref, kx_ref, k_ref, vt_ref, gout_ref, o_ref,
                sc_ref, m_ref, acc_ref, *, ksel, seq):
    t0 = pl.program_id(1) * TQ
    need = t0 + TQ
    nfull = (need + TK_TAIL - 1) // TK
    tail0 = pl.multiple_of(nfull * TK, TK)
    has_tail = need > tail0
    nhalf = 2 * nfull + has_tail.astype(I32)
    qpos = t0 + lax.broadcasted_iota(I32, (1, TQ), 1)
    krow = lax.broadcasted_iota(I32, (TK, TQ), 0)

    def over_tiles(body):
        lax.fori_loop(0, nfull, lambda kt, c: (body(pl.multiple_of(kt * TK, TK), TK), c)[1], 0)

        @pl.when(has_tail)
        def _():
            body(tail0, TK_TAIL)

    qit = (qi_ref[...] * (IDX_DIM ** -0.5)).T
    qht = jnp.concatenate([qit[h * IDX_DIM:(h + 1) * IDX_DIM] for h in range(IDX_HEADS)],
                          axis=1).astype(BF16)
    iwt = iw_ref[...].T
    wrow = [iwt[IDX_DIM + h:IDX_DIM + h + 1] * (IDX_HEADS ** -0.5) for h in range(IDX_HEADS)]

    def score_body(k0, rows):
        rel = jnp.dot(kx_ref[pl.ds(k0, rows), :], qht, preferred_element_type=F32)
        sc = jnp.zeros((rows, TQ), F32)
        for h in range(IDX_HEADS):
            sc = sc + wrow[h] * jnp.maximum(rel[:, h * TQ:(h + 1) * TQ], 0.0)
        sc_ref[pl.ds(k0, rows), :] = jnp.where(k0 + krow[:rows] <= qpos, sc, -jnp.inf)

    over_tiles(score_body)

    def u_to_float(u):
        key = u ^ INT_MIN
        return lax.bitcast_convert_type(key ^ ((key >> 31) & 0x7FFFFFFF), F32)

    def float_to_u(f):
        bits = lax.bitcast_convert_type(f, I32)
        return bits ^ ((bits >> 31) & 0x7FFFFFFF) ^ INT_MIN

    def count(pred):
        def block(k0, rows, acc):
            for u in range(0, rows, CNT_BLOCK):
                r0 = pl.multiple_of(k0 + u, CNT_BLOCK)
                hit = pred(sc_ref[pl.ds(r0, CNT_BLOCK), :], r0 + krow[:CNT_BLOCK])
                acc = acc + jnp.sum(jnp.where(hit, 1.0, 0.0).reshape(CNT_BLOCK // CNT_ROWS, CNT_ROWS, TQ), axis=0)
            return acc
        acc = lax.fori_loop(0, nfull, lambda kt, a: block(kt * TK, TK, a), jnp.zeros((CNT_ROWS, TQ), F32))
        acc = lax.cond(has_tail, lambda a: block(tail0, TK_TAIL, a), lambda a: a, acc)
        return jnp.sum(acc, axis=0, keepdims=True)

    def group_max(j, gm):
        r0 = pl.multiple_of(j * ksel, ksel)
        return jnp.maximum(gm, sc_ref[pl.ds(r0, ksel), :])

    gm = lax.fori_loop(0, nhalf * (TK_TAIL // ksel), group_max, jnp.full((ksel, TQ), -jnp.inf, F32))
    kf = float(ksel)

    def ult(a, b):
        return (a ^ INT_MIN) < (b ^ INT_MIN)

    def is_open(lo, hi):
        return ult(jnp.int32(1), hi - lo)

    def not_settled(cnt):
        return jnp.where(cnt == kf, 0.0, 1.0)

    def search_cond(st):
        _, lo, hi, cnt = st
        return jnp.max(jnp.where(is_open(lo, hi), not_settled(cnt), 0.0)) > 0.0

    def search_step(_, st):
        it, lo, hi, cnt = st
        mid_u = lo + lax.shift_right_logical(hi - lo, 1)
        mid_v = float_to_u(0.5 * (u_to_float(lo) + u_to_float(hi)))
        use_v = jnp.logical_and(it < VALUE_SPLITS, jnp.logical_and(ult(lo, mid_v), ult(mid_v, hi)))
        cand = jnp.where(is_open(lo, hi), jnp.where(use_v, mid_v, mid_u), lo)
        cand_f = u_to_float(cand)
        c = count(lambda s, pos: s >= cand_f)
        take = c >= kf
        return it + 1, jnp.where(take, cand, lo), jnp.where(take, hi, cand), jnp.where(take, c, cnt)

    def search_body(st):
        return lax.fori_loop(0, SETTLE_EVERY, search_step, st)

    _, u_thr, _, cnt_thr = lax.while_loop(
        search_cond, search_body,
        (jnp.int32(0), float_to_u(jnp.min(gm, axis=0, keepdims=True)),
         float_to_u(jnp.max(gm, axis=0, keepdims=True)) + 1, jnp.full((1, TQ), -1.0, F32)))
    thr = u_to_float(u_thr)
    real = (u_thr ^ INT_MIN) > KEY_NEGINF

    nbits = seq.bit_length()

    def resolve_ties():
        c_gt = count(lambda s, pos: s > thr)
        c_eq = count(lambda s, pos: s == thr)
        need = kf - c_gt
        excess = jnp.logical_and(c_eq > need, real)

        def find_last():
            def jbody(i, y):
                cand = y | lax.shift_left(jnp.int32(1), (nbits - 1) - i)
                below = count(lambda s, pos: jnp.logical_and(s == thr, pos < cand))
                return jnp.where(below <= need - 1.0, cand, y)
            return lax.fori_loop(0, nbits, jbody, jnp.zeros((1, TQ), I32))

        return lax.cond(jnp.max(jnp.where(excess, 1.0, 0.0)) > 0.0, find_last,
                        lambda: jnp.full((1, TQ), seq, I32))

    last = lax.cond(jnp.max(not_settled(cnt_thr)) > 0.0, resolve_ties, lambda: jnp.full((1, TQ), seq, I32))
    last = jnp.where(real, last, -1)
    thr_m = jnp.where(real, thr, -jnp.inf)

    aq = aq_ref[...] * ((ATT_DH ** -0.5) * LOG2E)
    qgt = []
    for g in range(ATT_KV_HEADS):
        blk = [aq[:, (g * ATT_GROUP + j) * ATT_DH:(g * ATT_GROUP + j + 1) * ATT_DH].T for j in range(ATT_GROUP)]
        qgt.append(jnp.concatenate(blk, axis=1).astype(BF16))

    m_ref[...] = jnp.full(m_ref.shape, NEG_BIG, F32)
    acc_ref[...] = jnp.zeros(acc_ref.shape, F32)

    def att_body(k0, rows):
        sc = sc_ref[pl.ds(k0, rows), :]
        sel = jnp.logical_or(sc > thr_m, jnp.logical_and(sc == thr_m, k0 + krow[:rows] <= last))
        bias = jnp.where(sel, 0.0, NEG_BIG)
        bias = jnp.concatenate([bias] * ATT_GROUP, axis=1)
        heads = range(ATT_KV_HEADS)
        ss = [jnp.dot(k_ref[pl.ds(k0, rows), g * ATT_DH:(g + 1) * ATT_DH], qgt[g],
                      preferred_element_type=F32) + bias for g in heads]
        ps, alphas = [], []
        for g in heads:
            m_old = m_ref[g]
            m_new = jnp.maximum(m_old, jnp.max(ss[g], axis=0, keepdims=True))
            m_ref[g] = m_new
            alphas.append(jnp.exp2(m_old - m_new))
            ps.append(jnp.exp2(ss[g] - m_new).astype(BF16))
        pvs = [jnp.dot(vt_ref[g, :, pl.ds(k0, rows)], ps[g], preferred_element_type=F32) for g in heads]
        for g in heads:
            acc_ref[g] = alphas[g] * acc_ref[g] + pvs[g]

    over_tiles(att_body)

    gout = gout_ref[...]
    for g in range(ATT_KV_HEADS):
        a = acc_ref[g]
        o = a[:ATT_DH] / a[ATT_DH:ATT_DH + 1]
        on = o * lax.rsqrt(jnp.mean(o * o, axis=0, keepdims=True) + EPS)
        for j in range(ATT_GROUP):
            hsl = slice((g * ATT_GROUP + j) * ATT_DH, (g * ATT_GROUP + j + 1) * ATT_DH)
            o_ref[:, hsl] = (on[:, j * TQ:(j + 1) * TQ].T * gout[:, hsl]).astype(o_ref.dtype)


def _dsa(proj, ptail, kb, vt, kxb, g_out, bsz, seq):
    n = proj.shape[0]
    nqb = seq // TQ
    ksel = min(TOPK_MAX, seq // 4)
    assert TK_TAIL % ksel == 0 and TK_TAIL % CNT_BLOCK == 0 and seq % TK == 0
    kb3 = kb.reshape(bsz, seq, 256)
    kx3 = kxb.reshape(bsz, seq, IDX_DIM)
    gq = ATT_GROUP * TQ
    return pl.pallas_call(
        functools.partial(_dsa_kernel, ksel=ksel, seq=seq),
        out_shape=jax.ShapeDtypeStruct((n, ATT_HEADS * ATT_DH), BF16),
        grid=(bsz, nqb),
        in_specs=[pl.BlockSpec((TQ, 512), lambda b, q: (b * nqb + q, OFF_IQ // 512)),
                  pl.BlockSpec((TQ, LANES), lambda b, q: (b * nqb + q, 0)),
                  pl.BlockSpec((TQ, 1024), lambda b, q: (b * nqb + q, OFF_AQ // 1024)),
                  pl.BlockSpec((None, seq, IDX_DIM), lambda b, q: (b, 0, 0)),
                  pl.BlockSpec((None, seq, 256), lambda b, q: (b, 0, 0)),
                  pl.BlockSpec((ATT_KV_HEADS, VT_ROWS, seq), lambda b, q: (0, 0, b)),
                  pl.BlockSpec((1, ATT_HEADS * ATT_DH), lambda b, q: (0, 0))],
        out_specs=pl.BlockSpec((TQ, ATT_HEADS * ATT_DH), lambda b, q: (b * nqb + q, 0)),
        scratch_shapes=[pltpu.VMEM((seq, TQ), F32),
                        pltpu.VMEM((ATT_KV_HEADS, 1, gq), F32),
                        pltpu.VMEM((ATT_KV_HEADS, VT_ROWS, gq), F32)],
        compiler_params=pltpu.CompilerParams(dimension_semantics=("arbitrary", "arbitrary"),
                                             vmem_limit_bytes=VMEM_LIMIT),
        name="dsa",
    )(proj, ptail, proj, kx3, kb3, vt, g_out.reshape(1, -1))


OUT_RB = 128


def _outproj_kernel(rec_ref, att_ref, x_ref, gt_ref, sc_ref, sh_ref, g_ref, wo_ref, wrh_ref, wrl_ref, br_ref,
                    x1_ref, h2_ref, oh0_ref, oh1_ref, wts_ref):
    wo = wo_ref[...]
    half = rec_ref.shape[1]
    tm = x_ref.shape[0]
    blocks = [slice(r, r + OUT_RB) for r in range(0, tm, OUT_RB)]
    mixed = [jnp.dot(rec_ref[rs, :], wo[:half], preferred_element_type=F32)
             + jnp.dot(att_ref[rs, :], wo[half:], preferred_element_type=F32) for rs in blocks]
    his, los = [], []
    for rs, mx in zip(blocks, mixed):
        x1 = x_ref[rs, :] + gt_ref[0] * mx
        x1_ref[rs, :] = x1
        h2 = _norm_mod(x1, g_ref[...], sc_ref[0], sh_ref[0])
        h2_ref[rs, :] = h2
        hi = h2.astype(BF16)
        his.append(hi)
        los.append((h2 - hi.astype(F32)).astype(BF16))
    wrh, wrl = wrh_ref[...], wrl_ref[...]
    logits = jnp.concatenate(
        [jnp.dot(hi, wrh, preferred_element_type=F32) + jnp.dot(lo, wrh, preferred_element_type=F32)
         + jnp.dot(hi, wrl, preferred_element_type=F32) for hi, lo in zip(his, los)], axis=0) + br_ref[...]
    lane = lax.broadcasted_iota(I32, (tm, LANES), 1)
    big = jnp.int32(LANES)

    def argmax_first(vals, mask):
        mv = jnp.where(mask, vals, -jnp.inf)
        top = jnp.max(mv, axis=1, keepdims=True)
        idx = jnp.min(jnp.where(jnp.logical_and(mask, mv == top), lane, big), axis=1, keepdims=True)
        return top, idx

    gmask = lane < N_GROUPS
    gtop, gidx = argmax_first(logits, gmask)
    p_g = 1.0 / jnp.sum(jnp.where(gmask, jnp.exp(logits - gtop), 0.0), axis=1, keepdims=True)
    e_lo = N_GROUPS + gidx * EXPERTS_PER_GROUP
    emask = jnp.logical_and(lane >= e_lo, lane < e_lo + EXPERTS_PER_GROUP)
    v1, i1 = argmax_first(logits, emask)
    v2, i2 = argmax_first(logits, jnp.logical_and(emask, lane != i1))
    r = jnp.exp(v2 - v1)
    w1 = p_g / (1.0 + r)
    w2 = p_g * r / (1.0 + r)
    oh0_ref[...] = jnp.where(lane + N_GROUPS == i1, 1.0, 0.0).astype(BF16)
    oh1_ref[...] = jnp.where(lane + N_GROUPS == i2, 1.0, 0.0).astype(BF16)
    wts_ref[...] = jnp.where(lane == 0, w1, jnp.where(lane == 1, w2, 0.0))


def _outproj(rec, att, x2, gt, sc, sh, g, wo_bf, wr, br, seq):
    n, d = x2.shape
    wr_hi = wr.astype(BF16)
    wr_lo = (wr - wr_hi.astype(F32)).astype(BF16)
    tm = min(512, seq)
    per_b = seq // tm
    half = rec.shape[1]
    bspec = pl.BlockSpec((1, 1, d), lambda i: (i // per_b, 0, 0))
    return pl.pallas_call(
        _outproj_kernel,
        out_shape=(jax.ShapeDtypeStruct((n, d), F32),
                   jax.ShapeDtypeStruct((n, d), F32),
                   jax.ShapeDtypeStruct((n, LANES), BF16),
                   jax.ShapeDtypeStruct((n, LANES), BF16),
                   jax.ShapeDtypeStruct((n, LANES), F32)),
        grid=(n // tm,),
        in_specs=[pl.BlockSpec((tm, half), lambda i: (i, 0)),
                  pl.BlockSpec((tm, half), lambda i: (i, 0)),
                  pl.BlockSpec((tm, d), lambda i: (i, 0)),
                  bspec, bspec, bspec,
                  pl.BlockSpec((1, d), lambda i: (0, 0)),
                  pl.BlockSpec((2 * half, d), lambda i: (0, 0)),
                  pl.BlockSpec((d, LANES), lambda i: (0, 0)),
                  pl.BlockSpec((d, LANES), lambda i: (0, 0)),
                  pl.BlockSpec((1, LANES), lambda i: (0, 0))],
        out_specs=(pl.BlockSpec((tm, d), lambda i: (i, 0)),
                   pl.BlockSpec((tm, d), lambda i: (i, 0)),
                   pl.BlockSpec((tm, LANES), lambda i: (i, 0)),
                   pl.BlockSpec((tm, LANES), lambda i: (i, 0)),
                   pl.BlockSpec((tm, LANES), lambda i: (i, 0))),
        compiler_params=pltpu.CompilerParams(dimension_semantics=("arbitrary",),
                                             vmem_limit_bytes=VMEM_LIMIT),
        name="outproj",
    )(rec, att, x2, gt, sc, sh, g, wo_bf, wr_hi, wr_lo, br)


MOE_TM = 256


def _moe_kernel(te_ref, nt_ref, hs_ref, wg_ref, wu_ref, wd_ref, o_ref):
    t = pl.program_id(0)

    @pl.when(t < nt_ref[0])
    def _():
        xs = hs_ref[...].astype(BF16)
        gte = jnp.dot(xs, wg_ref[...].astype(BF16), preferred_element_type=F32)
        up = jnp.dot(xs, wu_ref[...].astype(BF16), preferred_element_type=F32)
        act = (_silu(gte) * up).astype(BF16)
        o_ref[...] = jnp.dot(act, wd_ref[...].astype(BF16), preferred_element_type=F32)

    @pl.when(t >= nt_ref[0])
    def _():
        o_ref[...] = jnp.zeros_like(o_ref)


def _moe(tile_expert, n_tiles, hs, wg, wu, wd):
    p = hs.shape[0]
    d, de = wg.shape[1], wg.shape[2]
    tm = MOE_TM
    grid_spec = pltpu.PrefetchScalarGridSpec(
        num_scalar_prefetch=2,
        grid=(p // tm,),
        in_specs=[pl.BlockSpec((tm, d), lambda t, te, nt: (jnp.minimum(t, nt[0] - 1), 0)),
                  pl.BlockSpec((None, d, de), lambda t, te, nt: (te[t], 0, 0)),
                  pl.BlockSpec((None, d, de), lambda t, te, nt: (te[t], 0, 0)),
                  pl.BlockSpec((None, de, d), lambda t, te, nt: (te[t], 0, 0))],
        out_specs=pl.BlockSpec((tm, d), lambda t, te, nt: (t, 0)),
    )
    return pl.pallas_call(
        _moe_kernel,
        out_shape=jax.ShapeDtypeStruct((p, d), F32),
        grid_spec=grid_spec,
        compiler_params=pltpu.CompilerParams(dimension_semantics=("arbitrary",),
                                             vmem_limit_bytes=VMEM_LIMIT_MOE),
        name="moe",
    )(tile_expert, n_tiles, hs, wg, wu, wd)


FIN_TM = 256
DMA_UNROLL = 8


def _final_kernel(dcur_ref, dnext_ref, x1_ref, wts_ref, gt_ref, g_ref, ys_ref, o_ref, buf, sem):
    i = pl.program_id(0)
    n = pl.num_programs(0)
    tm = x1_ref.shape[0]

    def row_copy(dref, slot, r, s):
        return pltpu.make_async_copy(ys_ref.at[pl.ds(dref[0, 0, 2 * r + s], 1)],
                                     buf.at[slot, s, pl.ds(r, 1)], sem.at[slot])

    def issue(dref, slot):
        def body(r, c):
            row_copy(dref, slot, r, 0).start(priority=0)
            row_copy(dref, slot, r, 1).start(priority=1)
            return c
        lax.fori_loop(0, tm, body, 0, unroll=DMA_UNROLL)

    @pl.when(i == 0)
    def _():
        issue(dcur_ref, 0)

    @pl.when(i + 1 < n)
    def _():
        issue(dnext_ref, (i + 1) % 2)

    slot = i % 2
    for s in range(2):
        pltpu.make_async_copy(ys_ref.at[pl.ds(0, tm)], buf.at[slot, s], sem.at[slot]).wait()
    w = wts_ref[...]
    y = w[:, 0:1] * buf[slot, 0] + w[:, 1:2] * buf[slot, 1]
    xo = x1_ref[...] + gt_ref[0] * y
    o_ref[...] = xo * lax.rsqrt(jnp.mean(xo * xo, axis=-1, keepdims=True) + EPS) * g_ref[...]


def _final(dest3, x1, wts, gt, g, ys, seq):
    n, d = x1.shape
    tm = FIN_TM
    per_b = seq // tm
    steps = n // tm
    row = pl.BlockSpec((tm, d), lambda i: (i, 0))
    smem = lambda f: pl.BlockSpec((1, 1, 2 * tm), f, memory_space=pltpu.SMEM)
    return pl.pallas_call(
        _final_kernel,
        out_shape=jax.ShapeDtypeStruct((n, d), F32),
        grid=(steps,),
        in_specs=[smem(lambda i: (i, 0, 0)),
                  smem(lambda i: (jnp.minimum(i + 1, steps - 1), 0, 0)),
                  row,
                  pl.BlockSpec((tm, LANES), lambda i: (i, 0)),
                  pl.BlockSpec((1, 1, d), lambda i: (i // per_b, 0, 0)),
                  pl.BlockSpec((1, d), lambda i: (0, 0)),
                  pl.BlockSpec(memory_space=pl.ANY)],
        out_specs=row,
        scratch_shapes=[pltpu.VMEM((2, 2, tm, d), F32), pltpu.SemaphoreType.DMA((2,))],
        compiler_params=pltpu.CompilerParams(dimension_semantics=("arbitrary",),
                                             vmem_limit_bytes=VMEM_LIMIT),
        name="final",
    )(dest3, dest3, x1, wts, gt, g, ys)


ROUTE_T = 512


def _route_kernel(oh0_ref, oh1_ref, dest_ref, cnt_ref, run_ref, tot_ref):
    ph = pl.program_id(0)
    i = pl.program_id(1)
    a0 = oh0_ref[...]
    a1 = oh1_ref[...]
    both = a0 + a1
    colsum = jnp.sum(both.astype(F32), axis=0, keepdims=True)

    @pl.when(jnp.logical_and(ph == 0, i == 0))
    def _():
        tot_ref[...] = jnp.zeros_like(tot_ref)

    @pl.when(ph == 0)
    def _():
        tot_ref[...] = tot_ref[...] + colsum

    @pl.when(ph == 1)
    def _():
        @pl.when(i == 0)
        def _():
            run_ref[...] = jnp.zeros_like(run_ref)

        tot = tot_ref[...]
        tiles = jnp.ceil(tot * (1.0 / MOE_TM))
        rr = lax.broadcasted_iota(I32, (LANES, LANES), 0)
        cc = lax.broadcasted_iota(I32, (LANES, LANES), 1)
        before = (rr < cc).astype(BF16)
        tiles8 = jnp.broadcast_to(tiles, (8, LANES)).astype(BF16)
        poff = jnp.dot(tiles8, before, preferred_element_type=F32)[0:1] * float(MOE_TM)
        t = a0.shape[0]
        r2 = lax.broadcasted_iota(I32, (t, t), 0)
        c2 = lax.broadcasted_iota(I32, (t, t), 1)
        earlier = (c2 < r2).astype(BF16)
        rank = jnp.dot(earlier, both, preferred_element_type=F32)
        tgt = poff + run_ref[...] + rank
        d0 = jnp.sum(a0.astype(F32) * tgt, axis=1, keepdims=True)
        d1 = jnp.sum(a1.astype(F32) * tgt, axis=1, keepdims=True)
        lane = lax.broadcasted_iota(I32, (t, LANES), 1)
        dest_ref[...] = jnp.where(lane == 0, d0, jnp.where(lane == 1, d1, 0.0)).astype(I32)
        run_ref[...] = run_ref[...] + colsum
        cnt_ref[...] = tot


def _route(oh0, oh1):
    n = oh0.shape[0]
    t = ROUTE_T
    blk = pl.BlockSpec((t, LANES), lambda ph, i: (i, 0))
    return pl.pallas_call(
        _route_kernel,
        out_shape=(jax.ShapeDtypeStruct((n, LANES), I32), jax.ShapeDtypeStruct((1, LANES), F32)),
        grid=(2, n // t),
        in_specs=[blk, blk],
        out_specs=(pl.BlockSpec((t, LANES), lambda ph, i: (i * ph, 0)),
                   pl.BlockSpec((1, LANES), lambda ph, i: (0, 0))),
        scratch_shapes=[pltpu.VMEM((1, LANES), F32), pltpu.VMEM((1, LANES), F32)],
        compiler_params=pltpu.CompilerParams(dimension_semantics=("arbitrary", "arbitrary"),
                                             vmem_limit_bytes=VMEM_LIMIT),
        name="route",
    )(oh0, oh1)


DISP_TM = 1024


def _dispatch_kernel(pad_ref, dest_ref, h_ref, hs_ref, zero_ref, sem, zsem):
    tm = h_ref.shape[0]

    @pl.when(pl.program_id(0) == 0)
    def _():
        zero_ref[...] = jnp.zeros_like(zero_ref)

        def zero_copy(row0):
            return pltpu.make_async_copy(zero_ref, hs_ref.at[pl.ds(pl.multiple_of(row0, MOE_TM), MOE_TM)], zsem)

        def fill(e, c):
            @pl.when(pad_ref[e] >= 0)
            def _():
                zero_copy(pad_ref[e]).start()
            return c

        def drain(e, c):
            @pl.when(pad_ref[e] >= 0)
            def _():
                zero_copy(pad_ref[e]).wait()
            return c

        lax.fori_loop(0, N_EXPERTS, fill, 0)
        lax.fori_loop(0, N_EXPERTS, drain, 0)
        used = pad_ref[N_EXPERTS]
        total = hs_ref.shape[0] // MOE_TM
        lax.fori_loop(used, total, lambda t, c: (zero_copy(t * MOE_TM).start(), c)[1], 0)
        lax.fori_loop(used, total, lambda t, c: (zero_copy(t * MOE_TM).wait(), c)[1], 0)

    def body(r, c):
        for s in range(2):
            pltpu.make_async_copy(h_ref.at[pl.ds(r, 1)], hs_ref.at[pl.ds(dest_ref[0, 0, 2 * r + s], 1)],
                                  sem).start(priority=s)
        return c

    lax.fori_loop(0, tm, body, 0, unroll=DMA_UNROLL)
    for _ in range(2):
        pltpu.make_async_copy(h_ref, hs_ref.at[pl.ds(0, tm)], sem).wait()


def _dispatch(pad_start, dest3, h2, p_rows):
    n, w = h2.shape
    tm = DISP_TM
    grid_spec = pltpu.PrefetchScalarGridSpec(
        num_scalar_prefetch=1,
        grid=(n // tm,),
        in_specs=[pl.BlockSpec((1, 1, 2 * tm), lambda i, pad: (i, 0, 0), memory_space=pltpu.SMEM),
                  pl.BlockSpec((tm, w), lambda i, pad: (i, 0))],
        out_specs=pl.BlockSpec(memory_space=pl.ANY),
        scratch_shapes=[pltpu.VMEM((MOE_TM, w), h2.dtype), pltpu.SemaphoreType.DMA(()),
                        pltpu.SemaphoreType.DMA(())],
    )
    return pl.pallas_call(
        _dispatch_kernel,
        out_shape=jax.ShapeDtypeStruct((p_rows, w), h2.dtype),
        grid_spec=grid_spec,
        compiler_params=pltpu.CompilerParams(dimension_semantics=("arbitrary",),
                                             vmem_limit_bytes=VMEM_LIMIT),
        name="dispatch",
    )(pad_start, dest3, h2)


def _tile_tables(counts, n_tiles_max):
    cnt = counts[0, :N_EXPERTS].astype(I32)
    tiles = (cnt + MOE_TM - 1) // MOE_TM
    tiles_end = jnp.cumsum(tiles)
    tile = jnp.arange(n_tiles_max, dtype=I32)
    tile_expert = jnp.sum((tile[:, None] >= tiles_end[None, :]).astype(I32), axis=1)
    n_used = tiles_end[-1:].astype(I32)
    pad_start = jnp.where(tiles > 0, (tiles_end - 1) * MOE_TM, -1).astype(I32)
    return jnp.minimum(tile_expert, N_EXPERTS - 1), n_used, jnp.concatenate([pad_start, n_used])


def kernel(x, c, w_ada, b_ada, g_norm_mix, w_in, lb_logits, g_rec_out, g_att_out, w_out, g_norm_ffn,
           w_router_group, b_router_group, w_router_expert, b_router_expert,
           w_expert_gate, w_expert_up, w_expert_down, g_final):
    bsz, seq, d = x.shape
    n = bsz * seq
    assert w_ada.shape[0] == 1, "single trunk layer"
    layer = 0
    x2 = x.reshape(n, d)

    mod = _adaln(c, w_ada[layer], b_ada[layer])
    sh1, sc1, gt1, sh2, sc2, gt2 = [m.reshape(bsz, 1, d) for m in jnp.split(mod, 6, axis=-1)]

    proj, ptail = _inproj(x2, sc1, sh1, g_norm_mix[layer].reshape(1, d), w_in[layer], seq)
    kb, vb, kxb = _kvprep(proj, ptail)
    rec = _hgrn(proj, lb_logits, g_rec_out[layer], bsz, seq, layer)
    att = _dsa(proj, ptail, kb, vb, kxb, g_att_out[layer], bsz, seq)

    wr = jnp.concatenate([w_router_group[layer], w_router_expert[layer]], axis=1)
    wr = jnp.pad(wr, ((0, 0), (0, LANES - wr.shape[1])))
    br = jnp.concatenate([b_router_group[layer], b_router_expert[layer]])
    br = jnp.pad(br, (0, LANES - br.shape[0])).reshape(1, LANES)
    x1, h2, oh0, oh1, wts = _outproj(rec, att, x2, gt1, sc2, sh2, g_norm_ffn[layer].reshape(1, d),
                                      w_out[layer].astype(BF16), wr, br, seq)

    dest, counts = _route(oh0, oh1)
    dest2 = dest[:, :2]
    p_rows = 2 * n + N_EXPERTS * MOE_TM
    tile_expert, n_tiles, pad_start = _tile_tables(counts, p_rows // MOE_TM)
    hs = _dispatch(pad_start, dest2.reshape(n // DISP_TM, 1, 2 * DISP_TM), h2, p_rows)
    ys = _moe(tile_expert, n_tiles, hs, w_expert_gate[layer], w_expert_up[layer], w_expert_down[layer])
    out = _final(dest2.reshape(n // FIN_TM, 1, 2 * FIN_TM), x1, wts, gt2, g_final.reshape(1, d), ys, seq)
    return out.reshape(bsz, seq, d)
```
